```python
import jax, jax.numpy as jnp
from jax import lax
import numpy as np

D_MODEL = 2048
BATCH = 2
SEQ = 8192
DEPTH = 2

MIX_WIDTH = D_MODEL
GROUP_WIDTH = MIX_WIDTH // 2
ATT_HEADS = 8
ATT_KV_HEADS = 2
ATT_HEAD_DIM = GROUP_WIDTH // ATT_HEADS
ATT_WINDOW = 128
ATT_BLOCK = 128
ROPE_THETA = 10000.0
ML_HEADS = 4
ML_V_DIM = GROUP_WIDTH // ML_HEADS
ML_QK_DIM = ML_V_DIM // 2
ML_CHUNK = 128
CONV_WIDTH = 3
M_INIT = -1e30
D_FF = 4 * D_MODEL
NORM_EPS = 1e-6

ATT_Q_COLS = ATT_HEADS * ATT_HEAD_DIM
ATT_KV_COLS = ATT_KV_HEADS * ATT_HEAD_DIM
ML_QK_COLS = ML_HEADS * ML_QK_DIM
ML_V_COLS = ML_HEADS * ML_V_DIM
GATE_COLS = 4 * ML_HEADS
IN_COLS = ATT_Q_COLS + 2 * ATT_KV_COLS + 2 * ML_QK_COLS + 2 * ML_V_COLS + GATE_COLS

kernel_name = "hymba_style_swa_mlstm_encoder"


def rms_norm(x, g):
    xf = x.astype(jnp.float32)
    y = xf * lax.rsqrt(jnp.mean(xf * xf, axis=-1, keepdims=True) + NORM_EPS)
    return (y * g.astype(jnp.float32)).astype(x.dtype)


def rope(x, pos):
    half = x.shape[-1] // 2
    inv_freq = ROPE_THETA ** (-jnp.arange(half, dtype=jnp.float32) / half)
    ang = pos.astype(jnp.float32)[:, None] * inv_freq[None, :]
    cos = jnp.cos(ang)[None, :, None, :]
    sin = jnp.sin(ang)[None, :, None, :]
    xf = x.astype(jnp.float32)
    x1, x2 = xf[..., :half], xf[..., half:]
    return jnp.concatenate([x1 * cos - x2 * sin, x2 * cos + x1 * sin], axis=-1).astype(x.dtype)


def window_attention(q, k, v, sink):
    B, S, Hq, D = q.shape
    Hkv = k.shape[2]
    G = Hq // Hkv
    W = ATT_BLOCK
    NB = S // W
    pos = jnp.arange(S)
    q = rope(q, pos)
    k = rope(k, pos)
    pad = ((0, 0), (W, W), (0, 0), (0, 0))
    kb = jnp.pad(k, pad).reshape(B, NB + 2, W, Hkv, D)
    vb = jnp.pad(v, pad).reshape(B, NB + 2, W, Hkv, D)
    kwin = jnp.concatenate([kb[:, :-2], kb[:, 1:-1], kb[:, 2:]], axis=2)
    vwin = jnp.concatenate([vb[:, :-2], vb[:, 1:-1], vb[:, 2:]], axis=2)
    qb = q.reshape(B, NB, W, Hkv, G, D)
    s = jnp.einsum('bnqhgd,bnkhd->bnhgqk', qb, kwin).astype(jnp.float32) * (D ** -0.5)
    qi = jnp.arange(W)[:, None]
    kj = jnp.arange(3 * W)[None, :]
    band = jnp.abs(kj - qi - W) <= ATT_WINDOW
    key_pos = jnp.arange(NB)[:, None] * W - W + jnp.arange(3 * W)[None, :]
    in_range = (key_pos >= 0) & (key_pos < S)
    mask = band[None, :, :] & in_range[:, None, :]
    s = jnp.where(mask[None, :, None, None], s, -jnp.inf)
    sink_f = sink.astype(jnp.float32).reshape(Hkv, G)[None, None, :, :, None, None]
    m = jnp.maximum(jnp.max(s, axis=-1, keepdims=True), sink_f)
    p = jnp.exp(s - m)
    p = p / (jnp.sum(p, axis=-1, keepdims=True) + jnp.exp(sink_f - m))
    o = jnp.einsum('bnhgqk,bnkhd->bnqhgd', p.astype(v.dtype), vwin)
    return o.reshape(B, S, Hq * D)


def mlstm_chunkwise(q, k, v, log_i, log_f):
    B, H, S, dk = q.shape
    dv = v.shape[-1]
    L = ML_CHUNK
    NC = S // L
    qc = q.reshape(B, H, NC, L, dk)
    kc = k.reshape(B, H, NC, L, dk)
    vc = v.reshape(B, H, NC, L, dv)
    li = log_i.reshape(B, H, NC, L)
    b = jnp.cumsum(log_f.reshape(B, H, NC, L), axis=-1)
    g = b[..., -1]
    a = g[..., None] - b + li
    m_loc = jnp.max(a, axis=-1)
    w_end = jnp.exp(a - m_loc[..., None])
    C_loc = jnp.einsum('bhcld,bhcle->bhcde', w_end[..., None] * kc, vc)
    n_loc = jnp.einsum('bhcl,bhcld->bhcd', w_end, kc)

    def step(carry, xs):
        C, n, m = carry
        g_c, m_l, C_l, n_l = xs
        m_new = jnp.maximum(g_c + m, m_l)
        s_prev = jnp.exp(g_c + m - m_new)
        s_loc = jnp.exp(m_l - m_new)
        C_new = s_prev[..., None, None] * C + s_loc[..., None, None] * C_l
        n_new = s_prev[..., None] * n + s_loc[..., None] * n_l
        return (C_new, n_new, m_new), (C, n, m)

    init = (jnp.zeros((B, H, dk, dv), jnp.float32),
            jnp.zeros((B, H, dk), jnp.float32),
            jnp.full((B, H), M_INIT, jnp.float32))
    xs = (jnp.moveaxis(g, 2, 0), jnp.moveaxis(m_loc, 2, 0),
          jnp.moveaxis(C_loc, 2, 0), jnp.moveaxis(n_loc, 2, 0))
    _, (C_prev, n_prev, m_prev) = lax.scan(step, init, xs)
    C_prev = jnp.moveaxis(C_prev, 0, 2)
    n_prev = jnp.moveaxis(n_prev, 0, 2)
    m_prev = jnp.moveaxis(m_prev, 0, 2)

    Dm = b[..., :, None] - b[..., None, :] + li[..., None, :]
    lower = jnp.tril(jnp.ones((L, L), dtype=bool))
    Dm = jnp.where(lower, Dm, -jnp.inf)
    inter = b + m_prev[..., None]
    m_t = jnp.maximum(inter, jnp.max(Dm, axis=-1))
    sc = jnp.einsum('bhctd,bhcsd->bhcts', qc, kc) * jnp.exp(Dm - m_t[..., None])
    w_inter = jnp.exp(inter - m_t)
    num = (jnp.einsum('bhcts,bhcse->bhcte', sc, vc)
           + w_inter[..., None] * jnp.einsum('bhctd,bhcde->bhcte', qc, C_prev))
    den = jnp.sum(sc, axis=-1) + w_inter * jnp.einsum('bhctd,bhcd->bhct', qc, n_prev)
    h = num / jnp.maximum(jnp.abs(den), jnp.exp(-m_t))[..., None]
    return h.reshape(B, H, S, dv)


def centred_depthwise_conv(x, w):
    K = w.shape[0]
    r = K // 2
    S = x.shape[1]
    xp = jnp.pad(x, ((0, 0), (r, r), (0, 0)))
    y = xp[:, 0:S] * w[0]
    for j in range(1, K):
        y = y + xp[:, j:j + S] * w[j]
    return y


def mlstm_mixer(mq, mk, mv, mo, gate_pre, conv_w, gate_bias, head_norm_g):
    B, S, _ = mq.shape
    qk = jax.nn.silu(centred_depthwise_conv(jnp.concatenate([mq, mk], axis=-1), conv_w))
    q, k = qk[..., :ML_QK_COLS], qk[..., ML_QK_COLS:]
    q = q.reshape(B, S, ML_HEADS, ML_QK_DIM).transpose(0, 2, 1, 3).astype(jnp.float32) * (ML_QK_DIM ** -0.5)
    k = k.reshape(B, S, ML_HEADS, ML_QK_DIM).transpose(0, 2, 1, 3).astype(jnp.float32)
    v = mv.reshape(B, S, ML_HEADS, ML_V_DIM).transpose(0, 2, 1, 3).astype(jnp.float32)
    gates = (gate_pre.astype(jnp.float32) + gate_bias.astype(jnp.float32)).reshape(B, S, 4, ML_HEADS)
    gates = jnp.transpose(gates, (2, 0, 3, 1))
    i_fwd, f_fwd, i_bwd, f_bwd = gates[0], gates[1], gates[2], gates[3]
    h_fwd = mlstm_chunkwise(q, k, v, i_fwd, jax.nn.log_sigmoid(f_fwd))
    flip = lambda t: jnp.flip(t, axis=2)
    h_bwd = flip(mlstm_chunkwise(flip(q), flip(k), flip(v), flip(i_bwd),
                                 jax.nn.log_sigmoid(flip(f_bwd))))
    h = (h_fwd + h_bwd).transpose(0, 2, 1, 3)
    h = h * lax.rsqrt(jnp.mean(h * h, axis=-1, keepdims=True) + NORM_EPS)
    h = h.reshape(B, S, ML_V_COLS) * head_norm_g.astype(jnp.float32)
    out = h * jax.nn.sigmoid(mo.astype(jnp.float32))
    return out.astype(mq.dtype)


def hybrid_layer(x, w_in, conv_w, gate_bias, ml_norm_g, attn_sink, w_out,
                 g_pre_mix, g_post_mix, g_pre_mlp, g_post_mlp, w_up, w_down):
    B, S, _ = x.shape
    h = rms_norm(x, g_pre_mix)
    proj = h @ w_in
    sizes = [ATT_Q_COLS, ATT_KV_COLS, ATT_KV_COLS, ML_QK_COLS, ML_QK_COLS, ML_V_COLS, ML_V_COLS]
    cuts = []
    acc = 0
    for sz in sizes:
        acc += sz
        cuts.append(acc)
    aq, ak, av, mq, mk, mv, mo, mg = jnp.split(proj, cuts, axis=-1)
    att = window_attention(aq.reshape(B, S, ATT_HEADS, ATT_HEAD_DIM),
                           ak.reshape(B, S, ATT_KV_HEADS, ATT_HEAD_DIM),
                           av.reshape(B, S, ATT_KV_HEADS, ATT_HEAD_DIM),
                           attn_sink)
    mem = mlstm_mixer(mq, mk, mv, mo, mg, conv_w, gate_bias, ml_norm_g)
    mix = jnp.concatenate([att, mem.astype(att.dtype)], axis=-1) @ w_out
    x = x + rms_norm(mix, g_post_mix)
    h = rms_norm(x, g_pre_mlp)
    u = jnp.square(jax.nn.relu(h @ w_up))
    x = x + rms_norm(u @ w_down, g_post_mlp)
    return x


def setup_inputs(seed: int = 0) -> dict:
    key = jax.random.key(seed)
    ks = jax.random.split(key, 16)
    f32 = jnp.float32
    x = jax.random.normal(ks[0], (BATCH, SEQ, D_MODEL), f32)
    w_in = jax.random.normal(ks[1], (DEPTH, D_MODEL, IN_COLS), f32) * D_MODEL ** -0.5
    conv_w = jax.random.normal(ks[2], (DEPTH, CONV_WIDTH, 2 * ML_QK_COLS), f32) * CONV_WIDTH ** -0.5
    gk = jax.random.split(ks[3], 4)
    i_bias = 0.1 * jax.random.normal(gk[0], (DEPTH, 2, ML_HEADS), f32)
    f_bias = (jnp.linspace(3.0, 6.0, ML_HEADS, dtype=f32)[None, None, :]
              + 0.1 * jax.random.normal(gk[1], (DEPTH, 2, ML_HEADS), f32))
    gate_bias = jnp.stack([i_bias[:, 0], f_bias[:, 0], i_bias[:, 1], f_bias[:, 1]], axis=1).reshape(DEPTH, GATE_COLS)
    ml_norm_g = 1.0 + 0.05 * jax.random.normal(ks[4], (DEPTH, ML_V_COLS), f32)
    attn_sink = 0.5 * jax.random.normal(ks[5], (DEPTH, ATT_HEADS), f32)
    w_out = jax.random.normal(ks[6], (DEPTH, MIX_WIDTH, D_MODEL), f32) * MIX_WIDTH ** -0.5
    g_pre_mix = 1.0 + 0.05 * jax.random.normal(ks[7], (DEPTH, D_MODEL), f32)
    g_post_mix = 1.0 + 0.05 * jax.random.normal(ks[8], (DEPTH, D_MODEL), f32)
    g_pre_mlp = 1.0 + 0.05 * jax.random.normal(ks[9], (DEPTH, D_MODEL), f32)
    g_post_mlp = 1.0 + 0.05 * jax.random.normal(ks[10], (DEPTH, D_MODEL), f32)
    w_up = jax.random.normal(ks[11], (DEPTH, D_MODEL, D_FF), f32) * D_MODEL ** -0.5
    w_down = jax.random.normal(ks[12], (DEPTH, D_FF, D_MODEL), f32) * D_FF ** -0.5
    return {"x": x, "w_in": w_in, "conv_w": conv_w, "gate_bias": gate_bias,
            "ml_norm_g": ml_norm_g, "attn_sink": attn_sink, "w_out": w_out,
            "g_pre_mix": g_pre_mix, "g_post_mix": g_post_mix,
            "g_pre_mlp": g_pre_mlp, "g_post_mlp": g_post_mlp,
            "w_up": w_up, "w_down": w_down}


def reference(x, w_in, conv_w, gate_bias, ml_norm_g, attn_sink, w_out,
              g_pre_mix, g_post_mix, g_pre_mlp, g_post_mlp, w_up, w_down):
    for l in range(DEPTH):
        x = hybrid_layer(x, w_in[l], conv_w[l], gate_bias[l], ml_norm_g[l], attn_sink[l], w_out[l],
                         g_pre_mix[l], g_post_mix[l], g_pre_mlp[l], g_post_mlp[l], w_up[l], w_down[l])
    return x
```

```python
import functools

import jax
import jax.numpy as jnp
from jax import lax
from jax.experimental import pallas as pl
from jax.experimental.pallas import tpu as pltpu

F32 = jnp.float32
BF16 = jnp.bfloat16

D_MODEL = 2048
ATT_HEADS = 8
ATT_KV_HEADS = 2
ATT_GROUP = ATT_HEADS // ATT_KV_HEADS
HEAD_DIM = 128
ATT_WINDOW = 128
ROPE_THETA = 10000.0
ML_HEADS = 4
ML_V_DIM = 256
ML_QK_DIM = 128
ML_CHUNK = 128
M_INIT = -1e30
D_FF = 4 * D_MODEL
NORM_EPS = 1e-6
GATE_COLS = 4 * ML_HEADS

ATT_Q_COLS = ATT_HEADS * HEAD_DIM
ATT_KV_COLS = ATT_KV_HEADS * HEAD_DIM
ML_QK_COLS = ML_HEADS * ML_QK_DIM
ML_V_COLS = ML_HEADS * ML_V_DIM

C_AQ = 0
C_MV = C_AQ + ATT_Q_COLS
C_MO = C_MV + ML_V_COLS
C_MQK = C_MO + ML_V_COLS
C_AK = C_MQK + 2 * ML_QK_COLS
C_AV = C_AK + ATT_KV_COLS
PROJ_COLS = C_AV + ATT_KV_COLS
LANES = 128

VMEM_LIMIT = 56 * 1024 * 1024

IN_TM = 512
IN_TN = 512
ATT_TQ = 512
MIX_TM = 512
MLP_TM = 512
MLP_TF = 1024

_NT = (((1,), (1,)), ((), ()))
_TN = (((0,), (0,)), ((), ()))


def _sigmoid(x):
    return 1.0 / (1.0 + jnp.exp(-x))


def _log_sigmoid(x):
    return jnp.minimum(x, 0.0) - jnp.log(1.0 + jnp.exp(-jnp.abs(x)))


def _rms(x):
    return x * lax.rsqrt(jnp.mean(x * x, axis=-1, keepdims=True) + NORM_EPS)


def _in_proj_kernel(x_ref, g_ref, w_ref, wg_ref, wgt_ref, cos_ref, sin_ref,
                    proj_ref, gate_ref, gatet_ref, h_ref):
    h_ref[...] = (_rms(x_ref[...]) * g_ref[...]).astype(BF16)
    cos = cos_ref[...]
    sin = sin_ref[...]

    def rope(a):
        return a * cos + pltpu.roll(a, HEAD_DIM // 2, 1) * sin

    q_scale = HEAD_DIM ** -0.5
    for c in range(PROJ_COLS // IN_TN):
        c0 = c * IN_TN
        acc = jnp.dot(h_ref[...], w_ref[:, c0:c0 + IN_TN], preferred_element_type=F32)
        for k in range(IN_TN // HEAD_DIM):
            lo = c0 + k * HEAD_DIM
            a = acc[:, k * HEAD_DIM:(k + 1) * HEAD_DIM]
            if lo < C_AQ + ATT_Q_COLS:
                a = rope(a) * q_scale
            elif C_AK <= lo < C_AK + ATT_KV_COLS:
                a = rope(a)
            proj_ref[:, lo:lo + HEAD_DIM] = a.astype(BF16)
    gate_ref[...] = jnp.dot(h_ref[...], wg_ref[...], preferred_element_type=F32)
    gatet_ref[...] = lax.dot_general(wgt_ref[...], h_ref[...], _NT, preferred_element_type=F32)


def _in_proj(x2d, g, w_main, wg, wgt, cos_t, sin_t, seq):
    t = x2d.shape[0]
    pos_blocks = seq // IN_TM
    const = lambda i: (0, 0)
    return pl.pallas_call(
        _in_proj_kernel,
        grid=(t // IN_TM,),
        in_specs=[
            pl.BlockSpec((IN_TM, D_MODEL), lambda i: (i, 0)),
            pl.BlockSpec((1, D_MODEL), const),
            pl.BlockSpec((D_MODEL, PROJ_COLS), const, pipeline_mode=pl.Buffered(1)),
            pl.BlockSpec((D_MODEL, LANES), const),
            pl.BlockSpec((GATE_COLS, D_MODEL), const),
            pl.BlockSpec((IN_TM, HEAD_DIM), lambda i: (i % pos_blocks, 0)),
            pl.BlockSpec((IN_TM, HEAD_DIM), lambda i: (i % pos_blocks, 0)),
        ],
        out_specs=[
            pl.BlockSpec((IN_TM, PROJ_COLS), lambda i: (i, 0)),
            pl.BlockSpec((IN_TM, LANES), lambda i: (i, 0)),
            pl.BlockSpec((GATE_COLS, IN_TM), lambda i: (0, i)),
        ],
        out_shape=[
            jax.ShapeDtypeStruct((t, PROJ_COLS), BF16),
            jax.ShapeDtypeStruct((t, LANES), F32),
            jax.ShapeDtypeStruct((GATE_COLS, t), F32),
        ],
        scratch_shapes=[pltpu.VMEM((IN_TM, D_MODEL), BF16)],
        compiler_params=pltpu.CompilerParams(
            dimension_semantics=("parallel",), vmem_limit_bytes=VMEM_LIMIT),
        name="in_proj",
    )(x2d, g, w_main, wg, wgt, cos_t, sin_t)


def _attn_kernel(sink_ref, q_ref, kc_ref, kp_ref, kn_ref, vc_ref, vp_ref, vn_ref,
                 o_ref, kbuf, vbuf, *, seq):
    w = ATT_WINDOW
    kbuf[0:w] = kp_ref[...]
    kbuf[w:w + ATT_TQ] = kc_ref[...]
    kbuf[w + ATT_TQ:] = kn_ref[...]
    vbuf[0:w] = vp_ref[...]
    vbuf[w:w + ATT_TQ] = vc_ref[...]
    vbuf[w + ATT_TQ:] = vn_ref[...]
    t0 = pl.program_id(1) * ATT_TQ
    rows = ATT_GROUP * w
    qi = lax.broadcasted_iota(jnp.int32, (rows, 1), 0) & (w - 1)
    blk = lax.broadcasted_iota(jnp.int32, (rows, 1), 0) // w
    kj = lax.broadcasted_iota(jnp.int32, (rows, 3 * w), 1)
    for n in range(ATT_TQ // w):
        base = t0 + (n - 1) * w
        lo = jnp.maximum(qi, -base)
        hi = jnp.minimum(qi + 2 * w, seq - 1 - base)
        valid = (kj >= lo) & (kj <= hi)
        for h in range(ATT_KV_HEADS):
            qs = jnp.concatenate(
                [q_ref[n * w:(n + 1) * w, (h * ATT_GROUP + g) * HEAD_DIM:(h * ATT_GROUP + g + 1) * HEAD_DIM]
                 for g in range(ATT_GROUP)], axis=0)
            kw = kbuf[n * w:(n + 3) * w, h * HEAD_DIM:(h + 1) * HEAD_DIM]
            vw = vbuf[n * w:(n + 3) * w, h * HEAD_DIM:(h + 1) * HEAD_DIM]
            s = lax.dot_general(qs, kw, _NT, preferred_element_type=F32)
            s = jnp.where(valid, s, -jnp.inf)
            sink = jnp.full((rows, 1), sink_ref[h * ATT_GROUP], F32)
            for g in range(1, ATT_GROUP):
                sink = jnp.where(blk == g, sink_ref[h * ATT_GROUP + g], sink)
            m = jnp.maximum(jnp.max(s, axis=-1, keepdims=True), sink)
            p = jnp.exp(s - m)
            denom = jnp.sum(p, axis=-1, keepdims=True) + jnp.exp(sink - m)
            o = jnp.dot(p.astype(BF16), vw, preferred_element_type=F32) / denom
            for g in range(ATT_GROUP):
                col = (h * ATT_GROUP + g) * HEAD_DIM
                o_ref[n * w:(n + 1) * w, col:col + HEAD_DIM] = o[g * w:(g + 1) * w].astype(BF16)


def _attention(proj, sink, batch, seq):
    t = proj.shape[0]
    w = ATT_WINDOW
    nq = seq // ATT_TQ
    per = ATT_TQ // w
    last_blk = t // w - 1
    cur = lambda b, i: b * nq + i
    prev = lambda b, i: jnp.maximum((b * nq + i) * per - 1, 0)
    nxt = lambda b, i: jnp.minimum((b * nq + i + 1) * per, last_blk)
    kcol = C_AK // ATT_KV_COLS
    vcol = C_AV // ATT_KV_COLS
    return pl.pallas_call(
        functools.partial(_attn_kernel, seq=seq),
        grid=(batch, nq),
        in_specs=[
            pl.BlockSpec(memory_space=pltpu.SMEM),
            pl.BlockSpec((ATT_TQ, ATT_Q_COLS), lambda b, i: (cur(b, i), C_AQ // ATT_Q_COLS)),
            pl.BlockSpec((ATT_TQ, ATT_KV_COLS), lambda b, i: (cur(b, i), kcol)),
            pl.BlockSpec((w, ATT_KV_COLS), lambda b, i: (prev(b, i), kcol)),
            pl.BlockSpec((w, ATT_KV_COLS), lambda b, i: (nxt(b, i), kcol)),
            pl.BlockSpec((ATT_TQ, ATT_KV_COLS), lambda b, i: (cur(b, i), vcol)),
            pl.BlockSpec((w, ATT_KV_COLS), lambda b, i: (prev(b, i), vcol)),
            pl.BlockSpec((w, ATT_KV_COLS), lambda b, i: (nxt(b, i), vcol)),
        ],
        out_specs=pl.BlockSpec((ATT_TQ, ATT_Q_COLS), lambda b, i: (cur(b, i), 0)),
        out_shape=jax.ShapeDtypeStruct((t, ATT_Q_COLS), BF16),
        scratch_shapes=[pltpu.VMEM((ATT_TQ + 2 * w, ATT_KV_COLS), BF16),
                        pltpu.VMEM((ATT_TQ + 2 * w, ATT_KV_COLS), BF16)],
        compiler_params=pltpu.CompilerParams(
            dimension_semantics=("parallel", "parallel"), vmem_limit_bytes=VMEM_LIMIT),
        name="attn",
    )(sink, proj, proj, proj, proj, proj, proj, proj)


def _mlstm_kernel(qkf_ref, qkfp_ref, qkfn_ref, vf_ref, gf_ref, gtf_ref,
                  qkb_ref, qkbp_ref, qkbn_ref, vb_ref, gb_ref, gtb_ref,
                  convw_ref, brow_ref, bcol_ref,
                  hf_ref, hb_ref, c_sc, n_sc, m_sc, *, n_chunks):
    L = ML_CHUNK
    c = pl.program_id(1)

    @pl.when(c == 0)
    def _():
        c_sc[...] = jnp.zeros_like(c_sc)
        n_sc[...] = jnp.zeros_like(n_sc)
        m_sc[...] = jnp.full_like(m_sc, M_INIT)

    ti = lax.broadcasted_iota(jnp.int32, (L, L), 0)
    si = lax.broadcasted_iota(jnp.int32, (L, L), 1)
    low = si <= ti
    upp = si >= ti
    low_f = low.astype(F32)
    upp_f = upp.astype(F32)
    rowid = lax.broadcasted_iota(jnp.int32, (L, 1), 0)
    w0 = convw_ref[0:1, :]
    w1 = convw_ref[1:2, :]
    w2 = convw_ref[2:3, :]
    hp = lax.Precision.HIGHEST

    dirs = ((qkf_ref, qkfp_ref, qkfn_ref, vf_ref, gf_ref, gtf_ref, hf_ref),
            (qkb_ref, qkbp_ref, qkbn_ref, vb_ref, gb_ref, gtb_ref, hb_ref))
    for d, (qk_ref, qkp_ref, qkn_ref, v_ref, g_ref, gt_ref, out_ref) in enumerate(dirs):
        chunk = c if d == 0 else n_chunks - 1 - c
        x = qk_ref[...].astype(F32)
        xp = qkp_ref[7:8, :].astype(F32) * (chunk > 0).astype(F32)
        xn = qkn_ref[0:1, :].astype(F32) * (chunk < n_chunks - 1).astype(F32)
        x_prev = jnp.where(rowid == 0, xp, pltpu.roll(x, 1, 0))
        x_next = jnp.where(rowid == L - 1, xn, pltpu.roll(x, L - 1, 0))
        y = x_prev * w0 + x * w1 + x_next * w2
        y = y * _sigmoid(y)

        gates = g_ref[...] + brow_ref[...]
        gates_t = gt_ref[...] + bcol_ref[...]
        ls = _log_sigmoid(gates)
        ls_t = _log_sigmoid(gates_t)
        tri_col, tri_row, mask = (low_f, upp_f, low) if d == 0 else (upp_f, low_f, upp)
        cum = jnp.dot(tri_col, ls, precision=hp, preferred_element_type=F32)
        cum_t = jnp.dot(ls_t, tri_row, precision=hp, preferred_element_type=F32)

        for j in range(ML_HEADS):
            r = d * ML_HEADS + j
            ci = d * 2 * ML_HEADS + j
            cf = ci + ML_HEADS
            li_col = gates[:, ci:ci + 1]
            b_col = cum[:, cf:cf + 1]
            li_row = gates_t[ci:ci + 1, :]
            b_row = cum_t[cf:cf + 1, :]
            gtot = jnp.sum(ls_t[cf:cf + 1, :], axis=1, keepdims=True)
            m_prev = m_sc[r:r + 1, 0:1]

            q = (y[:, j * ML_QK_DIM:(j + 1) * ML_QK_DIM] * (ML_QK_DIM ** -0.5))
            k = y[:, ML_QK_COLS + j * ML_QK_DIM:ML_QK_COLS + (j + 1) * ML_QK_DIM]
            v = v_ref[:, j * ML_V_DIM:(j + 1) * ML_V_DIM]
            qb = q.astype(BF16)
            kb = k.astype(BF16)

            a_col = gtot - b_col + li_col
            m_loc = jnp.max(a_col, axis=0, keepdims=True)
            kw = k * jnp.exp(a_col - m_loc)
            c_loc = lax.dot_general(kw.astype(BF16), v, _TN, preferred_element_type=F32)
            n_loc = jnp.sum(kw, axis=0, keepdims=True)

            dm = jnp.where(mask, b_col - b_row + li_row, -jnp.inf)
            inter = b_col + m_prev
            m_t = jnp.maximum(inter, jnp.max(dm, axis=1, keepdims=True))
            sc = lax.dot_general(qb, kb, _NT, preferred_element_type=F32) * jnp.exp(dm - m_t)
            w_inter = jnp.exp(inter - m_t)
            c_prev = c_sc[r]
            n_prev = n_sc[r:r + 1, :]
            num = (jnp.dot(sc.astype(BF16), v, preferred_element_type=F32)
                   + w_inter * jnp.dot(qb, c_prev.astype(BF16), preferred_element_type=F32))
            den = (jnp.sum(sc, axis=1, keepdims=True)
                   + w_inter * jnp.sum(q * n_prev, axis=1, keepdims=True))
            out_ref[:, j * ML_V_DIM:(j + 1) * ML_V_DIM] = num / jnp.maximum(jnp.abs(den), jnp.exp(-m_t))

            m_new = jnp.maximum(gtot + m_prev, m_loc)
            s_prev = jnp.exp(gtot + m_prev - m_new)
            s_loc = jnp.exp(m_loc - m_new)
            c_sc[r] = s_prev * c_prev + s_loc * c_loc
            n_sc[r:r + 1, :] = s_prev * n_prev + s_loc * n_loc
            m_sc[r:r + 1, :] = jnp.broadcast_to(m_new, (1, LANES))


def _mlstm(proj, gates, gates_t, conv_w, bias_row, bias_col, batch, seq):
    t = proj.shape[0]
    L = ML_CHUNK
    nc = seq // L
    sub = 8
    last8 = t // sub - 1
    fwd = lambda b, c: b * nc + c
    bwd = lambda b, c: b * nc + nc - 1 - c
    qk_col = C_MQK // (2 * ML_QK_COLS)
    v_col = C_MV // ML_V_COLS

    def dir_specs(ch):
        return [
            pl.BlockSpec((L, 2 * ML_QK_COLS), lambda b, c: (ch(b, c), qk_col)),
            pl.BlockSpec((sub, 2 * ML_QK_COLS),
                         lambda b, c: (jnp.maximum(ch(b, c) * (L // sub) - 1, 0), qk_col)),
            pl.BlockSpec((sub, 2 * ML_QK_COLS),
                         lambda b, c: (jnp.minimum((ch(b, c) + 1) * (L // sub), last8), qk_col)),
            pl.BlockSpec((L, ML_V_COLS), lambda b, c: (ch(b, c), v_col)),
            pl.BlockSpec((L, LANES), lambda b, c: (ch(b, c), 0)),
            pl.BlockSpec((GATE_COLS, L), lambda b, c: (0, ch(b, c))),
        ]

    const = lambda b, c: (0, 0)
    n_states = 2 * ML_HEADS
    return pl.pallas_call(
        functools.partial(_mlstm_kernel, n_chunks=nc),
        grid=(batch, nc),
        in_specs=dir_specs(fwd) + dir_specs(bwd) + [
            pl.BlockSpec((3, 2 * ML_QK_COLS), const),
            pl.BlockSpec((1, LANES), const),
            pl.BlockSpec((GATE_COLS, 1), const),
        ],
        out_specs=[pl.BlockSpec((L, ML_V_COLS), lambda b, c: (fwd(b, c), 0)),
                   pl.BlockSpec((L, ML_V_COLS), lambda b, c: (bwd(b, c), 0))],
        out_shape=[jax.ShapeDtypeStruct((t, ML_V_COLS), F32),
                   jax.ShapeDtypeStruct((t, ML_V_COLS), F32)],
        scratch_shapes=[pltpu.VMEM((n_states, ML_QK_DIM, ML_V_DIM), F32),
                        pltpu.VMEM((n_states, ML_QK_DIM), F32),
                        pltpu.VMEM((n_states, LANES), F32)],
        compiler_params=pltpu.CompilerParams(
            dimension_semantics=("parallel", "arbitrary"), vmem_limit_bytes=VMEM_LIMIT),
        name="mlstm",
    )(proj, proj, proj, proj, gates, gates_t,
      proj, proj, proj, proj, gates, gates_t,
      conv_w, bias_row, bias_col)


def _mix_kernel(att_ref, hf_ref, hb_ref, mo_ref, x_ref, w_ref, gml_ref, gpost_ref, o_ref, cat_ref):
    cat_ref[:, 0:ATT_Q_COLS] = att_ref[...]
    for j in range(ML_HEADS):
        sl = slice(j * ML_V_DIM, (j + 1) * ML_V_DIM)
        h = _rms(hf_ref[:, sl] + hb_ref[:, sl]) * gml_ref[:, sl]
        out = h * _sigmoid(mo_ref[:, sl].astype(F32))
        cat_ref[:, ATT_Q_COLS + j * ML_V_DIM:ATT_Q_COLS + (j + 1) * ML_V_DIM] = out.astype(BF16)
    mix = jnp.dot(cat_ref[...], w_ref[...], preferred_element_type=F32)
    o_ref[...] = x_ref[...] + _rms(mix) * gpost_ref[...]


def _mix(att, hf, hb, proj, x2d, w_out, g_ml, g_post):
    t = x2d.shape[0]
    row = lambda i: (i, 0)
    const = lambda i: (0, 0)
    return pl.pallas_call(
        _mix_kernel,
        grid=(t // MIX_TM,),
        in_specs=[
            pl.BlockSpec((MIX_TM, ATT_Q_COLS), row),
            pl.BlockSpec((MIX_TM, ML_V_COLS), row),
            pl.BlockSpec((MIX_TM, ML_V_COLS), row),
            pl.BlockSpec((MIX_TM, ML_V_COLS), lambda i: (i, C_MO // ML_V_COLS)),
            pl.BlockSpec((MIX_TM, D_MODEL), row),
            pl.BlockSpec((D_MODEL, D_MODEL), const, pipeline_mode=pl.Buffered(1)),
            pl.BlockSpec((1, ML_V_COLS), const),
            pl.BlockSpec((1, D_MODEL), const),
        ],
        out_specs=pl.BlockSpec((MIX_TM, D_MODEL), row),
        out_shape=jax.ShapeDtypeStruct((t, D_MODEL), F32),
        scratch_shapes=[pltpu.VMEM((MIX_TM, D_MODEL), BF16)],
        compiler_params=pltpu.CompilerParams(
            dimension_semantics=("parallel",), vmem_limit_bytes=VMEM_LIMIT),
        name="mix",
    )(att, hf, hb, proj, x2d, w_out, g_ml, g_post)


def _mlp_kernel(x_ref, gpre_ref, wup_ref, wdown_ref, gpost_ref, o_ref, h_ref):
    j = pl.program_id(1)

    @pl.when(j == 0)
    def _():
        h_ref[...] = (_rms(x_ref[...]) * gpre_ref[...]).astype(BF16)

    u = jnp.dot(h_ref[...], wup_ref[...], preferred_element_type=F32)
    u = jnp.square(jnp.maximum(u, 0.0)).astype(BF16)
    part = jnp.dot(u, wdown_ref[...], preferred_element_type=F32)

    @pl.when(j == 0)
    def _():
        o_ref[...] = part

    @pl.when(j > 0)
    def _():
        o_ref[...] += part

    @pl.when(j == pl.num_programs(1) - 1)
    def _():
        o_ref[...] = x_ref[...] + _rms(o_ref[...]) * gpost_ref[...]


def _mlp(x2d, g_pre, w_up, w_down, g_post):
    t = x2d.shape[0]
    return pl.pallas_call(
        _mlp_kernel,
        grid=(t // MLP_TM, D_FF // MLP_TF),
        in_specs=[
            pl.BlockSpec((MLP_TM, D_MODEL), lambda i, j: (i, 0)),
            pl.BlockSpec((1, D_MODEL), lambda i, j: (0, 0)),
            pl.BlockSpec((D_MODEL, MLP_TF), lambda i, j: (0, j)),
            pl.BlockSpec((MLP_TF, D_MODEL), lambda i, j: (j, 0)),
            pl.BlockSpec((1, D_MODEL), lambda i, j: (0, 0)),
        ],
        out_specs=pl.BlockSpec((MLP_TM, D_MODEL), lambda i, j: (i, 0)),
        out_shape=jax.ShapeDtypeStruct((t, D_MODEL), F32),
        scratch_shapes=[pltpu.VMEM((MLP_TM, D_MODEL), BF16)],
        compiler_params=pltpu.CompilerParams(
            dimension_semantics=("parallel", "arbitrary"), vmem_limit_bytes=VMEM_LIMIT),
        name="mlp",
    )(x2d, g_pre, w_up, w_down, g_post)


def _rope_tables(seq):
    half = HEAD_DIM // 2
    inv_freq = ROPE_THETA ** (-jnp.arange(half, dtype=F32) / half)
    ang = jnp.arange(seq, dtype=F32)[:, None] * inv_freq[None, :]
    cos = jnp.cos(ang)
    sin = jnp.sin(ang)
    return jnp.concatenate([cos, cos], axis=1), jnp.concatenate([-sin, sin], axis=1)


def _split_w_in(w):
    sizes = [ATT_Q_COLS, ATT_KV_COLS, ATT_KV_COLS, ML_QK_COLS, ML_QK_COLS, ML_V_COLS, ML_V_COLS]
    offs = [0]
    for s in sizes:
        offs.append(offs[-1] + s)
    aq, ak, av, mq, mk, mv, mo = [w[:, offs[i]:offs[i + 1]] for i in range(7)]
    w_main = jnp.concatenate([aq, mv, mo, mq, mk, ak, av], axis=1).astype(BF16)
    w_gate = w[:, offs[-1]:]
    wg = jnp.pad(w_gate, ((0, 0), (0, LANES - GATE_COLS))).astype(BF16)
    wgt = w_gate.T.astype(BF16)
    return w_main, wg, wgt


def kernel(x, w_in, conv_w, gate_bias, ml_norm_g, attn_sink, w_out, g_pre_mix, g_post_mix,
           g_pre_mlp, g_post_mlp, w_up, w_down):
    batch, seq, d = x.shape
    depth = w_in.shape[0]
    cos_t, sin_t = _rope_tables(seq)
    x2d = x.reshape(batch * seq, d)
    for l in range(depth):
        w_main, wg, wgt = _split_w_in(w_in[l])
        proj, gates, gates_t = _in_proj(x2d, g_pre_mix[l][None, :], w_main, wg, wgt, cos_t, sin_t, seq)
        att = _attention(proj, attn_sink[l], batch, seq)
        bias_row = jnp.pad(gate_bias[l], (0, LANES - GATE_COLS))[None, :]
        bias_col = gate_bias[l][:, None]
        hf, hb = _mlstm(proj, gates, gates_t, conv_w[l], bias_row, bias_col, batch, seq)
        x2d = _mix(att, hf, hb, proj, x2d, w_out[l].astype(BF16), ml_norm_g[l][None, :],
                   g_post_mix[l][None, :])
        x2d = _mlp(x2d, g_pre_mlp[l][None, :], w_up[l].astype(BF16), w_down[l].astype(BF16),
                   g_post_mlp[l][None, :])
    return x2d.reshape(batch, seq, d)
```

```python
import functools

import jax
import jax.numpy as jnp
from jax import lax
from jax.experimental import pallas as pl
from jax.experimental.pallas import tpu as pltpu

F32 = jnp.float32
BF16 = jnp.bfloat16

D_MODEL = 2048
ATT_HEADS = 8
ATT_KV_HEADS = 2
ATT_GROUP = ATT_HEADS // ATT_KV_HEADS
HEAD_DIM = 128
ATT_WINDOW = 128
ROPE_THETA = 10000.0
ML_HEADS = 4
ML_V_DIM = 256
ML_QK_DIM = 128
ML_CHUNK = 128
M_INIT = -1e30
D_FF = 4 * D_MODEL
NORM_EPS = 1e-6
GATE_COLS = 4 * ML_HEADS

ATT_Q_COLS = ATT_HEADS * HEAD_DIM
ATT_KV_COLS = ATT_KV_HEADS * HEAD_DIM
ML_QK_COLS = ML_HEADS * ML_QK_DIM
ML_V_COLS = ML_HEADS * ML_V_DIM

C_AQ = 0
C_MQK = C_AQ + ATT_Q_COLS
C_AK = C_MQK + 2 * ML_QK_COLS
C_AV = C_AK + ATT_KV_COLS
PROJ_COLS = C_AV + ATT_KV_COLS
R_MV = 0
R_MO = R_MV + ML_V_COLS
PROJT_ROWS = R_MO + ML_V_COLS
LANES = 128
SUBLANES = 8
BF16_ROWS = 16

VMEM_LIMIT = 56 * 1024 * 1024

IN_TM = 512
IN_TN = 512
CONV_TM = 1024
CONV_SUB = 256
ATT_TQ = 512
MIX_TM = 512
MLP_TM = 512
MLP_TF = 1024
ML_STATE_ROWS = ML_V_DIM + BF16_ROWS

_NT = (((1,), (1,)), ((), ()))
_TN = (((0,), (0,)), ((), ()))


def _sigmoid(x):
    return 1.0 / (1.0 + jnp.exp(-x))


def _log_sigmoid(x):
    return jnp.minimum(x, 0.0) - jnp.log(1.0 + jnp.exp(-jnp.abs(x)))


def _rms(x, axis=-1):
    return x * lax.rsqrt(jnp.mean(x * x, axis=axis, keepdims=True) + NORM_EPS)


def _in_proj_kernel(x_ref, g_ref, w_ref, wt_ref, gbias_ref, cos_ref, sin_ref,
                    proj_ref, projt_ref, gatet_ref, h_ref):
    h_ref[...] = (_rms(x_ref[...]) * g_ref[...]).astype(BF16)
    cos = cos_ref[...]
    sin = sin_ref[...]

    def rope(a):
        return a * cos + pltpu.roll(a, HEAD_DIM // 2, 1) * sin

    q_scale = HEAD_DIM ** -0.5
    for c in range(PROJ_COLS // IN_TN):
        c0 = c * IN_TN
        acc = jnp.dot(h_ref[...], w_ref[:, c0:c0 + IN_TN], preferred_element_type=F32)
        for k in range(IN_TN // HEAD_DIM):
            lo = c0 + k * HEAD_DIM
            a = acc[:, k * HEAD_DIM:(k + 1) * HEAD_DIM]
            if lo < C_AQ + ATT_Q_COLS:
                a = rope(a) * q_scale
            elif C_AK <= lo < C_AK + ATT_KV_COLS:
                a = rope(a)
            proj_ref[:, lo:lo + HEAD_DIM] = a.astype(BF16)
    n_chunks = PROJT_ROWS // IN_TN
    for c in range(n_chunks):
        r0 = c * IN_TN
        rows = IN_TN + (GATE_COLS if c == n_chunks - 1 else 0)
        acc = lax.dot_general(wt_ref[r0:r0 + rows, :], h_ref[...], _NT, preferred_element_type=F32)
        projt_ref[r0:r0 + IN_TN, :] = acc[:IN_TN].astype(BF16)
        if c == n_chunks - 1:
            gatet_ref[...] = acc[IN_TN:] + gbias_ref[...]


def _in_proj(x2d, g, w_rm, w_t, gate_bias_col, cos_t, sin_t, seq):
    t = x2d.shape[0]
    pos_blocks = seq // IN_TM
    const = lambda i: (0, 0)
    return pl.pallas_call(
        _in_proj_kernel,
        grid=(t // IN_TM,),
        in_specs=[
            pl.BlockSpec((IN_TM, D_MODEL), lambda i: (i, 0)),
            pl.BlockSpec((1, D_MODEL), const),
            pl.BlockSpec((D_MODEL, PROJ_COLS), const, pipeline_mode=pl.Buffered(1)),
            pl.BlockSpec((PROJT_ROWS + GATE_COLS, D_MODEL), const, pipeline_mode=pl.Buffered(1)),
            pl.BlockSpec((GATE_COLS, 1), const),
            pl.BlockSpec((IN_TM, HEAD_DIM), lambda i: (i % pos_blocks, 0)),
            pl.BlockSpec((IN_TM, HEAD_DIM), lambda i: (i % pos_blocks, 0)),
        ],
        out_specs=[
            pl.BlockSpec((IN_TM, PROJ_COLS), lambda i: (i, 0)),
            pl.BlockSpec((PROJT_ROWS, IN_TM), lambda i: (0, i)),
            pl.BlockSpec((GATE_COLS, IN_TM), lambda i: (0, i)),
        ],
        out_shape=[
            jax.ShapeDtypeStruct((t, PROJ_COLS), BF16),
            jax.ShapeDtypeStruct((PROJT_ROWS, t), BF16),
            jax.ShapeDtypeStruct((GATE_COLS, t), F32),
        ],
        scratch_shapes=[pltpu.VMEM((IN_TM, D_MODEL), BF16)],
        compiler_params=pltpu.CompilerParams(
            dimension_semantics=("parallel",), vmem_limit_bytes=VMEM_LIMIT),
        name="in_proj",
    )(x2d, g, w_rm, w_t, gate_bias_col, cos_t, sin_t)


def _qk_conv_kernel(x_ref, xp_ref, xn_ref, w_ref, k_ref, qt_ref, *, seq_blocks):
    n = CONV_SUB
    pos = pl.program_id(0) % seq_blocks
    has_prev = (pos > 0).astype(F32)
    has_next = (pos < seq_blocks - 1).astype(F32)
    ri = lax.broadcasted_iota(jnp.int32, (n, n), 0)
    ci = lax.broadcasted_iota(jnp.int32, (n, n), 1)
    shift_prev = (ci == ri - 1).astype(BF16)
    shift_next = (ci == ri + 1).astype(BF16)
    rowid = lax.broadcasted_iota(jnp.int32, (n, 1), 0)
    w0 = w_ref[0:1, :]
    w1 = w_ref[1:2, :]
    w2 = w_ref[2:3, :]
    n_sub = CONV_TM // n
    for sb in range(n_sub):
        xs = x_ref[sb * n:(sb + 1) * n, :]
        x_prev = jnp.dot(shift_prev, xs, preferred_element_type=F32)
        x_next = jnp.dot(shift_next, xs, preferred_element_type=F32)
        if sb == 0:
            prev_row = xp_ref[BF16_ROWS - 1:BF16_ROWS, :].astype(F32) * has_prev
        else:
            prev_row = x_ref[sb * n - BF16_ROWS:sb * n, :].astype(F32)[BF16_ROWS - 1:BF16_ROWS]
        if sb == n_sub - 1:
            next_row = xn_ref[0:1, :].astype(F32) * has_next
        else:
            next_row = x_ref[(sb + 1) * n:(sb + 1) * n + BF16_ROWS, :].astype(F32)[0:1]
        x_prev = jnp.where(rowid == 0, prev_row, x_prev)
        x_next = jnp.where(rowid == n - 1, next_row, x_next)
        y = x_prev * w0 + xs.astype(F32) * w1 + x_next * w2
        y = y * _sigmoid(y)
        k_ref[sb * n:(sb + 1) * n, :] = y[:, ML_QK_COLS:].astype(BF16)
        q = y[:, :ML_QK_COLS] * (ML_QK_DIM ** -0.5)
        qt_ref[:, sb * n:(sb + 1) * n] = q.T.astype(BF16)


def _qk_conv(proj, conv_w, seq):
    t = proj.shape[0]
    width = 2 * ML_QK_COLS
    col = C_MQK // width
    per = CONV_TM // BF16_ROWS
    last = t // BF16_ROWS - 1
    return pl.pallas_call(
        functools.partial(_qk_conv_kernel, seq_blocks=seq // CONV_TM),
        grid=(t // CONV_TM,),
        in_specs=[
            pl.BlockSpec((CONV_TM, width), lambda i: (i, col)),
            pl.BlockSpec((BF16_ROWS, width), lambda i: (jnp.maximum(i * per - 1, 0), col)),
            pl.BlockSpec((BF16_ROWS, width), lambda i: (jnp.minimum((i + 1) * per, last), col)),
            pl.BlockSpec((3, width), lambda i: (0, 0)),
        ],
        out_specs=[pl.BlockSpec((CONV_TM, ML_QK_COLS), lambda i: (i, 0)),
                   pl.BlockSpec((ML_QK_COLS, CONV_TM), lambda i: (0, i))],
        out_shape=[jax.ShapeDtypeStruct((t, ML_QK_COLS), BF16),
                   jax.ShapeDtypeStruct((ML_QK_COLS, t), BF16)],
        compiler_params=pltpu.CompilerParams(
            dimension_semantics=("parallel",), vmem_limit_bytes=VMEM_LIMIT),
        name="qk_conv",
    )(proj, proj, proj, conv_w)


def _attn_kernel(sink_ref, q_ref, kc_ref, kp_ref, kn_ref, vc_ref, vp_ref, vn_ref,
                 o_ref, kbuf, vbuf, *, seq):
    w = ATT_WINDOW
    kbuf[0:w] = kp_ref[...]
    kbuf[w:w + ATT_TQ] = kc_ref[...]
    kbuf[w + ATT_TQ:] = kn_ref[...]
    vbuf[0:w] = vp_ref[...]
    vbuf[w:w + ATT_TQ] = vc_ref[...]
    vbuf[w + ATT_TQ:] = vn_ref[...]
    t0 = pl.program_id(1) * ATT_TQ
    rows = ATT_GROUP * w
    qi = lax.broadcasted_iota(jnp.int32, (rows, 1), 0) & (w - 1)
    blk = lax.broadcasted_iota(jnp.int32, (rows, 1), 0) // w
    kj = lax.broadcasted_iota(jnp.int32, (rows, 3 * w), 1)
    for n in range(ATT_TQ // w):
        base = t0 + (n - 1) * w
        lo = jnp.maximum(qi, -base)
        hi = jnp.minimum(qi + 2 * w, seq - 1 - base)
        valid = (kj >= lo) & (kj <= hi)
        for h in range(ATT_KV_HEADS):
            qs = jnp.concatenate(
                [q_ref[n * w:(n + 1) * w, (h * ATT_GROUP + g) * HEAD_DIM:(h * ATT_GROUP + g + 1) * HEAD_DIM]
                 for g in range(ATT_GROUP)], axis=0)
            kw = kbuf[n * w:(n + 3) * w, h * HEAD_DIM:(h + 1) * HEAD_DIM]
            vw = vbuf[n * w:(n + 3) * w, h * HEAD_DIM:(h + 1) * HEAD_DIM]
            s = lax.dot_general(qs, kw, _NT, preferred_element_type=F32)
            s = jnp.where(valid, s, -jnp.inf)
            sink = jnp.full((rows, 1), sink_ref[h * ATT_GROUP], F32)
            for g in range(1, ATT_GROUP):
                sink = jnp.where(blk == g, sink_ref[h * ATT_GROUP + g], sink)
            m = jnp.maximum(jnp.max(s, axis=-1, keepdims=True), sink)
            p = jnp.exp(s - m)
            denom = jnp.sum(p, axis=-1, keepdims=True) + jnp.exp(sink - m)
            o = jnp.dot(p.astype(BF16), vw, preferred_element_type=F32) / denom
            for g in range(ATT_GROUP):
                col = (h * ATT_GROUP + g) * HEAD_DIM
                o_ref[n * w:(n + 1) * w, col:col + HEAD_DIM] = o[g * w:(g + 1) * w].astype(BF16)


def _attention(proj, sink, batch, seq):
    t = proj.shape[0]
    w = ATT_WINDOW
    nq = seq // ATT_TQ
    per = ATT_TQ // w
    last_blk = t // w - 1
    cur = lambda b, i: b * nq + i
    prev = lambda b, i: jnp.maximum((b * nq + i) * per - 1, 0)
    nxt = lambda b, i: jnp.minimum((b * nq + i + 1) * per, last_blk)
    kcol = C_AK // ATT_KV_COLS
    vcol = C_AV // ATT_KV_COLS
    return pl.pallas_call(
        functools.partial(_attn_kernel, seq=seq),
        grid=(batch, nq),
        in_specs=[
            pl.BlockSpec(memory_space=pltpu.SMEM),
            pl.BlockSpec((ATT_TQ, ATT_Q_COLS), lambda b, i: (cur(b, i), C_AQ // ATT_Q_COLS)),
            pl.BlockSpec((ATT_TQ, ATT_KV_COLS), lambda b, i: (cur(b, i), kcol)),
            pl.BlockSpec((w, ATT_KV_COLS), lambda b, i: (prev(b, i), kcol)),
            pl.BlockSpec((w, ATT_KV_COLS), lambda b, i: (nxt(b, i), kcol)),
            pl.BlockSpec((ATT_TQ, ATT_KV_COLS), lambda b, i: (cur(b, i), vcol)),
            pl.BlockSpec((w, ATT_KV_COLS), lambda b, i: (prev(b, i), vcol)),
            pl.BlockSpec((w, ATT_KV_COLS), lambda b, i: (nxt(b, i), vcol)),
        ],
        out_specs=pl.BlockSpec((ATT_TQ, ATT_Q_COLS), lambda b, i: (cur(b, i), 0)),
        out_shape=jax.ShapeDtypeStruct((t, ATT_Q_COLS), BF16),
        scratch_shapes=[pltpu.VMEM((ATT_TQ + 2 * w, ATT_KV_COLS), BF16),
                        pltpu.VMEM((ATT_TQ + 2 * w, ATT_KV_COLS), BF16)],
        compiler_params=pltpu.CompilerParams(
            dimension_semantics=("parallel", "parallel"), vmem_limit_bytes=VMEM_LIMIT),
        name="attn",
    )(sink, proj, proj, proj, proj, proj, proj, proj)


def _rows_to_cols(x):
    length = x.shape[1]
    padded = jnp.concatenate([x, jnp.zeros((length - x.shape[0], length), x.dtype)], axis=0)
    return padded.T


def _mlstm_kernel(kf_ref, qtf_ref, vtf_ref, gtf_ref, kb_ref, qtb_ref, vtb_ref, gtb_ref,
                  hf_ref, hb_ref, c_sc, m_sc):
    L = ML_CHUNK
    H = ML_HEADS
    c = pl.program_id(1)

    @pl.when(c == 0)
    def _():
        c_sc[...] = jnp.zeros_like(c_sc)
        m_sc[...] = jnp.full_like(m_sc, M_INIT)

    row8 = lax.broadcasted_iota(jnp.int32, (2 * H, L), 0)
    lane = lax.broadcasted_iota(jnp.int32, (2 * H, L), 1)
    is_fwd = row8 < H
    gi = jnp.where(is_fwd, gtf_ref[0:2 * H, :], gtb_ref[0:2 * H, :])
    gf = jnp.where(is_fwd, gtf_ref[2 * H:4 * H, :], gtb_ref[2 * H:4 * H, :])
    ls = _log_sigmoid(gf)
    hi = ls.astype(BF16).astype(F32)
    rem = ls - hi
    mid = rem.astype(BF16).astype(F32)
    lo = rem - mid
    parts = jnp.concatenate([hi, mid, lo, jnp.zeros_like(hi)], axis=0).astype(BF16)
    si = lax.broadcasted_iota(jnp.int32, (L, L), 0)
    ti = lax.broadcasted_iota(jnp.int32, (L, L), 1)
    s_le_t = si <= ti
    s_ge_t = si >= ti
    pre = jnp.dot(parts, s_le_t.astype(BF16), preferred_element_type=F32)
    prefix = pre[0:2 * H] + pre[2 * H:4 * H] + pre[4 * H:6 * H]
    gtot = jnp.sum(ls, axis=1, keepdims=True)
    b = jnp.where(is_fwd, prefix, gtot - prefix + ls)
    r = gi - b
    cm = r
    sh = 1
    while sh < L:
        from_left = jnp.where(lane >= sh, pltpu.roll(cm, sh, 1), -jnp.inf)
        from_right = jnp.where(lane < L - sh, pltpu.roll(cm, L - sh, 1), -jnp.inf)
        cm = jnp.maximum(cm, jnp.where(is_fwd, from_left, from_right))
        sh *= 2
    m_prev = m_sc[...]
    top = jnp.maximum(m_prev, cm)
    m_t = b + top
    bm = -top
    w_inter = jnp.exp(m_prev + bm)
    clamp = jnp.exp(-m_t)
    a = gtot - b + gi
    m_loc = jnp.max(a, axis=1, keepdims=True)
    w_end = jnp.exp(a - m_loc)
    m_new = jnp.maximum(gtot + m_prev, m_loc)
    s_prev = jnp.exp(gtot + m_prev - m_new)
    s_loc = jnp.exp(m_loc - m_new)
    m_sc[...] = m_new
    r_cols = _rows_to_cols(r)
    w_end_cols = _rows_to_cols(w_end)

    n_row = lax.broadcasted_iota(jnp.int32, (BF16_ROWS, 1), 0) == 0
    dirs = ((kf_ref, qtf_ref, vtf_ref, hf_ref, s_le_t), (kb_ref, qtb_ref, vtb_ref, hb_ref, s_ge_t))
    for d, (k_ref, qt_ref, vt_ref, out_ref, mask) in enumerate(dirs):
        for j in range(H):
            p = d * H + j
            k_j = k_ref[:, j * ML_QK_DIM:(j + 1) * ML_QK_DIM]
            qt_j = qt_ref[j * ML_QK_DIM:(j + 1) * ML_QK_DIM, :]
            vt_j = vt_ref[j * ML_V_DIM:(j + 1) * ML_V_DIM, :]
            arg = jnp.broadcast_to(r_cols[:, p:p + 1], (L, L)) + bm[p:p + 1, :]
            e = jnp.exp(jnp.where(mask, arg, -jnp.inf))
            sc_t = jnp.dot(k_j, qt_j, preferred_element_type=F32) * e
            den = jnp.sum(sc_t, axis=0, keepdims=True)
            kw = k_j.astype(F32) * jnp.broadcast_to(w_end_cols[:, p:p + 1], (L, ML_QK_DIM))
            n_loc = jnp.sum(kw, axis=0, keepdims=True)
            rhs = jnp.concatenate([sc_t.astype(BF16), kw.astype(BF16)], axis=1)
            both = jnp.dot(vt_j, rhs, preferred_element_type=F32)
            c_prev = c_sc[p]
            qtw = (qt_j.astype(F32) * w_inter[p:p + 1, :]).astype(BF16)
            carried = jnp.dot(c_prev.astype(BF16), qtw, preferred_element_type=F32)
            num = both[:, :L] + carried[:ML_V_DIM]
            den = den + carried[ML_V_DIM:ML_V_DIM + 1]
            out_ref[j * ML_V_DIM:(j + 1) * ML_V_DIM, :] = num / jnp.maximum(jnp.abs(den), clamp[p:p + 1, :])
            sp = s_prev[p:p + 1, :]
            sl = s_loc[p:p + 1, :]
            c_sc[p, 0:ML_V_DIM, :] = sp * c_prev[:ML_V_DIM] + sl * both[:, L:]
            c_sc[p, ML_V_DIM:, :] = sp * c_prev[ML_V_DIM:] + sl * jnp.where(n_row, n_loc, 0.0)


def _mlstm(k_conv, qt_conv, projt, gates_t, batch, seq):
    t = k_conv.shape[0]
    L = ML_CHUNK
    nc = seq // L
    fwd = lambda b, c: b * nc + c
    bwd = lambda b, c: b * nc + nc - 1 - c

    def dir_specs(ch):
        return [
            pl.BlockSpec((L, ML_QK_COLS), lambda b, c: (ch(b, c), 0)),
            pl.BlockSpec((ML_QK_COLS, L), lambda b, c: (0, ch(b, c))),
            pl.BlockSpec((ML_V_COLS, L), lambda b, c: (R_MV // ML_V_COLS, ch(b, c))),
            pl.BlockSpec((GATE_COLS, L), lambda b, c: (0, ch(b, c))),
        ]

    n_states = 2 * ML_HEADS
    return pl.pallas_call(
        _mlstm_kernel,
        grid=(batch, nc),
        in_specs=dir_specs(fwd) + dir_specs(bwd),
        out_specs=[pl.BlockSpec((ML_V_COLS, L), lambda b, c: (0, fwd(b, c))),
                   pl.BlockSpec((ML_V_COLS, L), lambda b, c: (0, bwd(b, c)))],
        out_shape=[jax.ShapeDtypeStruct((ML_V_COLS, t), F32),
                   jax.ShapeDtypeStruct((ML_V_COLS, t), F32)],
        scratch_shapes=[pltpu.VMEM((n_states, ML_STATE_ROWS, ML_QK_DIM), F32),
                        pltpu.VMEM((n_states, L), F32)],
        compiler_params=pltpu.CompilerParams(
            dimension_semantics=("parallel", "arbitrary"), vmem_limit_bytes=VMEM_LIMIT),
        name="mlstm",
    )(k_conv, qt_conv, projt, gates_t, k_conv, qt_conv, projt, gates_t)


def _mix_kernel(att_ref, hft_ref, hbt_ref, mot_ref, x_ref, w_ref, gml_ref, gpost_ref, o_ref, memt_ref):
    for j in range(ML_HEADS):
        sl = slice(j * ML_V_DIM, (j + 1) * ML_V_DIM)
        h = _rms(hft_ref[sl, :] + hbt_ref[sl, :], axis=0)
        gain = jnp.concatenate([gml_ref[sl, :]] * (MIX_TM // LANES), axis=1)
        out = h * gain * _sigmoid(mot_ref[sl, :].astype(F32))
        memt_ref[sl, :] = out.astype(BF16)
    mix = jnp.dot(att_ref[...], w_ref[0:ATT_Q_COLS, :], preferred_element_type=F32)
    mix = mix + lax.dot_general(memt_ref[...], w_ref[ATT_Q_COLS:, :], _TN, preferred_element_type=F32)
    o_ref[...] = x_ref[...] + _rms(mix) * gpost_ref[...]


def _mix(att, hft, hbt, projt, x2d, w_out, g_ml_b, g_post):
    t = x2d.shape[0]
    row = lambda i: (i, 0)
    col = lambda i: (0, i)
    const = lambda i: (0, 0)
    return pl.pallas_call(
        _mix_kernel,
        grid=(t // MIX_TM,),
        in_specs=[
            pl.BlockSpec((MIX_TM, ATT_Q_COLS), row),
            pl.BlockSpec((ML_V_COLS, MIX_TM), col),
            pl.BlockSpec((ML_V_COLS, MIX_TM), col),
            pl.BlockSpec((ML_V_COLS, MIX_TM), lambda i: (R_MO // ML_V_COLS, i)),
            pl.BlockSpec((MIX_TM, D_MODEL), row),
            pl.BlockSpec((D_MODEL, D_MODEL), const, pipeline_mode=pl.Buffered(1)),
            pl.BlockSpec((ML_V_COLS, LANES), const),
            pl.BlockSpec((1, D_MODEL), const),
        ],
        out_specs=pl.BlockSpec((MIX_TM, D_MODEL), row),
        out_shape=jax.ShapeDtypeStruct((t, D_MODEL), F32),
        scratch_shapes=[pltpu.VMEM((ML_V_COLS, MIX_TM), BF16)],
        compiler_params=pltpu.CompilerParams(
            dimension_semantics=("parallel",), vmem_limit_bytes=VMEM_LIMIT),
        name="mix",
    )(att, hft, hbt, projt, x2d, w_out, g_ml_b, g_post)


def _mlp_kernel(x_ref, gpre_ref, wup_ref, wdown_ref, gpost_ref, o_ref, h_ref):
    j = pl.program_id(1)

    @pl.when(j == 0)
    def _():
        h_ref[...] = (_rms(x_ref[...]) * gpre_ref[...]).astype(BF16)
        o_ref[...] = jnp.zeros_like(o_ref)

    u = jnp.dot(h_ref[...], wup_ref[...], preferred_element_type=F32)
    u = jnp.square(jnp.maximum(u, 0.0)).astype(BF16)
    o_ref[...] += jnp.dot(u, wdown_ref[...], preferred_element_type=F32)

    @pl.when(j == pl.num_programs(1) - 1)
    def _():
        o_ref[...] = x_ref[...] + _rms(o_ref[...]) * gpost_ref[...]


def _mlp(x2d, g_pre, w_up, w_down, g_post):
    t = x2d.shape[0]
    return pl.pallas_call(
        _mlp_kernel,
        grid=(t // MLP_TM, D_FF // MLP_TF),
        in_specs=[
            pl.BlockSpec((MLP_TM, D_MODEL), lambda i, j: (i, 0)),
            pl.BlockSpec((1, D_MODEL), lambda i, j: (0, 0)),
            pl.BlockSpec((D_MODEL, MLP_TF), lambda i, j: (0, j)),
            pl.BlockSpec((MLP_TF, D_MODEL), lambda i, j: (j, 0)),
            pl.BlockSpec((1, D_MODEL), lambda i, j: (0, 0)),
        ],
        out_specs=pl.BlockSpec((MLP_TM, D_MODEL), lambda i, j: (i, 0)),
        out_shape=jax.ShapeDtypeStruct((t, D_MODEL), F32),
        scratch_shapes=[pltpu.VMEM((MLP_TM, D_MODEL), BF16)],
        compiler_params=pltpu.CompilerParams(
            dimension_semantics=("parallel", "arbitrary"), vmem_limit_bytes=VMEM_LIMIT),
        name="mlp",
    )(x2d, g_pre, w_up, w_down, g_post)


def _rope_tables(seq):
    half = HEAD_DIM // 2
    inv_freq = ROPE_THETA ** (-jnp.arange(half, dtype=F32) / half)
    ang = jnp.arange(seq, dtype=F32)[:, None] * inv_freq[None, :]
    cos = jnp.cos(ang)
    sin = jnp.sin(ang)
    return jnp.concatenate([cos, cos], axis=1), jnp.concatenate([-sin, sin], axis=1)


def _gate_order():
    h = ML_HEADS
    return (list(range(0, h)) + list(range(2 * h, 3 * h))
            + list(range(h, 2 * h)) + list(range(3 * h, 4 * h)))


def _split_w_in(w):
    sizes = [ATT_Q_COLS, ATT_KV_COLS, ATT_KV_COLS, ML_QK_COLS, ML_QK_COLS, ML_V_COLS, ML_V_COLS]
    offs = [0]
    for s in sizes:
        offs.append(offs[-1] + s)
    aq, ak, av, mq, mk, mv, mo = [w[:, offs[i]:offs[i + 1]] for i in range(7)]
    w_rm = jnp.concatenate([aq, mq, mk, ak, av], axis=1).astype(BF16)
    w_gate = w[:, offs[-1]:][:, jnp.array(_gate_order())]
    w_t = jnp.concatenate([mv, mo, w_gate], axis=1).T.astype(BF16)
    return w_rm, w_t


def kernel(x, w_in, conv_w, gate_bias, ml_norm_g, attn_sink, w_out, g_pre_mix, g_post_mix,
           g_pre_mlp, g_post_mlp, w_up, w_down):
    batch, seq, d = x.shape
    depth = w_in.shape[0]
    cos_t, sin_t = _rope_tables(seq)
    gate_order = jnp.array(_gate_order())
    x2d = x.reshape(batch * seq, d)
    for l in range(depth):
        w_rm, w_t = _split_w_in(w_in[l])
        bias_col = gate_bias[l][gate_order][:, None]
        proj, projt, gates_t = _in_proj(x2d, g_pre_mix[l][None, :], w_rm, w_t, bias_col, cos_t, sin_t, seq)
        k_conv, qt_conv = _qk_conv(proj, conv_w[l], seq)
        att = _attention(proj, attn_sink[l], batch, seq)
        hft, hbt = _mlstm(k_conv, qt_conv, projt, gates_t, batch, seq)
        g_ml_b = jnp.broadcast_to(ml_norm_g[l][:, None], (ML_V_COLS, LANES))
        x2d = _mix(att, hft, hbt, projt, x2d, w_out[l].astype(BF16), g_ml_b, g_post_mix[l][None, :])
        x2d = _mlp(x2d, g_pre_mlp[l][None, :], w_up[l].astype(BF16), w_down[l].astype(BF16),
                   g_post_mlp[l][None, :])
    return x2d.reshape(batch, seq, d)
```

```python
import functools

import jax
import jax.numpy as jnp
from jax import lax
from jax.experimental import pallas as pl
from jax.experimental.pallas import tpu as pltpu

F32 = jnp.float32
BF16 = jnp.bfloat16

D_MODEL = 2048
ATT_HEADS = 8
ATT_KV_HEADS = 2
ATT_GROUP = ATT_HEADS // ATT_KV_HEADS
HEAD_DIM = 128
ATT_WINDOW = 128
ROPE_THETA = 10000.0
ML_HEADS = 4
ML_V_DIM = 256
ML_QK_DIM = 128
ML_CHUNK = 128
M_INIT = -1e30
D_FF = 4 * D_MODEL
NORM_EPS = 1e-6
GATE_COLS = 4 * ML_HEADS

ATT_Q_COLS = ATT_HEADS * HEAD_DIM
ATT_KV_COLS = ATT_KV_HEADS * HEAD_DIM
ML_QK_COLS = ML_HEADS * ML_QK_DIM
ML_V_COLS = ML_HEADS * ML_V_DIM

S_AK = ATT_Q_COLS
S_AV = S_AK + ATT_KV_COLS
S_MQ = S_AV + ATT_KV_COLS
S_MV = S_MQ + 2 * ML_QK_COLS
S_GATE = S_MV + 2 * ML_V_COLS
IN_COLS = S_GATE + GATE_COLS
C_AQ = 0
C_MQK = C_AQ + ATT_Q_COLS
C_AK = C_MQK + 2 * ML_QK_COLS
C_AV = C_AK + ATT_KV_COLS
PROJ_COLS = C_AV + ATT_KV_COLS
R_MV = 0
R_MO = R_MV + ML_V_COLS
PROJT_ROWS = R_MO + ML_V_COLS
LANES = 128
SUBLANES = 8
BF16_ROWS = 16

VMEM_LIMIT = 56 * 1024 * 1024

IN_TM = 512
IN_TN = 512
CONV_TM = 1024
CONV_SUB = 256
ATT_TQ = 512
MIX_TM = 512
MLP_TM = 512
MLP_TF = 1024
ML_STATE_ROWS = ML_V_DIM + BF16_ROWS
ML_PAIRS = 2 * ML_HEADS
ML_ROW_KINDS = 7

_NT = (((1,), (1,)), ((), ()))
_TN = (((0,), (0,)), ((), ()))


def _sigmoid(x):
    return 1.0 / (1.0 + jnp.exp(-x))


def _log_sigmoid(x):
    return jnp.minimum(x, 0.0) - jnp.log(1.0 + jnp.exp(-jnp.abs(x)))


def _rms(x, axis=-1):
    return x * lax.rsqrt(jnp.mean(x * x, axis=axis, keepdims=True) + NORM_EPS)


_ROW_MAJOR_CHUNKS = ((0, C_AQ), (IN_TN, C_AQ + IN_TN), (S_AK, C_AK), (S_MQ, C_MQK),
                     (S_MQ + IN_TN, C_MQK + IN_TN))


def _in_proj_kernel(x_ref, g_ref, w_ref, wg_ref, gbias_ref, cos_ref, sin_ref, wup_ref, wout_ref,
                    proj_ref, projt_ref, gatet_ref, wup_bf_ref, wout_bf_ref, h_ref):
    wup_bf_ref[...] = wup_ref[...].astype(BF16)
    wout_bf_ref[...] = wout_ref[...].astype(BF16)
    h_ref[...] = (_rms(x_ref[...]) * g_ref[...]).astype(BF16)
    cos = cos_ref[...]
    sin = sin_ref[...]

    def rope(a):
        return a * cos + pltpu.roll(a, HEAD_DIM // 2, 1) * sin

    q_scale = HEAD_DIM ** -0.5
    for src, dst in _ROW_MAJOR_CHUNKS:
        acc = jnp.dot(h_ref[...], w_ref[:, src:src + IN_TN], preferred_element_type=F32)
        for k in range(IN_TN // HEAD_DIM):
            col = src + k * HEAD_DIM
            a = acc[:, k * HEAD_DIM:(k + 1) * HEAD_DIM]
            if col < S_AK:
                a = rope(a) * q_scale
            elif col < S_AV:
                a = rope(a)
            proj_ref[:, dst + k * HEAD_DIM:dst + (k + 1) * HEAD_DIM] = a.astype(BF16)
    for c in range(PROJT_ROWS // IN_TN):
        acc = jnp.dot(h_ref[...], w_ref[:, S_MV + c * IN_TN:S_MV + (c + 1) * IN_TN],
                      preferred_element_type=F32)
        projt_ref[c * IN_TN:(c + 1) * IN_TN, :] = acc.T.astype(BF16)
    gate = jnp.dot(h_ref[...], wg_ref[...], preferred_element_type=F32)
    gatet_ref[...] = gate.T[:GATE_COLS] + gbias_ref[...]


def _in_proj(x2d, g, w_bf, wg, gate_bias_col, cos_t, sin_t, w_up, w_out, seq):
    t = x2d.shape[0]
    steps = t // IN_TM
    pos_blocks = seq // IN_TM
    up_rows = D_MODEL // steps
    const = lambda i: (0, 0)
    row = lambda i: (i, 0)
    return pl.pallas_call(
        _in_proj_kernel,
        grid=(steps,),
        in_specs=[
            pl.BlockSpec((IN_TM, D_MODEL), row),
            pl.BlockSpec((1, D_MODEL), const),
            pl.BlockSpec((D_MODEL, IN_COLS), const, pipeline_mode=pl.Buffered(1)),
            pl.BlockSpec((D_MODEL, LANES), const),
            pl.BlockSpec((GATE_COLS, 1), const),
            pl.BlockSpec((IN_TM, HEAD_DIM), lambda i: (i % pos_blocks, 0)),
            pl.BlockSpec((IN_TM, HEAD_DIM), lambda i: (i % pos_blocks, 0)),
            pl.BlockSpec((up_rows, D_FF), row),
            pl.BlockSpec((up_rows, D_MODEL), row),
        ],
        out_specs=[
            pl.BlockSpec((IN_TM, PROJ_COLS), row),
            pl.BlockSpec((PROJT_ROWS, IN_TM), lambda i: (0, i)),
            pl.BlockSpec((GATE_COLS, IN_TM), lambda i: (0, i)),
            pl.BlockSpec((up_rows, D_FF), row),
            pl.BlockSpec((up_rows, D_MODEL), row),
        ],
        out_shape=[
            jax.ShapeDtypeStruct((t, PROJ_COLS), BF16),
            jax.ShapeDtypeStruct((PROJT_ROWS, t), BF16),
            jax.ShapeDtypeStruct((GATE_COLS, t), F32),
            jax.ShapeDtypeStruct((D_MODEL, D_FF), BF16),
            jax.ShapeDtypeStruct((D_MODEL, D_MODEL), BF16),
        ],
        scratch_shapes=[pltpu.VMEM((IN_TM, D_MODEL), BF16)],
        compiler_params=pltpu.CompilerParams(
            dimension_semantics=("parallel",), vmem_limit_bytes=VMEM_LIMIT),
        name="in_proj",
    )(x2d, g, w_bf, wg, gate_bias_col, cos_t, sin_t, w_up, w_out)


def _qk_conv_kernel(x_ref, xp_ref, xn_ref, w_ref, k_ref, qt_ref, *, seq_blocks):
    n = CONV_SUB
    pos = pl.program_id(0) % seq_blocks
    has_prev = (pos > 0).astype(F32)
    has_next = (pos < seq_blocks - 1).astype(F32)
    ri = lax.broadcasted_iota(jnp.int32, (n, n), 0)
    ci = lax.broadcasted_iota(jnp.int32, (n, n), 1)
    shift_prev = (ci == ri - 1).astype(BF16)
    shift_next = (ci == ri + 1).astype(BF16)
    rowid = lax.broadcasted_iota(jnp.int32, (n, 1), 0)
    w0 = w_ref[0:1, :]
    w1 = w_ref[1:2, :]
    w2 = w_ref[2:3, :]
    n_sub = CONV_TM // n
    for sb in range(n_sub):
        xs = x_ref[sb * n:(sb + 1) * n, :]
        x_prev = jnp.dot(shift_prev, xs, preferred_element_type=F32)
        x_next = jnp.dot(shift_next, xs, preferred_element_type=F32)
        if sb == 0:
            prev_row = xp_ref[BF16_ROWS - 1:BF16_ROWS, :].astype(F32) * has_prev
        else:
            prev_row = x_ref[sb * n - BF16_ROWS:sb * n, :].astype(F32)[BF16_ROWS - 1:BF16_ROWS]
        if sb == n_sub - 1:
            next_row = xn_ref[0:1, :].astype(F32) * has_next
        else:
            next_row = x_ref[(sb + 1) * n:(sb + 1) * n + BF16_ROWS, :].astype(F32)[0:1]
        x_prev = jnp.where(rowid == 0, prev_row, x_prev)
        x_next = jnp.where(rowid == n - 1, next_row, x_next)
        y = x_prev * w0 + xs.astype(F32) * w1 + x_next * w2
        y = y * _sigmoid(y)
        k_ref[sb * n:(sb + 1) * n, :] = y[:, ML_QK_COLS:].astype(BF16)
        q = y[:, :ML_QK_COLS] * (ML_QK_DIM ** -0.5)
        qt_ref[:, sb * n:(sb + 1) * n] = q.T.astype(BF16)


def _qk_conv(proj, conv_w, seq):
    t = proj.shape[0]
    width = 2 * ML_QK_COLS
    col = C_MQK // width
    per = CONV_TM // BF16_ROWS
    last = t // BF16_ROWS - 1
    return pl.pallas_call(
        functools.partial(_qk_conv_kernel, seq_blocks=seq // CONV_TM),
        grid=(t // CONV_TM,),
        in_specs=[
            pl.BlockSpec((CONV_TM, width), lambda i: (i, col)),
            pl.BlockSpec((BF16_ROWS, width), lambda i: (jnp.maximum(i * per - 1, 0), col)),
            pl.BlockSpec((BF16_ROWS, width), lambda i: (jnp.minimum((i + 1) * per, last), col)),
            pl.BlockSpec((3, width), lambda i: (0, 0)),
        ],
        out_specs=[pl.BlockSpec((CONV_TM, ML_QK_COLS), lambda i: (i, 0)),
                   pl.BlockSpec((ML_QK_COLS, CONV_TM), lambda i: (0, i))],
        out_shape=[jax.ShapeDtypeStruct((t, ML_QK_COLS), BF16),
                   jax.ShapeDtypeStruct((ML_QK_COLS, t), BF16)],
        compiler_params=pltpu.CompilerParams(
            dimension_semantics=("parallel",), vmem_limit_bytes=VMEM_LIMIT),
        name="qk_conv",
    )(proj, proj, proj, conv_w)


def _attn_kernel(sink_ref, q_ref, kc_ref, kp_ref, kn_ref, vc_ref, vp_ref, vn_ref,
                 o_ref, kbuf, vbuf, *, seq):
    w = ATT_WINDOW
    kbuf[0:w] = kp_ref[...]
    kbuf[w:w + ATT_TQ] = kc_ref[...]
    kbuf[w + ATT_TQ:] = kn_ref[...]
    vbuf[0:w] = vp_ref[...]
    vbuf[w:w + ATT_TQ] = vc_ref[...]
    vbuf[w + ATT_TQ:] = vn_ref[...]
    t0 = pl.program_id(1) * ATT_TQ
    rows = ATT_GROUP * w
    qi = lax.broadcasted_iota(jnp.int32, (rows, 1), 0) & (w - 1)
    blk = lax.broadcasted_iota(jnp.int32, (rows, 1), 0) // w
    kj = lax.broadcasted_iota(jnp.int32, (rows, 3 * w), 1)
    for n in range(ATT_TQ // w):
        base = t0 + (n - 1) * w
        lo = jnp.maximum(qi, -base)
        hi = jnp.minimum(qi + 2 * w, seq - 1 - base)
        valid = (kj >= lo) & (kj <= hi)
        for h in range(ATT_KV_HEADS):
            qs = jnp.concatenate(
                [q_ref[n * w:(n + 1) * w, (h * ATT_GROUP + g) * HEAD_DIM:(h * ATT_GROUP + g + 1) * HEAD_DIM]
                 for g in range(ATT_GROUP)], axis=0)
            kw = kbuf[n * w:(n + 3) * w, h * HEAD_DIM:(h + 1) * HEAD_DIM]
            vw = vbuf[n * w:(n + 3) * w, h * HEAD_DIM:(h + 1) * HEAD_DIM]
            s = lax.dot_general(qs, kw, _NT, preferred_element_type=F32)
            s = jnp.where(valid, s, -jnp.inf)
            sink = jnp.full((rows, 1), sink_ref[h * ATT_GROUP], F32)
            for g in range(1, ATT_GROUP):
                sink = jnp.where(blk == g, sink_ref[h * ATT_GROUP + g], sink)
            m = jnp.maximum(jnp.max(s, axis=-1, keepdims=True), sink)
            p = jnp.exp(s - m)
            denom = jnp.sum(p, axis=-1, keepdims=True) + jnp.exp(sink - m)
            o = jnp.dot(p.astype(BF16), vw, preferred_element_type=F32) / denom
            for g in range(ATT_GROUP):
                col = (h * ATT_GROUP + g) * HEAD_DIM
                o_ref[n * w:(n + 1) * w, col:col + HEAD_DIM] = o[g * w:(g + 1) * w].astype(BF16)


def _attention(proj, sink, batch, seq):
    t = proj.shape[0]
    w = ATT_WINDOW
    nq = seq // ATT_TQ
    per = ATT_TQ // w
    last_blk = t // w - 1
    cur = lambda b, i: b * nq + i
    prev = lambda b, i: jnp.maximum((b * nq + i) * per - 1, 0)
    nxt = lambda b, i: jnp.minimum((b * nq + i + 1) * per, last_blk)
    kcol = C_AK // ATT_KV_COLS
    vcol = C_AV // ATT_KV_COLS
    return pl.pallas_call(
        functools.partial(_attn_kernel, seq=seq),
        grid=(batch, nq),
        in_specs=[
            pl.BlockSpec(memory_space=pltpu.SMEM),
            pl.BlockSpec((ATT_TQ, ATT_Q_COLS), lambda b, i: (cur(b, i), C_AQ // ATT_Q_COLS)),
            pl.BlockSpec((ATT_TQ, ATT_KV_COLS), lambda b, i: (cur(b, i), kcol)),
            pl.BlockSpec((w, ATT_KV_COLS), lambda b, i: (prev(b, i), kcol)),
            pl.BlockSpec((w, ATT_KV_COLS), lambda b, i: (nxt(b, i), kcol)),
            pl.BlockSpec((ATT_TQ, ATT_KV_COLS), lambda b, i: (cur(b, i), vcol)),
            pl.BlockSpec((w, ATT_KV_COLS), lambda b, i: (prev(b, i), vcol)),
            pl.BlockSpec((w, ATT_KV_COLS), lambda b, i: (nxt(b, i), vcol)),
        ],
        out_specs=pl.BlockSpec((ATT_TQ, ATT_Q_COLS), lambda b, i: (cur(b, i), 0)),
        out_shape=jax.ShapeDtypeStruct((t, ATT_Q_COLS), BF16),
        scratch_shapes=[pltpu.VMEM((ATT_TQ + 2 * w, ATT_KV_COLS), BF16),
                        pltpu.VMEM((ATT_TQ + 2 * w, ATT_KV_COLS), BF16)],
        compiler_params=pltpu.CompilerParams(
            dimension_semantics=("parallel", "parallel"), vmem_limit_bytes=VMEM_LIMIT),
        name="attn",
    )(sink, proj, proj, proj, proj, proj, proj, proj)


def _rows_to_cols(x):
    length = x.shape[1]
    padded = jnp.concatenate([x, jnp.zeros((length - x.shape[0], length), x.dtype)], axis=0)
    return padded.T


def _ml_gate_rows(gates_f, gates_b, m_prev):
    L = ML_CHUNK
    H = ML_HEADS
    row8 = lax.broadcasted_iota(jnp.int32, (2 * H, L), 0)
    lane = lax.broadcasted_iota(jnp.int32, (2 * H, L), 1)
    is_fwd = row8 < H
    fwd_if = gates_f[0:2 * H]
    bwd_if = gates_b[2 * H:4 * H]
    gi = jnp.where(is_fwd, fwd_if, pltpu.roll(bwd_if, H, 0))
    gf = jnp.where(is_fwd, pltpu.roll(fwd_if, H, 0), bwd_if)
    ls = _log_sigmoid(gf)
    hi = ls.astype(BF16).astype(F32)
    rem = ls - hi
    mid = rem.astype(BF16).astype(F32)
    lo = rem - mid
    parts = jnp.concatenate([hi, mid, lo, jnp.zeros_like(hi)], axis=0).astype(BF16)
    si = lax.broadcasted_iota(jnp.int32, (L, L), 0)
    ti = lax.broadcasted_iota(jnp.int32, (L, L), 1)
    pre = jnp.dot(parts, (si <= ti).astype(BF16), preferred_element_type=F32)
    prefix = pre[0:2 * H] + pre[2 * H:4 * H] + pre[4 * H:6 * H]
    gtot = jnp.sum(ls, axis=1, keepdims=True)
    b = jnp.where(is_fwd, prefix, gtot - prefix + ls)
    r = gi - b
    cm = r
    sh = 1
    while sh < L:
        from_left = jnp.where(lane >= sh, pltpu.roll(cm, sh, 1), -jnp.inf)
        from_right = jnp.where(lane < L - sh, pltpu.roll(cm, L - sh, 1), -jnp.inf)
        cm = jnp.maximum(cm, jnp.where(is_fwd, from_left, from_right))
        sh *= 2
    top = jnp.maximum(m_prev, cm)
    a = gtot - b + gi
    m_loc = jnp.max(a, axis=1, keepdims=True)
    m_new = jnp.maximum(gtot + m_prev, m_loc)
    rows = jnp.concatenate([
        -top,
        jnp.exp(m_prev - top),
        jnp.exp(-(b + top)),
        r,
        jnp.exp(a - m_loc),
        jnp.exp(gtot + m_prev - m_new),
        jnp.exp(m_loc - m_new) + jnp.zeros_like(r),
    ], axis=0)
    return rows, m_new


def _mlstm_kernel(kf_ref, qtf_ref, vtf_ref, gtf_ref, gtfn_ref, kb_ref, qtb_ref, vtb_ref, gtb_ref,
                  gtbn_ref, hf_ref, hb_ref, c_sc, m_sc, rows_sc):
    L = ML_CHUNK
    H = ML_HEADS
    P = ML_PAIRS
    c = pl.program_id(1)
    slot = c % 2

    @pl.when(c == 0)
    def _():
        c_sc[...] = jnp.zeros_like(c_sc)
        rows0, m1 = _ml_gate_rows(gtf_ref[...], gtb_ref[...], jnp.full((P, L), M_INIT, F32))
        rows_sc[0] = rows0
        m_sc[...] = m1

    rows = rows_sc[slot]
    rows_next, m_next = _ml_gate_rows(gtfn_ref[...], gtbn_ref[...], m_sc[...])
    rows_sc[1 - slot] = rows_next
    m_sc[...] = m_next

    bm, w_inter, clamp, r, w_end, s_prev, s_loc = [rows[i * P:(i + 1) * P] for i in range(ML_ROW_KINDS)]
    r_cols = _rows_to_cols(r)
    w_end_cols = _rows_to_cols(w_end)
    si = lax.broadcasted_iota(jnp.int32, (L, L), 0)
    ti = lax.broadcasted_iota(jnp.int32, (L, L), 1)
    masks = (si <= ti, si >= ti)
    n_row = lax.broadcasted_iota(jnp.int32, (BF16_ROWS, 1), 0) == 0
    refs = ((kf_ref, qtf_ref, vtf_ref, hf_ref), (kb_ref, qtb_ref, vtb_ref, hb_ref))
    pairs = [(d, j) for d in range(2) for j in range(H)]

    def k_of(d, j):
        return refs[d][0][:, j * ML_QK_DIM:(j + 1) * ML_QK_DIM]

    def qt_of(d, j):
        return refs[d][1][j * ML_QK_DIM:(j + 1) * ML_QK_DIM, :]

    scores = []
    carried = []
    for d, j in pairs:
        p = d * H + j
        scores.append(jnp.dot(k_of(d, j), qt_of(d, j), preferred_element_type=F32))
        qtw = (qt_of(d, j).astype(F32) * w_inter[p:p + 1, :]).astype(BF16)
        carried.append(jnp.dot(c_sc[p].astype(BF16), qtw, preferred_element_type=F32))
    dens = []
    n_locs = []
    rhss = []
    for d, j in pairs:
        p = d * H + j
        arg = jnp.broadcast_to(r_cols[:, p:p + 1], (L, L)) + bm[p:p + 1, :]
        sc_t = scores[p] * jnp.exp(jnp.where(masks[d], arg, -jnp.inf))
        dens.append(jnp.sum(sc_t, axis=0, keepdims=True))
        kw = k_of(d, j).astype(F32) * jnp.broadcast_to(w_end_cols[:, p:p + 1], (L, ML_QK_DIM))
        n_locs.append(jnp.sum(kw, axis=0, keepdims=True))
        rhss.append(jnp.concatenate([sc_t.astype(BF16), kw.astype(BF16)], axis=1))
    boths = []
    for d, j in pairs:
        p = d * H + j
        vt_j = refs[d][2][j * ML_V_DIM:(j + 1) * ML_V_DIM, :]
        boths.append(jnp.dot(vt_j, rhss[p], preferred_element_type=F32))
    for d, j in pairs:
        p = d * H + j
        num = boths[p][:, :L] + carried[p][:ML_V_DIM]
        den = dens[p] + carried[p][ML_V_DIM:ML_V_DIM + 1]
        refs[d][3][j * ML_V_DIM:(j + 1) * ML_V_DIM, :] = num / jnp.maximum(jnp.abs(den), clamp[p:p + 1, :])
        c_prev = c_sc[p]
        sp = s_prev[p:p + 1, :]
        sl = s_loc[p:p + 1, :]
        c_sc[p, 0:ML_V_DIM, :] = sp * c_prev[:ML_V_DIM] + sl * boths[p][:, L:]
        c_sc[p, ML_V_DIM:, :] = sp * c_prev[ML_V_DIM:] + sl * jnp.where(n_row, n_locs[p], 0.0)


def _mlstm(k_conv, qt_conv, projt, gates_t, batch, seq):
    t = k_conv.shape[0]
    L = ML_CHUNK
    nc = seq // L
    fwd = lambda b, c: b * nc + c
    bwd = lambda b, c: b * nc + nc - 1 - c
    nxt = lambda c: jnp.minimum(c + 1, nc - 1)

    def dir_specs(ch):
        return [
            pl.BlockSpec((L, ML_QK_COLS), lambda b, c: (ch(b, c), 0)),
            pl.BlockSpec((ML_QK_COLS, L), lambda b, c: (0, ch(b, c))),
            pl.BlockSpec((ML_V_COLS, L), lambda b, c: (R_MV // ML_V_COLS, ch(b, c))),
            pl.BlockSpec((GATE_COLS, L), lambda b, c: (0, ch(b, c))),
            pl.BlockSpec((GATE_COLS, L), lambda b, c: (0, ch(b, nxt(c)))),
        ]

    return pl.pallas_call(
        _mlstm_kernel,
        grid=(batch, nc),
        in_specs=dir_specs(fwd) + dir_specs(bwd),
        out_specs=[pl.BlockSpec((ML_V_COLS, L), lambda b, c: (0, fwd(b, c))),
                   pl.BlockSpec((ML_V_COLS, L), lambda b, c: (0, bwd(b, c)))],
        out_shape=[jax.ShapeDtypeStruct((ML_V_COLS, t), F32),
                   jax.ShapeDtypeStruct((ML_V_COLS, t), F32)],
        scratch_shapes=[pltpu.VMEM((ML_PAIRS, ML_STATE_ROWS, ML_QK_DIM), F32),
                        pltpu.VMEM((ML_PAIRS, L), F32),
                        pltpu.VMEM((2, ML_ROW_KINDS * ML_PAIRS, L), F32)],
        compiler_params=pltpu.CompilerParams(
            dimension_semantics=("parallel", "arbitrary"), vmem_limit_bytes=VMEM_LIMIT),
        name="mlstm",
    )(k_conv, qt_conv, projt, gates_t, gates_t, k_conv, qt_conv, projt, gates_t, gates_t)


def _mix_kernel(att_ref, hft_ref, hbt_ref, mot_ref, x_ref, w_ref, gml_ref, gpost_ref, wdown_ref,
                o_ref, wdown_bf_ref, memt_ref):
    wdown_bf_ref[...] = wdown_ref[...].astype(BF16)
    for j in range(ML_HEADS):
        sl = slice(j * ML_V_DIM, (j + 1) * ML_V_DIM)
        h = _rms(hft_ref[sl, :] + hbt_ref[sl, :], axis=0)
        gain = jnp.concatenate([gml_ref[sl, :]] * (MIX_TM // LANES), axis=1)
        out = h * gain * _sigmoid(mot_ref[sl, :].astype(F32))
        memt_ref[sl, :] = out.astype(BF16)
    mix = jnp.dot(att_ref[...], w_ref[0:ATT_Q_COLS, :], preferred_element_type=F32)
    mix = mix + lax.dot_general(memt_ref[...], w_ref[ATT_Q_COLS:, :], _TN, preferred_element_type=F32)
    o_ref[...] = x_ref[...] + _rms(mix) * gpost_ref[...]


def _mix(att, hft, hbt, projt, x2d, w_out_bf, g_ml_b, g_post, w_down):
    t = x2d.shape[0]
    steps = t // MIX_TM
    down_rows = D_FF // steps
    row = lambda i: (i, 0)
    col = lambda i: (0, i)
    const = lambda i: (0, 0)
    return pl.pallas_call(
        _mix_kernel,
        grid=(steps,),
        in_specs=[
            pl.BlockSpec((MIX_TM, ATT_Q_COLS), row),
            pl.BlockSpec((ML_V_COLS, MIX_TM), col),
            pl.BlockSpec((ML_V_COLS, MIX_TM), col),
            pl.BlockSpec((ML_V_COLS, MIX_TM), lambda i: (R_MO // ML_V_COLS, i)),
            pl.BlockSpec((MIX_TM, D_MODEL), row),
            pl.BlockSpec((D_MODEL, D_MODEL), const, pipeline_mode=pl.Buffered(1)),
            pl.BlockSpec((ML_V_COLS, LANES), const),
            pl.BlockSpec((1, D_MODEL), const),
            pl.BlockSpec((down_rows, D_MODEL), row),
        ],
        out_specs=[pl.BlockSpec((MIX_TM, D_MODEL), row),
                   pl.BlockSpec((down_rows, D_MODEL), row)],
        out_shape=[jax.ShapeDtypeStruct((t, D_MODEL), F32),
                   jax.ShapeDtypeStruct((D_FF, D_MODEL), BF16)],
        scratch_shapes=[pltpu.VMEM((ML_V_COLS, MIX_TM), BF16)],
        compiler_params=pltpu.CompilerParams(
            dimension_semantics=("parallel",), vmem_limit_bytes=VMEM_LIMIT),
        name="mix",
    )(att, hft, hbt, projt, x2d, w_out_bf, g_ml_b, g_post, w_down)


def _mlp_kernel(x_ref, gpre_ref, wup_ref, wdown_ref, gpost_ref, o_ref, h_ref):
    j = pl.program_id(1)

    @pl.when(j == 0)
    def _():
        h_ref[...] = (_rms(x_ref[...]) * gpre_ref[...]).astype(BF16)
        o_ref[...] = jnp.zeros_like(o_ref)

    u = jnp.dot(h_ref[...], wup_ref[...], preferred_element_type=F32)
    u = jnp.square(jnp.maximum(u, 0.0)).astype(BF16)
    o_ref[...] += jnp.dot(u, wdown_ref[...], preferred_element_type=F32)

    @pl.when(j == pl.num_programs(1) - 1)
    def _():
        o_ref[...] = x_ref[...] + _rms(o_ref[...]) * gpost_ref[...]


def _mlp(x2d, g_pre, w_up, w_down, g_post):
    t = x2d.shape[0]
    return pl.pallas_call(
        _mlp_kernel,
        grid=(t // MLP_TM, D_FF // MLP_TF),
        in_specs=[
            pl.BlockSpec((MLP_TM, D_MODEL), lambda i, j: (i, 0)),
            pl.BlockSpec((1, D_MODEL), lambda i, j: (0, 0)),
            pl.BlockSpec((D_MODEL, MLP_TF), lambda i, j: (0, j)),
            pl.BlockSpec((MLP_TF, D_MODEL), lambda i, j: (j, 0)),
            pl.BlockSpec((1, D_MODEL), lambda i, j: (0, 0)),
        ],
        out_specs=pl.BlockSpec((MLP_TM, D_MODEL), lambda i, j: (i, 0)),
        out_shape=jax.ShapeDtypeStruct((t, D_MODEL), F32),
        scratch_shapes=[pltpu.VMEM((MLP_TM, D_MODEL), BF16)],
        compiler_params=pltpu.CompilerParams(
            dimension_semantics=("parallel", "arbitrary"), vmem_limit_bytes=VMEM_LIMIT),
        name="mlp",
    )(x2d, g_pre, w_up, w_down, g_post)


def _rope_tables(seq):
    half = HEAD_DIM // 2
    inv_freq = ROPE_THETA ** (-jnp.arange(half, dtype=F32) / half)
    ang = jnp.arange(seq, dtype=F32)[:, None] * inv_freq[None, :]
    cos = jnp.cos(ang)
    sin = jnp.sin(ang)
    return jnp.concatenate([cos, cos], axis=1), jnp.concatenate([-sin, sin], axis=1)


def kernel(x, w_in, conv_w, gate_bias, ml_norm_g, attn_sink, w_out, g_pre_mix, g_post_mix,
           g_pre_mlp, g_post_mlp, w_up, w_down):
    batch, seq, d = x.shape
    depth = w_in.shape[0]
    cos_t, sin_t = _rope_tables(seq)
    x2d = x.reshape(batch * seq, d)
    for l in range(depth):
        w_bf = w_in[l].astype(BF16)
        wg = jnp.pad(w_in[l][:, S_GATE:], ((0, 0), (0, LANES - GATE_COLS))).astype(BF16)
        proj, projt, gates_t, w_up_bf, w_out_bf = _in_proj(
            x2d, g_pre_mix[l][None, :], w_bf, wg, gate_bias[l][:, None], cos_t, sin_t,
            w_up[l], w_out[l], seq)
        k_conv, qt_conv = _qk_conv(proj, conv_w[l], seq)
        att = _attention(proj, attn_sink[l], batch, seq)
        hft, hbt = _mlstm(k_conv, qt_conv, projt, gates_t, batch, seq)
        g_ml_b = jnp.broadcast_to(ml_norm_g[l][:, None], (ML_V_COLS, LANES))
        x2d, w_down_bf = _mix(att, hft, hbt, projt, x2d, w_out_bf, g_ml_b, g_post_mix[l][None, :],
                              w_down[l])
        x2d = _mlp(x2d, g_pre_mlp[l][None, :], w_up_bf, w_down_bf, g_post_mlp[l][None, :])
    return x2d.reshape(batch, seq, d)
```

```python
import functools

import jax
import jax.numpy as jnp
from jax import lax
from jax.experimental import pallas as pl
from jax.experimental.pallas import tpu as pltpu

F32 = jnp.float32
BF16 = jnp.bfloat16

D_MODEL = 2048
ATT_HEADS = 8
ATT_KV_HEADS = 2
ATT_GROUP = ATT_HEADS // ATT_KV_HEADS
HEAD_DIM = 128
ATT_WINDOW = 128
ROPE_THETA = 10000.0
ML_HEADS = 4
ML_V_DIM = 256
ML_QK_DIM = 128
ML_CHUNK = 128
M_INIT = -1e30
ATT_MASKED = -1e30
LOG2_E = 1.4426950408889634
D_FF = 4 * D_MODEL
NORM_EPS = 1e-6
GATE_COLS = 4 * ML_HEADS

ATT_Q_COLS = ATT_HEADS * HEAD_DIM
ATT_KV_COLS = ATT_KV_HEADS * HEAD_DIM
ML_QK_COLS = ML_HEADS * ML_QK_DIM
ML_V_COLS = ML_HEADS * ML_V_DIM

S_AK = ATT_Q_COLS
S_AV = S_AK + ATT_KV_COLS
S_MQ = S_AV + ATT_KV_COLS
S_MV = S_MQ + 2 * ML_QK_COLS
S_GATE = S_MV + 2 * ML_V_COLS
IN_COLS = S_GATE + GATE_COLS
IN_COLS_PAD = S_GATE + 128
C_AQ = 0
C_MQK = C_AQ + ATT_Q_COLS
C_AK = C_MQK + 2 * ML_QK_COLS
C_AV = C_AK + ATT_KV_COLS
PROJ_COLS = C_AV + ATT_KV_COLS
R_MV = 0
R_MO = R_MV + ML_V_COLS
PROJT_ROWS = R_MO + ML_V_COLS
LANES = 128
SUBLANES = 8
BF16_ROWS = 16

VMEM_LIMIT = 56 * 1024 * 1024

IN_TM = 512
IN_TN = 512
CONV_TM = 1024
CONV_SUB = 256
ATT_TQ = 512
MIX_TM = 512
MLP_TM = 512
MLP_TF = 1024
ML_STATE_ROWS = ML_V_DIM + BF16_ROWS
ML_PAIRS = 2 * ML_HEADS
ML_ROW_KINDS = 7

_NT = (((1,), (1,)), ((), ()))
_TN = (((0,), (0,)), ((), ()))


def _sigmoid(x):
    return 1.0 / (1.0 + jnp.exp(-x))


def _log_sigmoid(x):
    return jnp.minimum(x, 0.0) - jnp.log(1.0 + jnp.exp(-jnp.abs(x)))


def _rms(x, axis=-1):
    return x * lax.rsqrt(jnp.mean(x * x, axis=axis, keepdims=True) + NORM_EPS)


_ROW_MAJOR_CHUNKS = ((0, C_AQ), (IN_TN, C_AQ + IN_TN), (S_AK, C_AK), (S_MQ, C_MQK),
                     (S_MQ + IN_TN, C_MQK + IN_TN))


def _in_proj_kernel(x_ref, g_ref, w_ref, gbias_ref, cos_ref, sin_ref, wup_ref, wout_ref,
                    proj_ref, projt_ref, gatet_ref, wup_bf_ref, wout_bf_ref, h_ref):
    wup_bf_ref[...] = wup_ref[...].astype(BF16)
    wout_bf_ref[...] = wout_ref[...].astype(BF16)
    h_ref[...] = (_rms(x_ref[...]) * g_ref[...]).astype(BF16)
    cos = cos_ref[...]
    sin = sin_ref[...]

    def rope(a):
        return a * cos + pltpu.roll(a, HEAD_DIM // 2, 1) * sin

    q_scale = HEAD_DIM ** -0.5 * LOG2_E
    for src, dst in _ROW_MAJOR_CHUNKS:
        acc = jnp.dot(h_ref[...], w_ref[:, src:src + IN_TN], preferred_element_type=F32)
        for k in range(IN_TN // HEAD_DIM):
            col = src + k * HEAD_DIM
            a = acc[:, k * HEAD_DIM:(k + 1) * HEAD_DIM]
            if col < S_AK:
                a = rope(a) * q_scale
            elif col < S_AV:
                a = rope(a)
            proj_ref[:, dst + k * HEAD_DIM:dst + (k + 1) * HEAD_DIM] = a.astype(BF16)
    for c in range(PROJT_ROWS // IN_TN):
        acc = jnp.dot(h_ref[...], w_ref[:, S_MV + c * IN_TN:S_MV + (c + 1) * IN_TN],
                      preferred_element_type=F32)
        projt_ref[c * IN_TN:(c + 1) * IN_TN, :] = acc.T.astype(BF16)
    gate = jnp.dot(h_ref[...], w_ref[:, S_GATE:IN_COLS_PAD], preferred_element_type=F32)
    gatet_ref[...] = gate.T[:GATE_COLS] + gbias_ref[...]


def _in_proj(x2d, g, w_bf_all, gate_bias_col, cos_t, sin_t, w_up_all, w_out_all, layer, seq):
    t = x2d.shape[0]
    steps = t // IN_TM
    pos_blocks = seq // IN_TM
    up_rows = D_MODEL // steps
    const = lambda i: (0, 0)
    row = lambda i: (i, 0)
    layer_row = lambda i: (layer, i, 0)
    return pl.pallas_call(
        _in_proj_kernel,
        grid=(steps,),
        in_specs=[
            pl.BlockSpec((IN_TM, D_MODEL), row),
            pl.BlockSpec((1, D_MODEL), const),
            pl.BlockSpec((None, D_MODEL, IN_COLS_PAD), lambda i: (layer, 0, 0), pipeline_mode=pl.Buffered(1)),
            pl.BlockSpec((GATE_COLS, 1), const),
            pl.BlockSpec((IN_TM, HEAD_DIM), lambda i: (i % pos_blocks, 0)),
            pl.BlockSpec((IN_TM, HEAD_DIM), lambda i: (i % pos_blocks, 0)),
            pl.BlockSpec((None, up_rows, D_FF), layer_row),
            pl.BlockSpec((None, up_rows, D_MODEL), layer_row),
        ],
        out_specs=[
            pl.BlockSpec((IN_TM, PROJ_COLS), row),
            pl.BlockSpec((PROJT_ROWS, IN_TM), lambda i: (0, i)),
            pl.BlockSpec((GATE_COLS, IN_TM), lambda i: (0, i)),
            pl.BlockSpec((up_rows, D_FF), row),
            pl.BlockSpec((up_rows, D_MODEL), row),
        ],
        out_shape=[
            jax.ShapeDtypeStruct((t, PROJ_COLS), BF16),
            jax.ShapeDtypeStruct((PROJT_ROWS, t), BF16),
            jax.ShapeDtypeStruct((GATE_COLS, t), F32),
            jax.ShapeDtypeStruct((D_MODEL, D_FF), BF16),
            jax.ShapeDtypeStruct((D_MODEL, D_MODEL), BF16),
        ],
        scratch_shapes=[pltpu.VMEM((IN_TM, D_MODEL), BF16)],
        compiler_params=pltpu.CompilerParams(
            dimension_semantics=("parallel",), vmem_limit_bytes=VMEM_LIMIT),
        name="in_proj",
    )(x2d, g, w_bf_all, gate_bias_col, cos_t, sin_t, w_up_all, w_out_all)


def _qk_conv_kernel(x_ref, xp_ref, xn_ref, w_ref, k_ref, qt_ref, *, seq_blocks):
    n = CONV_SUB
    pos = pl.program_id(0) % seq_blocks
    has_prev = (pos > 0).astype(F32)
    has_next = (pos < seq_blocks - 1).astype(F32)
    ri = lax.broadcasted_iota(jnp.int32, (n, n), 0)
    ci = lax.broadcasted_iota(jnp.int32, (n, n), 1)
    shift_prev = (ci == ri - 1).astype(BF16)
    shift_next = (ci == ri + 1).astype(BF16)
    rowid = lax.broadcasted_iota(jnp.int32, (n, 1), 0)
    w0 = w_ref[0:1, :]
    w1 = w_ref[1:2, :]
    w2 = w_ref[2:3, :]
    n_sub = CONV_TM // n
    for sb in range(n_sub):
        xs = x_ref[sb * n:(sb + 1) * n, :]
        x_prev = jnp.dot(shift_prev, xs, preferred_element_type=F32)
        x_next = jnp.dot(shift_next, xs, preferred_element_type=F32)
        if sb == 0:
            prev_row = xp_ref[BF16_ROWS - 1:BF16_ROWS, :].astype(F32) * has_prev
        else:
            prev_row = x_ref[sb * n - BF16_ROWS:sb * n, :].astype(F32)[BF16_ROWS - 1:BF16_ROWS]
        if sb == n_sub - 1:
            next_row = xn_ref[0:1, :].astype(F32) * has_next
        else:
            next_row = x_ref[(sb + 1) * n:(sb + 1) * n + BF16_ROWS, :].astype(F32)[0:1]
        x_prev = jnp.where(rowid == 0, prev_row, x_prev)
        x_next = jnp.where(rowid == n - 1, next_row, x_next)
        y = x_prev * w0 + xs.astype(F32) * w1 + x_next * w2
        y = y * _sigmoid(y)
        k_ref[sb * n:(sb + 1) * n, :] = y[:, ML_QK_COLS:].astype(BF16)
        q = y[:, :ML_QK_COLS] * (ML_QK_DIM ** -0.5)
        qt_ref[:, sb * n:(sb + 1) * n] = q.T.astype(BF16)


def _qk_conv(proj, conv_w, seq):
    t = proj.shape[0]
    width = 2 * ML_QK_COLS
    col = C_MQK // width
    per = CONV_TM // BF16_ROWS
    last = t // BF16_ROWS - 1
    return pl.pallas_call(
        functools.partial(_qk_conv_kernel, seq_blocks=seq // CONV_TM),
        grid=(t // CONV_TM,),
        in_specs=[
            pl.BlockSpec((CONV_TM, width), lambda i: (i, col)),
            pl.BlockSpec((BF16_ROWS, width), lambda i: (jnp.maximum(i * per - 1, 0), col)),
            pl.BlockSpec((BF16_ROWS, width), lambda i: (jnp.minimum((i + 1) * per, last), col)),
            pl.BlockSpec((3, width), lambda i: (0, 0)),
        ],
        out_specs=[pl.BlockSpec((CONV_TM, ML_QK_COLS), lambda i: (i, 0)),
                   pl.BlockSpec((ML_QK_COLS, CONV_TM), lambda i: (0, i))],
        out_shape=[jax.ShapeDtypeStruct((t, ML_QK_COLS), BF16),
                   jax.ShapeDtypeStruct((ML_QK_COLS, t), BF16)],
        compiler_params=pltpu.CompilerParams(
            dimension_semantics=("parallel",), vmem_limit_bytes=VMEM_LIMIT),
        name="qk_conv",
    )(proj, proj, proj, conv_w)


def _attn_kernel(sink_ref, q_ref, kc_ref, kp_ref, kn_ref, vc_ref, vp_ref, vn_ref,
                 o_ref, kbuf, vbuf, *, n_blocks):
    w = ATT_WINDOW
    kbuf[0:w] = kp_ref[...]
    kbuf[w:w + ATT_TQ] = kc_ref[...]
    kbuf[w + ATT_TQ:] = kn_ref[...]
    for h in range(ATT_KV_HEADS):
        hs = slice(h * HEAD_DIM, (h + 1) * HEAD_DIM)
        vbuf[h, 0:w, 0:HEAD_DIM] = vp_ref[:, hs]
        vbuf[h, w:w + ATT_TQ, 0:HEAD_DIM] = vc_ref[:, hs]
        vbuf[h, w + ATT_TQ:, 0:HEAD_DIM] = vn_ref[:, hs]
        vbuf[h, :, HEAD_DIM:] = jnp.ones((ATT_TQ + 2 * w, HEAD_DIM), BF16)
    i = pl.program_id(1)
    rows = ATT_GROUP * w
    qi = lax.broadcasted_iota(jnp.int32, (rows, 1), 0) & (w - 1)
    blk = lax.broadcasted_iota(jnp.int32, (rows, 1), 0) // w
    kj = lax.broadcasted_iota(jnp.int32, (rows, 3 * w), 1)
    kj_row = lax.broadcasted_iota(jnp.int32, (1, 3 * w), 1)
    band = jnp.where((kj >= qi) & (kj <= qi + 2 * w), 0.0, ATT_MASKED)
    n_win = ATT_TQ // w
    sinks = []
    for h in range(ATT_KV_HEADS):
        sink = jnp.full((rows, 1), sink_ref[h * ATT_GROUP], F32)
        for g in range(1, ATT_GROUP):
            sink = jnp.where(blk == g, sink_ref[h * ATT_GROUP + g], sink)
        sinks.append(sink * LOG2_E)
    tiles = [(n, h) for n in range(n_win) for h in range(ATT_KV_HEADS)]
    scores = []
    for n, h in tiles:
        qs = jnp.concatenate(
            [q_ref[n * w:(n + 1) * w, (h * ATT_GROUP + g) * HEAD_DIM:(h * ATT_GROUP + g + 1) * HEAD_DIM]
             for g in range(ATT_GROUP)], axis=0)
        kw = kbuf[n * w:(n + 3) * w, h * HEAD_DIM:(h + 1) * HEAD_DIM]
        scores.append(lax.dot_general(qs, kw, _NT, preferred_element_type=F32))
    probs = []
    maxes = []
    for idx, (n, h) in enumerate(tiles):
        bias = band
        if n == 0:
            bias = bias + jnp.where(kj_row < w, jnp.where(i == 0, ATT_MASKED, 0.0), 0.0)
        if n == n_win - 1:
            bias = bias + jnp.where(kj_row >= 2 * w, jnp.where(i == n_blocks - 1, ATT_MASKED, 0.0), 0.0)
        s = scores[idx] + bias
        m = jnp.maximum(jnp.max(s, axis=-1, keepdims=True), sinks[h])
        maxes.append(m)
        probs.append(jnp.exp2(s - m).astype(BF16))
    for idx, (n, h) in enumerate(tiles):
        o_aug = jnp.dot(probs[idx], vbuf[h, n * w:(n + 3) * w, :], preferred_element_type=F32)
        denom = o_aug[:, HEAD_DIM:] + jnp.exp2(sinks[h] - maxes[idx])
        o = o_aug[:, :HEAD_DIM] * (1.0 / denom)
        for g in range(ATT_GROUP):
            col = (h * ATT_GROUP + g) * HEAD_DIM
            o_ref[n * w:(n + 1) * w, col:col + HEAD_DIM] = o[g * w:(g + 1) * w].astype(BF16)


def _attention(proj, sink, batch, seq):
    t = proj.shape[0]
    w = ATT_WINDOW
    nq = seq // ATT_TQ
    per = ATT_TQ // w
    last_blk = t // w - 1
    cur = lambda b, i: b * nq + i
    prev = lambda b, i: jnp.maximum((b * nq + i) * per - 1, 0)
    nxt = lambda b, i: jnp.minimum((b * nq + i + 1) * per, last_blk)
    kcol = C_AK // ATT_KV_COLS
    vcol = C_AV // ATT_KV_COLS
    return pl.pallas_call(
        functools.partial(_attn_kernel, n_blocks=nq),
        grid=(batch, nq),
        in_specs=[
            pl.BlockSpec(memory_space=pltpu.SMEM),
            pl.BlockSpec((ATT_TQ, ATT_Q_COLS), lambda b, i: (cur(b, i), C_AQ // ATT_Q_COLS)),
            pl.BlockSpec((ATT_TQ, ATT_KV_COLS), lambda b, i: (cur(b, i), kcol)),
            pl.BlockSpec((w, ATT_KV_COLS), lambda b, i: (prev(b, i), kcol)),
            pl.BlockSpec((w, ATT_KV_COLS), lambda b, i: (nxt(b, i), kcol)),
            pl.BlockSpec((ATT_TQ, ATT_KV_COLS), lambda b, i: (cur(b, i), vcol)),
            pl.BlockSpec((w, ATT_KV_COLS), lambda b, i: (prev(b, i), vcol)),
            pl.BlockSpec((w, ATT_KV_COLS), lambda b, i: (nxt(b, i), vcol)),
        ],
        out_specs=pl.BlockSpec((ATT_TQ, ATT_Q_COLS), lambda b, i: (cur(b, i), 0)),
        out_shape=jax.ShapeDtypeStruct((t, ATT_Q_COLS), BF16),
        scratch_shapes=[pltpu.VMEM((ATT_TQ + 2 * w, ATT_KV_COLS), BF16),
                        pltpu.VMEM((ATT_KV_HEADS, ATT_TQ + 2 * w, 2 * HEAD_DIM), BF16)],
        compiler_params=pltpu.CompilerParams(
            dimension_semantics=("parallel", "parallel"), vmem_limit_bytes=VMEM_LIMIT),
        name="attn",
    )(sink, proj, proj, proj, proj, proj, proj, proj)


def _rows_to_cols(x):
    length = x.shape[1]
    padded = jnp.concatenate([x, jnp.zeros((length - x.shape[0], length), x.dtype)], axis=0)
    return padded.T


def _ml_gate_rows(gates_f, gates_b, m_prev):
    L = ML_CHUNK
    H = ML_HEADS
    row8 = lax.broadcasted_iota(jnp.int32, (2 * H, L), 0)
    lane = lax.broadcasted_iota(jnp.int32, (2 * H, L), 1)
    is_fwd = row8 < H
    fwd_if = gates_f[0:2 * H]
    bwd_if = gates_b[2 * H:4 * H]
    gi = jnp.where(is_fwd, fwd_if, pltpu.roll(bwd_if, H, 0))
    gf = jnp.where(is_fwd, pltpu.roll(fwd_if, H, 0), bwd_if)
    ls = _log_sigmoid(gf)
    hi = ls.astype(BF16).astype(F32)
    rem = ls - hi
    mid = rem.astype(BF16).astype(F32)
    lo = rem - mid
    parts = jnp.concatenate([hi, mid, lo, jnp.zeros_like(hi)], axis=0).astype(BF16)
    si = lax.broadcasted_iota(jnp.int32, (L, L), 0)
    ti = lax.broadcasted_iota(jnp.int32, (L, L), 1)
    pre = jnp.dot(parts, (si <= ti).astype(BF16), preferred_element_type=F32)
    prefix = pre[0:2 * H] + pre[2 * H:4 * H] + pre[4 * H:6 * H]
    gtot = jnp.sum(ls, axis=1, keepdims=True)
    b = jnp.where(is_fwd, prefix, gtot - prefix + ls)
    r = gi - b
    cm = r
    sh = 1
    while sh < L:
        from_left = jnp.where(lane >= sh, pltpu.roll(cm, sh, 1), -jnp.inf)
        from_right = jnp.where(lane < L - sh, pltpu.roll(cm, L - sh, 1), -jnp.inf)
        cm = jnp.maximum(cm, jnp.where(is_fwd, from_left, from_right))
        sh *= 2
    top = jnp.maximum(m_prev, cm)
    a = gtot - b + gi
    m_loc = jnp.max(a, axis=1, keepdims=True)
    m_new = jnp.maximum(gtot + m_prev, m_loc)
    rows = jnp.concatenate([
        -top,
        jnp.exp(m_prev - top),
        jnp.exp(-(b + top)),
        r,
        jnp.exp(a - m_loc),
        jnp.exp(gtot + m_prev - m_new),
        jnp.exp(m_loc - m_new) + jnp.zeros_like(r),
    ], axis=0)
    return rows, m_new


def _mlstm_kernel(kf_ref, qtf_ref, vtf_ref, gtf_ref, gtfn_ref, kb_ref, qtb_ref, vtb_ref, gtb_ref,
                  gtbn_ref, hf_ref, hb_ref, c_sc, m_sc, rows_sc):
    L = ML_CHUNK
    H = ML_HEADS
    P = ML_PAIRS
    c = pl.program_id(1)
    slot = c % 2

    @pl.when(c == 0)
    def _():
        c_sc[...] = jnp.zeros_like(c_sc)
        rows0, m1 = _ml_gate_rows(gtf_ref[...], gtb_ref[...], jnp.full((P, L), M_INIT, F32))
        rows_sc[0] = rows0
        m_sc[...] = m1

    rows = rows_sc[slot]
    rows_next, m_next = _ml_gate_rows(gtfn_ref[...], gtbn_ref[...], m_sc[...])
    rows_sc[1 - slot] = rows_next
    m_sc[...] = m_next

    bm, w_inter, clamp, r, w_end, s_prev, s_loc = [rows[i * P:(i + 1) * P] for i in range(ML_ROW_KINDS)]
    r_cols = _rows_to_cols(r)
    w_end_cols = _rows_to_cols(w_end)
    si = lax.broadcasted_iota(jnp.int32, (L, L), 0)
    ti = lax.broadcasted_iota(jnp.int32, (L, L), 1)
    masks = (si <= ti, si >= ti)
    n_row = lax.broadcasted_iota(jnp.int32, (BF16_ROWS, 1), 0) == 0
    refs = ((kf_ref, qtf_ref, vtf_ref, hf_ref), (kb_ref, qtb_ref, vtb_ref, hb_ref))
    pairs = [(d, j) for d in range(2) for j in range(H)]

    def k_of(d, j):
        return refs[d][0][:, j * ML_QK_DIM:(j + 1) * ML_QK_DIM]

    def qt_of(d, j):
        return refs[d][1][j * ML_QK_DIM:(j + 1) * ML_QK_DIM, :]

    scores = []
    carried = []
    for d, j in pairs:
        p = d * H + j
        scores.append(jnp.dot(k_of(d, j), qt_of(d, j), preferred_element_type=F32))
        qtw = (qt_of(d, j).astype(F32) * w_inter[p:p + 1, :]).astype(BF16)
        carried.append(jnp.dot(c_sc[p].astype(BF16), qtw, preferred_element_type=F32))
    dens = []
    n_locs = []
    rhss = []
    for d, j in pairs:
        p = d * H + j
        arg = jnp.broadcast_to(r_cols[:, p:p + 1], (L, L)) + bm[p:p + 1, :]
        sc_t = scores[p] * jnp.exp(jnp.where(masks[d], arg, -jnp.inf))
        dens.append(jnp.sum(sc_t, axis=0, keepdims=True))
        kw = k_of(d, j).astype(F32) * jnp.broadcast_to(w_end_cols[:, p:p + 1], (L, ML_QK_DIM))
        n_locs.append(jnp.sum(kw, axis=0, keepdims=True))
        rhss.append(jnp.concatenate([sc_t.astype(BF16), kw.astype(BF16)], axis=1))
    boths = []
    for d, j in pairs:
        p = d * H + j
        vt_j = refs[d][2][j * ML_V_DIM:(j + 1) * ML_V_DIM, :]
        boths.append(jnp.dot(vt_j, rhss[p], preferred_element_type=F32))
    for d, j in pairs:
        p = d * H + j
        num = boths[p][:, :L] + carried[p][:ML_V_DIM]
        den = dens[p] + carried[p][ML_V_DIM:ML_V_DIM + 1]
        refs[d][3][j * ML_V_DIM:(j + 1) * ML_V_DIM, :] = num / jnp.maximum(jnp.abs(den), clamp[p:p + 1, :])
        c_prev = c_sc[p]
        sp = s_prev[p:p + 1, :]
        sl = s_loc[p:p + 1, :]
        c_sc[p, 0:ML_V_DIM, :] = sp * c_prev[:ML_V_DIM] + sl * boths[p][:, L:]
        c_sc[p, ML_V_DIM:, :] = sp * c_prev[ML_V_DIM:] + sl * jnp.where(n_row, n_locs[p], 0.0)


def _mlstm(k_conv, qt_conv, projt, gates_t, batch, seq):
    t = k_conv.shape[0]
    L = ML_CHUNK
    nc = seq // L
    fwd = lambda b, c: b * nc + c
    bwd = lambda b, c: b * nc + nc - 1 - c
    nxt = lambda c: jnp.minimum(c + 1, nc - 1)

    def dir_specs(ch):
        return [
            pl.BlockSpec((L, ML_QK_COLS), lambda b, c: (ch(b, c), 0)),
            pl.BlockSpec((ML_QK_COLS, L), lambda b, c: (0, ch(b, c))),
            pl.BlockSpec((ML_V_COLS, L), lambda b, c: (R_MV // ML_V_COLS, ch(b, c))),
            pl.BlockSpec((GATE_COLS, L), lambda b, c: (0, ch(b, c))),
            pl.BlockSpec((GATE_COLS, L), lambda b, c: (0, ch(b, nxt(c)))),
        ]

    return pl.pallas_call(
        _mlstm_kernel,
        grid=(batch, nc),
        in_specs=dir_specs(fwd) + dir_specs(bwd),
        out_specs=[pl.BlockSpec((ML_V_COLS, L), lambda b, c: (0, fwd(b, c))),
                   pl.BlockSpec((ML_V_COLS, L), lambda b, c: (0, bwd(b, c)))],
        out_shape=[jax.ShapeDtypeStruct((ML_V_COLS, t), F32),
                   jax.ShapeDtypeStruct((ML_V_COLS, t), F32)],
        scratch_shapes=[pltpu.VMEM((ML_PAIRS, ML_STATE_ROWS, ML_QK_DIM), F32),
                        pltpu.VMEM((ML_PAIRS, L), F32),
                        pltpu.VMEM((2, ML_ROW_KINDS * ML_PAIRS, L), F32)],
        compiler_params=pltpu.CompilerParams(
            dimension_semantics=("parallel", "arbitrary"), vmem_limit_bytes=VMEM_LIMIT),
        name="mlstm",
    )(k_conv, qt_conv, projt, gates_t, gates_t, k_conv, qt_conv, projt, gates_t, gates_t)


def _mix_kernel(att_ref, hft_ref, hbt_ref, mot_ref, x_ref, w_ref, gml_ref, gpost_ref, wdown_ref,
                o_ref, wdown_bf_ref, memt_ref):
    wdown_bf_ref[...] = wdown_ref[...].astype(BF16)
    for j in range(ML_HEADS):
        sl = slice(j * ML_V_DIM, (j + 1) * ML_V_DIM)
        h = _rms(hft_ref[sl, :] + hbt_ref[sl, :], axis=0)
        gain = jnp.concatenate([gml_ref[sl, :]] * (MIX_TM // LANES), axis=1)
        out = h * gain * _sigmoid(mot_ref[sl, :].astype(F32))
        memt_ref[sl, :] = out.astype(BF16)
    mix = jnp.dot(att_ref[...], w_ref[0:ATT_Q_COLS, :], preferred_element_type=F32)
    mix = mix + lax.dot_general(memt_ref[...], w_ref[ATT_Q_COLS:, :], _TN, preferred_element_type=F32)
    o_ref[...] = x_ref[...] + _rms(mix) * gpost_ref[...]


def _mix(att, hft, hbt, projt, x2d, w_out_bf, g_ml_b, g_post, w_down_all, layer):
    t = x2d.shape[0]
    steps = t // MIX_TM
    down_rows = D_FF // steps
    row = lambda i: (i, 0)
    col = lambda i: (0, i)
    const = lambda i: (0, 0)
    return pl.pallas_call(
        _mix_kernel,
        grid=(steps,),
        in_specs=[
            pl.BlockSpec((MIX_TM, ATT_Q_COLS), row),
            pl.BlockSpec((ML_V_COLS, MIX_TM), col),
            pl.BlockSpec((ML_V_COLS, MIX_TM), col),
            pl.BlockSpec((ML_V_COLS, MIX_TM), lambda i: (R_MO // ML_V_COLS, i)),
            pl.BlockSpec((MIX_TM, D_MODEL), row),
            pl.BlockSpec((D_MODEL, D_MODEL), const, pipeline_mode=pl.Buffered(1)),
            pl.BlockSpec((ML_V_COLS, LANES), const),
            pl.BlockSpec((1, D_MODEL), const),
            pl.BlockSpec((None, down_rows, D_MODEL), lambda i: (layer, i, 0)),
        ],
        out_specs=[pl.BlockSpec((MIX_TM, D_MODEL), row),
                   pl.BlockSpec((down_rows, D_MODEL), row)],
        out_shape=[jax.ShapeDtypeStruct((t, D_MODEL), F32),
                   jax.ShapeDtypeStruct((D_FF, D_MODEL), BF16)],
        scratch_shapes=[pltpu.VMEM((ML_V_COLS, MIX_TM), BF16)],
        compiler_params=pltpu.CompilerParams(
            dimension_semantics=("parallel",), vmem_limit_bytes=VMEM_LIMIT),
        name="mix",
    )(att, hft, hbt, projt, x2d, w_out_bf, g_ml_b, g_post, w_down_all)


def _mlp_kernel(x_ref, gpre_ref, wup_ref, wdown_ref, gpost_ref, o_ref, h_ref):
    j = pl.program_id(1)

    @pl.when(j == 0)
    def _():
        h_ref[...] = (_rms(x_ref[...]) * gpre_ref[...]).astype(BF16)
        o_ref[...] = jnp.zeros_like(o_ref)

    u = jnp.dot(h_ref[...], wup_ref[...], preferred_element_type=F32)
    u = jnp.square(jnp.maximum(u, 0.0)).astype(BF16)
    o_ref[...] += jnp.dot(u, wdown_ref[...], preferred_element_type=F32)

    @pl.when(j == pl.num_programs(1) - 1)
    def _():
        o_ref[...] = x_ref[...] + _rms(o_ref[...]) * gpost_ref[...]


def _mlp(x2d, g_pre, w_up, w_down, g_post):
    t = x2d.shape[0]
    return pl.pallas_call(
        _mlp_kernel,
        grid=(t // MLP_TM, D_FF // MLP_TF),
        in_specs=[
            pl.BlockSpec((MLP_TM, D_MODEL), lambda i, j: (i, 0)),
            pl.BlockSpec((1, D_MODEL), lambda i, j: (0, 0)),
            pl.BlockSpec((D_MODEL, MLP_TF), lambda i, j: (0, j)),
            pl.BlockSpec((MLP_TF, D_MODEL), lambda i, j: (j, 0)),
            pl.BlockSpec((1, D_MODEL), lambda i, j: (0, 0)),
        ],
        out_specs=pl.BlockSpec((MLP_TM, D_MODEL), lambda i, j: (i, 0)),
        out_shape=jax.ShapeDtypeStruct((t, D_MODEL), F32),
        scratch_shapes=[pltpu.VMEM((MLP_TM, D_MODEL), BF16)],
        compiler_params=pltpu.CompilerParams(
            dimension_semantics=("parallel", "arbitrary"), vmem_limit_bytes=VMEM_LIMIT),
        name="mlp",
    )(x2d, g_pre, w_up, w_down, g_post)


def _rope_tables(seq):
    half = HEAD_DIM // 2
    inv_freq = ROPE_THETA ** (-jnp.arange(half, dtype=F32) / half)
    ang = jnp.arange(seq, dtype=F32)[:, None] * inv_freq[None, :]
    cos = jnp.cos(ang)
    sin = jnp.sin(ang)
    return jnp.concatenate([cos, cos], axis=1), jnp.concatenate([-sin, sin], axis=1)


def kernel(x, w_in, conv_w, gate_bias, ml_norm_g, attn_sink, w_out, g_pre_mix, g_post_mix,
           g_pre_mlp, g_post_mlp, w_up, w_down):
    batch, seq, d = x.shape
    depth = w_in.shape[0]
    cos_t, sin_t = _rope_tables(seq)
    x2d = x.reshape(batch * seq, d)
    w_in_bf = jnp.pad(w_in, ((0, 0), (0, 0), (0, IN_COLS_PAD - IN_COLS))).astype(BF16)
    for l in range(depth):
        proj, projt, gates_t, w_up_bf, w_out_bf = _in_proj(
            x2d, g_pre_mix[l][None, :], w_in_bf, gate_bias[l][:, None], cos_t, sin_t,
            w_up, w_out, l, seq)
        k_conv, qt_conv = _qk_conv(proj, conv_w[l], seq)
        att = _attention(proj, attn_sink[l], batch, seq)
        hft, hbt = _mlstm(k_conv, qt_conv, projt, gates_t, batch, seq)
        g_ml_b = jnp.broadcast_to(ml_norm_g[l][:, None], (ML_V_COLS, LANES))
        x2d, w_down_bf = _mix(att, hft, hbt, projt, x2d, w_out_bf, g_ml_b, g_post_mix[l][None, :],
                              w_down, l)
        x2d = _mlp(x2d, g_pre_mlp[l][None, :], w_up_bf, w_down_bf, g_post_mlp[l][None, :])
    return x2d.reshape(batch, seq, d)
```

```python
import functools

import jax
import jax.numpy as jnp
import numpy as np
from jax import lax
from jax.experimental import pallas as pl
from jax.experimental.pallas import tpu as pltpu

F32 = jnp.float32
BF16 = jnp.bfloat16

D_MODEL = 2048
ATT_HEADS = 8
ATT_KV_HEADS = 2
ATT_GROUP = ATT_HEADS // ATT_KV_HEADS
HEAD_DIM = 128
ATT_WINDOW = 128
ROPE_THETA = 10000.0
ML_HEADS = 4
ML_V_DIM = 256
ML_QK_DIM = 128
ML_CHUNK = 128
M_INIT = -1e30
ATT_MASKED = -1e30
LOG2_E = 1.4426950408889634
D_FF = 4 * D_MODEL
NORM_EPS = 1e-6
GATE_COLS = 4 * ML_HEADS

ATT_Q_COLS = ATT_HEADS * HEAD_DIM
ATT_KV_COLS = ATT_KV_HEADS * HEAD_DIM
ML_QK_COLS = ML_HEADS * ML_QK_DIM
ML_V_COLS = ML_HEADS * ML_V_DIM

S_AK = ATT_Q_COLS
S_AV = S_AK + ATT_KV_COLS
S_MQ = S_AV + ATT_KV_COLS
S_MV = S_MQ + 2 * ML_QK_COLS
S_GATE = S_MV + 2 * ML_V_COLS
IN_COLS = S_GATE + GATE_COLS
IN_COLS_PAD = S_GATE + 128
C_AQ = 0
C_MQK = C_AQ + ATT_Q_COLS
C_AK = C_MQK + 2 * ML_QK_COLS
C_AV = C_AK + ATT_KV_COLS
PROJ_COLS = C_AV + ATT_KV_COLS
R_MV = 0
R_MO = R_MV + ML_V_COLS
PROJT_ROWS = R_MO + ML_V_COLS
LANES = 128
SUBLANES = 8
BF16_ROWS = 16

VMEM_LIMIT = 56 * 1024 * 1024
BIG_VMEM_LIMIT = 60 * 1024 * 1024

IN_TM = 512
IN_TN = 512
CONV_TM = 1024
CONV_SUB = 256
ATT_TQ = 512
MIX_TM = 512
MLP_TM = 512
MLP_TF = 2048
MLP_SUB = MLP_TM // (D_FF // MLP_TF)
ML_STATE_ROWS = ML_V_DIM + BF16_ROWS
ML_PAIRS = 2 * ML_HEADS
ML_ROW_KINDS = 7

_NT = (((1,), (1,)), ((), ()))
_TN = (((0,), (0,)), ((), ()))


def _sigmoid(x):
    return 1.0 / (1.0 + jnp.exp(-x))


def _log_sigmoid(x):
    return jnp.minimum(x, 0.0) - jnp.log(1.0 + jnp.exp(-jnp.abs(x)))


def _rms(x, axis=-1):
    return x * lax.rsqrt(jnp.mean(x * x, axis=axis, keepdims=True) + NORM_EPS)


_ROW_MAJOR_CHUNKS = ((0, C_AQ), (IN_TN, C_AQ + IN_TN), (S_AK, C_AK), (S_MQ, C_MQK),
                     (S_MQ + IN_TN, C_MQK + IN_TN))


def _in_proj_kernel(x_ref, g_ref, w_ref, gbias_ref, cos_ref, sin_ref, wup_ref, wout_ref,
                    proj_ref, projt_ref, gatet_ref, wup_bf_ref, wout_bf_ref, h_ref):
    wup_bf_ref[...] = wup_ref[...].astype(BF16)
    wout_bf_ref[...] = wout_ref[...].astype(BF16)
    h_ref[...] = (_rms(x_ref[...]) * g_ref[...]).astype(BF16)
    cos = cos_ref[...]
    sin = sin_ref[...]

    def rope(a):
        return a * cos + pltpu.roll(a, HEAD_DIM // 2, 1) * sin

    q_scale = HEAD_DIM ** -0.5 * LOG2_E
    for src, dst in _ROW_MAJOR_CHUNKS:
        acc = jnp.dot(h_ref[...], w_ref[:, src:src + IN_TN], preferred_element_type=F32)
        for k in range(IN_TN // HEAD_DIM):
            col = src + k * HEAD_DIM
            a = acc[:, k * HEAD_DIM:(k + 1) * HEAD_DIM]
            if col < S_AK:
                a = rope(a) * q_scale
            elif col < S_AV:
                a = rope(a)
            proj_ref[:, dst + k * HEAD_DIM:dst + (k + 1) * HEAD_DIM] = a.astype(BF16)
    for c in range(PROJT_ROWS // IN_TN):
        acc = jnp.dot(h_ref[...], w_ref[:, S_MV + c * IN_TN:S_MV + (c + 1) * IN_TN],
                      preferred_element_type=F32)
        projt_ref[c * IN_TN:(c + 1) * IN_TN, :] = acc.T.astype(BF16)
    gate = jnp.dot(h_ref[...], w_ref[:, S_GATE:IN_COLS_PAD], preferred_element_type=F32)
    gatet_ref[...] = gate.T[:GATE_COLS] + gbias_ref[...]


def _in_proj(x2d, g, w_bf_all, gate_bias_col, cos_t, sin_t, w_up_all, w_out_all, layer, seq):
    t = x2d.shape[0]
    steps = t // IN_TM
    pos_blocks = seq // IN_TM
    up_rows = D_MODEL // steps
    const = lambda i: (0, 0)
    row = lambda i: (i, 0)
    layer_row = lambda i: (layer, i, 0)
    return pl.pallas_call(
        _in_proj_kernel,
        grid=(steps,),
        in_specs=[
            pl.BlockSpec((IN_TM, D_MODEL), row),
            pl.BlockSpec((1, D_MODEL), const),
            pl.BlockSpec((None, D_MODEL, IN_COLS_PAD), lambda i: (layer, 0, 0), pipeline_mode=pl.Buffered(1)),
            pl.BlockSpec((GATE_COLS, 1), const),
            pl.BlockSpec((IN_TM, HEAD_DIM), lambda i: (i % pos_blocks, 0)),
            pl.BlockSpec((IN_TM, HEAD_DIM), lambda i: (i % pos_blocks, 0)),
            pl.BlockSpec((None, up_rows, D_FF), layer_row),
            pl.BlockSpec((None, up_rows, D_MODEL), layer_row),
        ],
        out_specs=[
            pl.BlockSpec((IN_TM, PROJ_COLS), row),
            pl.BlockSpec((PROJT_ROWS, IN_TM), lambda i: (0, i)),
            pl.BlockSpec((GATE_COLS, IN_TM), lambda i: (0, i)),
            pl.BlockSpec((up_rows, D_FF), row),
            pl.BlockSpec((up_rows, D_MODEL), row),
        ],
        out_shape=[
            jax.ShapeDtypeStruct((t, PROJ_COLS), BF16),
            jax.ShapeDtypeStruct((PROJT_ROWS, t), BF16),
            jax.ShapeDtypeStruct((GATE_COLS, t), F32),
            jax.ShapeDtypeStruct((D_MODEL, D_FF), BF16),
            jax.ShapeDtypeStruct((D_MODEL, D_MODEL), BF16),
        ],
        scratch_shapes=[pltpu.VMEM((IN_TM, D_MODEL), BF16)],
        compiler_params=pltpu.CompilerParams(
            dimension_semantics=("parallel",), vmem_limit_bytes=VMEM_LIMIT),
        name="in_proj",
    )(x2d, g, w_bf_all, gate_bias_col, cos_t, sin_t, w_up_all, w_out_all)


def _qk_conv_kernel(x_ref, xp_ref, xn_ref, w_ref, k_ref, qt_ref, *, seq_blocks):
    n = CONV_SUB
    pos = pl.program_id(0) % seq_blocks
    has_prev = (pos > 0).astype(F32)
    has_next = (pos < seq_blocks - 1).astype(F32)
    ri = lax.broadcasted_iota(jnp.int32, (n, n), 0)
    ci = lax.broadcasted_iota(jnp.int32, (n, n), 1)
    shift_prev = (ci == ri - 1).astype(BF16)
    shift_next = (ci == ri + 1).astype(BF16)
    rowid = lax.broadcasted_iota(jnp.int32, (n, 1), 0)
    w0 = w_ref[0:1, :]
    w1 = w_ref[1:2, :]
    w2 = w_ref[2:3, :]
    n_sub = CONV_TM // n
    for sb in range(n_sub):
        xs = x_ref[sb * n:(sb + 1) * n, :]
        x_prev = jnp.dot(shift_prev, xs, preferred_element_type=F32)
        x_next = jnp.dot(shift_next, xs, preferred_element_type=F32)
        if sb == 0:
            prev_row = xp_ref[BF16_ROWS - 1:BF16_ROWS, :].astype(F32) * has_prev
        else:
            prev_row = x_ref[sb * n - BF16_ROWS:sb * n, :].astype(F32)[BF16_ROWS - 1:BF16_ROWS]
        if sb == n_sub - 1:
            next_row = xn_ref[0:1, :].astype(F32) * has_next
        else:
            next_row = x_ref[(sb + 1) * n:(sb + 1) * n + BF16_ROWS, :].astype(F32)[0:1]
        x_prev = jnp.where(rowid == 0, prev_row, x_prev)
        x_next = jnp.where(rowid == n - 1, next_row, x_next)
        y = x_prev * w0 + xs.astype(F32) * w1 + x_next * w2
        y = y * _sigmoid(y)
        k_ref[sb * n:(sb + 1) * n, :] = y[:, ML_QK_COLS:].astype(BF16)
        q = y[:, :ML_QK_COLS] * (ML_QK_DIM ** -0.5)
        qt_ref[:, sb * n:(sb + 1) * n] = q.T.astype(BF16)


def _qk_conv(proj, conv_w, seq):
    t = proj.shape[0]
    width = 2 * ML_QK_COLS
    col = C_MQK // width
    per = CONV_TM // BF16_ROWS
    last = t // BF16_ROWS - 1
    return pl.pallas_call(
        functools.partial(_qk_conv_kernel, seq_blocks=seq // CONV_TM),
        grid=(t // CONV_TM,),
        in_specs=[
            pl.BlockSpec((CONV_TM, width), lambda i: (i, col)),
            pl.BlockSpec((BF16_ROWS, width), lambda i: (jnp.maximum(i * per - 1, 0), col)),
            pl.BlockSpec((BF16_ROWS, width), lambda i: (jnp.minimum((i + 1) * per, last), col)),
            pl.BlockSpec((3, width), lambda i: (0, 0)),
        ],
        out_specs=[pl.BlockSpec((CONV_TM, ML_QK_COLS), lambda i: (i, 0)),
                   pl.BlockSpec((ML_QK_COLS, CONV_TM), lambda i: (0, i))],
        out_shape=[jax.ShapeDtypeStruct((t, ML_QK_COLS), BF16),
                   jax.ShapeDtypeStruct((ML_QK_COLS, t), BF16)],
        compiler_params=pltpu.CompilerParams(
            dimension_semantics=("parallel",), vmem_limit_bytes=VMEM_LIMIT),
        name="qk_conv",
    )(proj, proj, proj, conv_w)


def _attn_kernel(sink_ref, q_ref, kc_ref, kp_ref, kn_ref, vc_ref, vp_ref, vn_ref,
                 o_ref, kbuf, vbuf, *, n_blocks):
    w = ATT_WINDOW
    kbuf[0:w] = kp_ref[...]
    kbuf[w:w + ATT_TQ] = kc_ref[...]
    kbuf[w + ATT_TQ:] = kn_ref[...]
    for h in range(ATT_KV_HEADS):
        hs = slice(h * HEAD_DIM, (h + 1) * HEAD_DIM)
        vbuf[h, 0:w, 0:HEAD_DIM] = vp_ref[:, hs]
        vbuf[h, w:w + ATT_TQ, 0:HEAD_DIM] = vc_ref[:, hs]
        vbuf[h, w + ATT_TQ:, 0:HEAD_DIM] = vn_ref[:, hs]
        vbuf[h, :, HEAD_DIM:] = jnp.ones((ATT_TQ + 2 * w, HEAD_DIM), BF16)
    i = pl.program_id(1)
    rows = ATT_GROUP * w
    qi = lax.broadcasted_iota(jnp.int32, (rows, 1), 0) & (w - 1)
    blk = lax.broadcasted_iota(jnp.int32, (rows, 1), 0) // w
    kj = lax.broadcasted_iota(jnp.int32, (rows, 3 * w), 1)
    kj_row = lax.broadcasted_iota(jnp.int32, (1, 3 * w), 1)
    band = jnp.where((kj >= qi) & (kj <= qi + 2 * w), 0.0, ATT_MASKED)
    n_win = ATT_TQ // w
    sinks = []
    for h in range(ATT_KV_HEADS):
        sink = jnp.full((rows, 1), sink_ref[h * ATT_GROUP], F32)
        for g in range(1, ATT_GROUP):
            sink = jnp.where(blk == g, sink_ref[h * ATT_GROUP + g], sink)
        sinks.append(sink * LOG2_E)
    tiles = [(n, h) for n in range(n_win) for h in range(ATT_KV_HEADS)]
    scores = []
    for n, h in tiles:
        qs = jnp.concatenate(
            [q_ref[n * w:(n + 1) * w, (h * ATT_GROUP + g) * HEAD_DIM:(h * ATT_GROUP + g + 1) * HEAD_DIM]
             for g in range(ATT_GROUP)], axis=0)
        kw = kbuf[n * w:(n + 3) * w, h * HEAD_DIM:(h + 1) * HEAD_DIM]
        scores.append(lax.dot_general(qs, kw, _NT, preferred_element_type=F32))
    probs = []
    maxes = []
    for idx, (n, h) in enumerate(tiles):
        bias = band
        if n == 0:
            bias = bias + jnp.where(kj_row < w, jnp.where(i == 0, ATT_MASKED, 0.0), 0.0)
        if n == n_win - 1:
            bias = bias + jnp.where(kj_row >= 2 * w, jnp.where(i == n_blocks - 1, ATT_MASKED, 0.0), 0.0)
        s = scores[idx] + bias
        m = jnp.maximum(jnp.max(s, axis=-1, keepdims=True), sinks[h])
        maxes.append(m)
        probs.append(jnp.exp2(s - m).astype(BF16))
    for idx, (n, h) in enumerate(tiles):
        o_aug = jnp.dot(probs[idx], vbuf[h, n * w:(n + 3) * w, :], preferred_element_type=F32)
        denom = o_aug[:, HEAD_DIM:] + jnp.exp2(sinks[h] - maxes[idx])
        o = o_aug[:, :HEAD_DIM] * (1.0 / denom)
        for g in range(ATT_GROUP):
            col = (h * ATT_GROUP + g) * HEAD_DIM
            o_ref[n * w:(n + 1) * w, col:col + HEAD_DIM] = o[g * w:(g + 1) * w].astype(BF16)


def _attention(proj, sink, batch, seq):
    t = proj.shape[0]
    w = ATT_WINDOW
    nq = seq // ATT_TQ
    per = ATT_TQ // w
    last_blk = t // w - 1
    cur = lambda b, i: b * nq + i
    prev = lambda b, i: jnp.maximum((b * nq + i) * per - 1, 0)
    nxt = lambda b, i: jnp.minimum((b * nq + i + 1) * per, last_blk)
    kcol = C_AK // ATT_KV_COLS
    vcol = C_AV // ATT_KV_COLS
    return pl.pallas_call(
        functools.partial(_attn_kernel, n_blocks=nq),
        grid=(batch, nq),
        in_specs=[
            pl.BlockSpec(memory_space=pltpu.SMEM),
            pl.BlockSpec((ATT_TQ, ATT_Q_COLS), lambda b, i: (cur(b, i), C_AQ // ATT_Q_COLS)),
            pl.BlockSpec((ATT_TQ, ATT_KV_COLS), lambda b, i: (cur(b, i), kcol)),
            pl.BlockSpec((w, ATT_KV_COLS), lambda b, i: (prev(b, i), kcol)),
            pl.BlockSpec((w, ATT_KV_COLS), lambda b, i: (nxt(b, i), kcol)),
            pl.BlockSpec((ATT_TQ, ATT_KV_COLS), lambda b, i: (cur(b, i), vcol)),
            pl.BlockSpec((w, ATT_KV_COLS), lambda b, i: (prev(b, i), vcol)),
            pl.BlockSpec((w, ATT_KV_COLS), lambda b, i: (nxt(b, i), vcol)),
        ],
        out_specs=pl.BlockSpec((ATT_TQ, ATT_Q_COLS), lambda b, i: (cur(b, i), 0)),
        out_shape=jax.ShapeDtypeStruct((t, ATT_Q_COLS), BF16),
        scratch_shapes=[pltpu.VMEM((ATT_TQ + 2 * w, ATT_KV_COLS), BF16),
                        pltpu.VMEM((ATT_KV_HEADS, ATT_TQ + 2 * w, 2 * HEAD_DIM), BF16)],
        compiler_params=pltpu.CompilerParams(
            dimension_semantics=("parallel", "parallel"), vmem_limit_bytes=VMEM_LIMIT),
        name="attn",
    )(sink, proj, proj, proj, proj, proj, proj, proj)


def _rows_to_cols(x):
    length = x.shape[1]
    padded = jnp.concatenate([x, jnp.zeros((length - x.shape[0], length), x.dtype)], axis=0)
    return padded.T


def _ml_gate_rows(gates_f, gates_b, m_prev):
    L = ML_CHUNK
    H = ML_HEADS
    row8 = lax.broadcasted_iota(jnp.int32, (2 * H, L), 0)
    lane = lax.broadcasted_iota(jnp.int32, (2 * H, L), 1)
    is_fwd = row8 < H
    fwd_if = gates_f[0:2 * H]
    bwd_if = gates_b[2 * H:4 * H]
    gi = jnp.where(is_fwd, fwd_if, pltpu.roll(bwd_if, H, 0))
    gf = jnp.where(is_fwd, pltpu.roll(fwd_if, H, 0), bwd_if)
    ls = _log_sigmoid(gf)
    hi = ls.astype(BF16).astype(F32)
    rem = ls - hi
    mid = rem.astype(BF16).astype(F32)
    lo = rem - mid
    parts = jnp.concatenate([hi, mid, lo, jnp.zeros_like(hi)], axis=0).astype(BF16)
    si = lax.broadcasted_iota(jnp.int32, (L, L), 0)
    ti = lax.broadcasted_iota(jnp.int32, (L, L), 1)
    pre = jnp.dot(parts, (si <= ti).astype(BF16), preferred_element_type=F32)
    prefix = pre[0:2 * H] + pre[2 * H:4 * H] + pre[4 * H:6 * H]
    gtot = jnp.sum(ls, axis=1, keepdims=True)
    b = jnp.where(is_fwd, prefix, gtot - prefix + ls)
    r = gi - b
    cm = r
    sh = 1
    while sh < L:
        from_left = jnp.where(lane >= sh, pltpu.roll(cm, sh, 1), -jnp.inf)
        from_right = jnp.where(lane < L - sh, pltpu.roll(cm, L - sh, 1), -jnp.inf)
        cm = jnp.maximum(cm, jnp.where(is_fwd, from_left, from_right))
        sh *= 2
    top = jnp.maximum(m_prev, cm)
    a = gtot - b + gi
    m_loc = jnp.max(a, axis=1, keepdims=True)
    m_new = jnp.maximum(gtot + m_prev, m_loc)
    rows = jnp.concatenate([
        -top,
        jnp.exp(m_prev - top),
        jnp.exp(-(b + top)),
        r,
        jnp.exp(a - m_loc),
        jnp.exp(gtot + m_prev - m_new),
        jnp.exp(m_loc - m_new) + jnp.zeros_like(r),
    ], axis=0)
    return rows, m_new


def _mlstm_kernel(kf_ref, qtf_ref, vtf_ref, gtf_ref, gtfn_ref, kb_ref, qtb_ref, vtb_ref, gtb_ref,
                  gtbn_ref, hf_ref, hb_ref, c_sc, m_sc, rows_sc):
    L = ML_CHUNK
    H = ML_HEADS
    P = ML_PAIRS
    c = pl.program_id(1)
    slot = c % 2

    @pl.when(c == 0)
    def _():
        c_sc[...] = jnp.zeros_like(c_sc)
        rows0, m1 = _ml_gate_rows(gtf_ref[...], gtb_ref[...], jnp.full((P, L), M_INIT, F32))
        rows_sc[0] = rows0
        m_sc[...] = m1

    rows = rows_sc[slot]
    rows_next, m_next = _ml_gate_rows(gtfn_ref[...], gtbn_ref[...], m_sc[...])
    rows_sc[1 - slot] = rows_next
    m_sc[...] = m_next

    bm, w_inter, clamp, r, w_end, s_prev, s_loc = [rows[i * P:(i + 1) * P] for i in range(ML_ROW_KINDS)]
    r_cols = _rows_to_cols(r)
    w_end_cols = _rows_to_cols(w_end)
    si = lax.broadcasted_iota(jnp.int32, (L, L), 0)
    ti = lax.broadcasted_iota(jnp.int32, (L, L), 1)
    masks = (si <= ti, si >= ti)
    n_row = lax.broadcasted_iota(jnp.int32, (BF16_ROWS, 1), 0) == 0
    refs = ((kf_ref, qtf_ref, vtf_ref, hf_ref), (kb_ref, qtb_ref, vtb_ref, hb_ref))
    pairs = [(d, j) for d in range(2) for j in range(H)]

    def k_of(d, j):
        return refs[d][0][:, j * ML_QK_DIM:(j + 1) * ML_QK_DIM]

    def qt_of(d, j):
        return refs[d][1][j * ML_QK_DIM:(j + 1) * ML_QK_DIM, :]

    scores = []
    carried = []
    for d, j in pairs:
        p = d * H + j
        scores.append(jnp.dot(k_of(d, j), qt_of(d, j), preferred_element_type=F32))
        qtw = (qt_of(d, j).astype(F32) * w_inter[p:p + 1, :]).astype(BF16)
        carried.append(jnp.dot(c_sc[p].astype(BF16), qtw, preferred_element_type=F32))
    dens = []
    n_locs = []
    rhss = []
    for d, j in pairs:
        p = d * H + j
        arg = jnp.broadcast_to(r_cols[:, p:p + 1], (L, L)) + bm[p:p + 1, :]
        sc_t = scores[p] * jnp.exp(jnp.where(masks[d], arg, -jnp.inf))
        dens.append(jnp.sum(sc_t, axis=0, keepdims=True))
        kw = k_of(d, j).astype(F32) * jnp.broadcast_to(w_end_cols[:, p:p + 1], (L, ML_QK_DIM))
        n_locs.append(jnp.sum(kw, axis=0, keepdims=True))
        rhss.append(jnp.concatenate([sc_t.astype(BF16), kw.astype(BF16)], axis=1))
    boths = []
    for d, j in pairs:
        p = d * H + j
        vt_j = refs[d][2][j * ML_V_DIM:(j + 1) * ML_V_DIM, :]
        boths.append(jnp.dot(vt_j, rhss[p], preferred_element_type=F32))
    for d, j in pairs:
        p = d * H + j
        num = boths[p][:, :L] + carried[p][:ML_V_DIM]
        den = dens[p] + carried[p][ML_V_DIM:ML_V_DIM + 1]
        refs[d][3][j * ML_V_DIM:(j + 1) * ML_V_DIM, :] = num / jnp.maximum(jnp.abs(den), clamp[p:p + 1, :])
        c_prev = c_sc[p]
        sp = s_prev[p:p + 1, :]
        sl = s_loc[p:p + 1, :]
        c_sc[p, 0:ML_V_DIM, :] = sp * c_prev[:ML_V_DIM] + sl * boths[p][:, L:]
        c_sc[p, ML_V_DIM:, :] = sp * c_prev[ML_V_DIM:] + sl * jnp.where(n_row, n_locs[p], 0.0)


def _mlstm(k_conv, qt_conv, projt, gates_t, batch, seq):
    t = k_conv.shape[0]
    L = ML_CHUNK
    nc = seq // L
    fwd = lambda b, c: b * nc + c
    bwd = lambda b, c: b * nc + nc - 1 - c
    nxt = lambda c: jnp.minimum(c + 1, nc - 1)

    def dir_specs(ch):
        return [
            pl.BlockSpec((L, ML_QK_COLS), lambda b, c: (ch(b, c), 0)),
            pl.BlockSpec((ML_QK_COLS, L), lambda b, c: (0, ch(b, c))),
            pl.BlockSpec((ML_V_COLS, L), lambda b, c: (R_MV // ML_V_COLS, ch(b, c))),
            pl.BlockSpec((GATE_COLS, L), lambda b, c: (0, ch(b, c))),
            pl.BlockSpec((GATE_COLS, L), lambda b, c: (0, ch(b, nxt(c)))),
        ]

    return pl.pallas_call(
        _mlstm_kernel,
        grid=(batch, nc),
        in_specs=dir_specs(fwd) + dir_specs(bwd),
        out_specs=[pl.BlockSpec((ML_V_COLS, L), lambda b, c: (0, fwd(b, c))),
                   pl.BlockSpec((ML_V_COLS, L), lambda b, c: (0, bwd(b, c)))],
        out_shape=[jax.ShapeDtypeStruct((ML_V_COLS, t), F32),
                   jax.ShapeDtypeStruct((ML_V_COLS, t), F32)],
        scratch_shapes=[pltpu.VMEM((ML_PAIRS, ML_STATE_ROWS, ML_QK_DIM), F32),
                        pltpu.VMEM((ML_PAIRS, L), F32),
                        pltpu.VMEM((2, ML_ROW_KINDS * ML_PAIRS, L), F32)],
        compiler_params=pltpu.CompilerParams(
            dimension_semantics=("parallel", "arbitrary"), vmem_limit_bytes=VMEM_LIMIT),
        name="mlstm",
    )(k_conv, qt_conv, projt, gates_t, gates_t, k_conv, qt_conv, projt, gates_t, gates_t)


def _mix_kernel(att_ref, hft_ref, hbt_ref, mot_ref, x_ref, w_ref, gml_ref, gpost_ref, wdown_ref,
                o_ref, wdown_bf_ref, cat_ref):
    wdown_bf_ref[...] = wdown_ref[...].astype(BF16)
    cat_ref[:, 0:ATT_Q_COLS] = att_ref[...]
    for j in range(ML_HEADS):
        sl = slice(j * ML_V_DIM, (j + 1) * ML_V_DIM)
        h = _rms(hft_ref[sl, :] + hbt_ref[sl, :], axis=0)
        gain = jnp.concatenate([gml_ref[sl, :]] * (MIX_TM // LANES), axis=1)
        mem = h * gain * _sigmoid(mot_ref[sl, :].astype(F32))
        cat_ref[:, ATT_Q_COLS + j * ML_V_DIM:ATT_Q_COLS + (j + 1) * ML_V_DIM] = mem.T.astype(BF16)
    mix = jnp.dot(cat_ref[...], w_ref[...], preferred_element_type=F32)
    o_ref[...] = x_ref[...] + _rms(mix) * gpost_ref[...]


def _mix(att, hft, hbt, projt, x2d, w_out_bf, g_ml_b, g_post, w_down_all, layer):
    t = x2d.shape[0]
    steps = t // MIX_TM
    down_rows = D_FF // steps
    row = lambda i: (i, 0)
    col = lambda i: (0, i)
    const = lambda i: (0, 0)
    return pl.pallas_call(
        _mix_kernel,
        grid=(steps,),
        in_specs=[
            pl.BlockSpec((MIX_TM, ATT_Q_COLS), row),
            pl.BlockSpec((ML_V_COLS, MIX_TM), col),
            pl.BlockSpec((ML_V_COLS, MIX_TM), col),
            pl.BlockSpec((ML_V_COLS, MIX_TM), lambda i: (R_MO // ML_V_COLS, i)),
            pl.BlockSpec((MIX_TM, D_MODEL), row),
            pl.BlockSpec((D_MODEL, D_MODEL), const, pipeline_mode=pl.Buffered(1)),
            pl.BlockSpec((ML_V_COLS, LANES), const),
            pl.BlockSpec((1, D_MODEL), const),
            pl.BlockSpec((None, down_rows, D_MODEL), lambda i: (layer, i, 0)),
        ],
        out_specs=[pl.BlockSpec((MIX_TM, D_MODEL), row),
                   pl.BlockSpec((down_rows, D_MODEL), row)],
        out_shape=[jax.ShapeDtypeStruct((t, D_MODEL), F32),
                   jax.ShapeDtypeStruct((D_FF, D_MODEL), BF16)],
        scratch_shapes=[pltpu.VMEM((MIX_TM, D_MODEL), BF16)],
        compiler_params=pltpu.CompilerParams(
            dimension_semantics=("parallel",), vmem_limit_bytes=VMEM_LIMIT),
        name="mix",
    )(att, hft, hbt, projt, x2d, w_out_bf, g_ml_b, g_post, w_down_all)


def _mlp_kernel(xn_ref, xp_ref, gpre_ref, wup_ref, wdown_ref, gpost_ref, o_ref,
                h_a, h_b, acc_a, acc_b, *, n_blocks):
    blk = pl.program_id(0) - 1
    j = pl.program_id(1)
    rows = pl.ds(pl.multiple_of(j * MLP_SUB, MLP_SUB), MLP_SUB)

    def pre_norm(h_next):
        h_next[rows, :] = (_rms(xn_ref[...]) * gpre_ref[...]).astype(BF16)

    def finish(acc_prev):
        o_ref[...] = xp_ref[...] + _rms(acc_prev[rows, :]) * gpost_ref[...]
        acc_prev[rows, :] = jnp.zeros((MLP_SUB, D_MODEL), F32)

    def main(h_cur, h_next, acc_cur, acc_prev):
        finish(acc_prev)
        u = jnp.dot(h_cur[...], wup_ref[...], preferred_element_type=F32)
        u = jnp.square(jnp.maximum(u, 0.0)).astype(BF16)
        acc_cur[...] += jnp.dot(u, wdown_ref[...], preferred_element_type=F32)
        pre_norm(h_next)

    @pl.when(blk < 0)
    def _():
        pre_norm(h_a)
        acc_a[rows, :] = jnp.zeros((MLP_SUB, D_MODEL), F32)
        acc_b[rows, :] = jnp.zeros((MLP_SUB, D_MODEL), F32)

    in_range = (blk >= 0) & (blk < n_blocks)
    even = (blk % 2) == 0

    @pl.when(in_range & even)
    def _():
        main(h_a, h_b, acc_a, acc_b)

    @pl.when(in_range & jnp.logical_not(even))
    def _():
        main(h_b, h_a, acc_b, acc_a)

    @pl.when(blk == n_blocks)
    def _():
        finish(acc_a if (n_blocks - 1) % 2 == 0 else acc_b)


def _mlp(x2d, g_pre, w_up, w_down, g_post):
    t = x2d.shape[0]
    ni = t // MLP_TM
    nj = D_FF // MLP_TF
    clamp = lambda blk: jnp.clip(blk, 0, ni - 1)
    sub_row = lambda blk, j: (clamp(blk) * nj + j, 0)
    wj = lambda i, j: jnp.where(i < 1, 0, jnp.where(i > ni, nj - 1, j))
    return pl.pallas_call(
        functools.partial(_mlp_kernel, n_blocks=ni),
        grid=(ni + 2, nj),
        in_specs=[
            pl.BlockSpec((MLP_SUB, D_MODEL), lambda i, j: sub_row(i, j)),
            pl.BlockSpec((MLP_SUB, D_MODEL), lambda i, j: sub_row(i - 2, j)),
            pl.BlockSpec((1, D_MODEL), lambda i, j: (0, 0)),
            pl.BlockSpec((D_MODEL, MLP_TF), lambda i, j: (0, wj(i, j))),
            pl.BlockSpec((MLP_TF, D_MODEL), lambda i, j: (wj(i, j), 0)),
            pl.BlockSpec((1, D_MODEL), lambda i, j: (0, 0)),
        ],
        out_specs=pl.BlockSpec((MLP_SUB, D_MODEL), lambda i, j: (jnp.where(i < 2, 0, (i - 2) * nj + j), 0)),
        out_shape=jax.ShapeDtypeStruct((t, D_MODEL), F32),
        scratch_shapes=[pltpu.VMEM((MLP_TM, D_MODEL), BF16), pltpu.VMEM((MLP_TM, D_MODEL), BF16),
                        pltpu.VMEM((MLP_TM, D_MODEL), F32), pltpu.VMEM((MLP_TM, D_MODEL), F32)],
        compiler_params=pltpu.CompilerParams(
            dimension_semantics=("arbitrary", "arbitrary"), vmem_limit_bytes=BIG_VMEM_LIMIT),
        name="mlp",
    )(x2d, x2d, g_pre, w_up, w_down, g_post)


def _rope_tables(seq):
    half = HEAD_DIM // 2
    inv_freq = ROPE_THETA ** (-np.arange(half, dtype=np.float64) / half)
    ang = np.arange(seq, dtype=np.float64)[:, None] * inv_freq[None, :]
    cos = np.cos(ang)
    sin = np.sin(ang)
    cos_t = np.concatenate([cos, cos], axis=1).astype(np.float32)
    sin_t = np.concatenate([-sin, sin], axis=1).astype(np.float32)
    return jnp.asarray(cos_t), jnp.asarray(sin_t)


def kernel(x, w_in, conv_w, gate_bias, ml_norm_g, attn_sink, w_out, g_pre_mix, g_post_mix,
           g_pre_mlp, g_post_mlp, w_up, w_down):
    batch, seq, d = x.shape
    depth = w_in.shape[0]
    cos_t, sin_t = _rope_tables(seq)
    x2d = x.reshape(batch * seq, d)
    w_in_bf = jnp.pad(w_in, ((0, 0), (0, 0), (0, IN_COLS_PAD - IN_COLS))).astype(BF16)
    for l in range(depth):
        proj, projt, gates_t, w_up_bf, w_out_bf = _in_proj(
            x2d, g_pre_mix[l][None, :], w_in_bf, gate_bias[l][:, None], cos_t, sin_t,
            w_up, w_out, l, seq)
        k_conv, qt_conv = _qk_conv(proj, conv_w[l], seq)
        att = _attention(proj, attn_sink[l], batch, seq)
        hft, hbt = _mlstm(k_conv, qt_conv, projt, gates_t, batch, seq)
        g_ml_b = jnp.broadcast_to(ml_norm_g[l][:, None], (ML_V_COLS, LANES))
        x2d, w_down_bf = _mix(att, hft, hbt, projt, x2d, w_out_bf, g_ml_b, g_post_mix[l][None, :],
                              w_down, l)
        x2d = _mlp(x2d, g_pre_mlp[l][None, :], w_up_bf, w_down_bf, g_post_mlp[l][None, :])
    return x2d.reshape(batch, seq, d)
```

```python
import functools

import jax
import jax.numpy as jnp
import numpy as np
from jax import lax
from jax.experimental import pallas as pl
from jax.experimental.pallas import tpu as pltpu

F32 = jnp.float32
BF16 = jnp.bfloat16

D_MODEL = 2048
ATT_HEADS = 8
ATT_KV_HEADS = 2
ATT_GROUP = ATT_HEADS // ATT_KV_HEADS
HEAD_DIM = 128
ATT_WINDOW = 128
ROPE_THETA = 10000.0
ML_HEADS = 4
ML_V_DIM = 256
ML_QK_DIM = 128
ML_CHUNK = 128
M_INIT = -1e30
ATT_MASKED = -1e30
LOG2_E = 1.4426950408889634
D_FF = 4 * D_MODEL
NORM_EPS = 1e-6
GATE_COLS = 4 * ML_HEADS

ATT_Q_COLS = ATT_HEADS * HEAD_DIM
ATT_KV_COLS = ATT_KV_HEADS * HEAD_DIM
ML_QK_COLS = ML_HEADS * ML_QK_DIM
ML_V_COLS = ML_HEADS * ML_V_DIM

S_AK = ATT_Q_COLS
S_AV = S_AK + ATT_KV_COLS
S_MQ = S_AV + ATT_KV_COLS
S_MV = S_MQ + 2 * ML_QK_COLS
S_GATE = S_MV + 2 * ML_V_COLS
IN_COLS = S_GATE + GATE_COLS
IN_COLS_PAD = S_GATE + 128
C_AQ = 0
C_MQK = C_AQ + ATT_Q_COLS
C_AK = C_MQK + 2 * ML_QK_COLS
C_AV = C_AK + ATT_KV_COLS
PROJ_COLS = C_AV + ATT_KV_COLS
R_MV = 0
R_MO = R_MV + ML_V_COLS
PROJT_ROWS = R_MO + ML_V_COLS
LANES = 128
SUBLANES = 8
BF16_ROWS = 16

VMEM_LIMIT = 56 * 1024 * 1024
BIG_VMEM_LIMIT = 60 * 1024 * 1024

IN_TM = 512
IN_TN = 512
CONV_TM = 1024
CONV_SUB = 256
ATT_TQ = 512
MIX_TM = 512
MLP_TM = 512
MLP_TF = 2048
MLP_UP_CHUNK = 512
MLP_SUB = MLP_TM // (D_FF // MLP_TF)
ML_STATE_ROWS = ML_V_DIM + BF16_ROWS
ML_PAIRS = 2 * ML_HEADS
ML_ROW_KINDS = 7

_NT = (((1,), (1,)), ((), ()))
_TN = (((0,), (0,)), ((), ()))


def _sigmoid(x):
    return 1.0 / (1.0 + jnp.exp(-x))


def _log_sigmoid(x):
    return jnp.minimum(x, 0.0) - jnp.log(1.0 + jnp.exp(-jnp.abs(x)))


def _rms(x, axis=-1):
    return x * lax.rsqrt(jnp.mean(x * x, axis=axis, keepdims=True) + NORM_EPS)


_ROW_MAJOR_CHUNKS = ((0, C_AQ), (IN_TN, C_AQ + IN_TN), (S_AK, C_AK), (S_MQ, C_MQK),
                     (S_MQ + IN_TN, C_MQK + IN_TN))


def _in_proj_kernel(x_ref, g_ref, w_ref, gbias_ref, cos_ref, sin_ref, wup_ref, wout_ref,
                    proj_ref, projt_ref, gatet_ref, wup_bf_ref, wout_bf_ref, h_ref):
    wup_bf_ref[...] = wup_ref[...].astype(BF16)
    wout_bf_ref[...] = wout_ref[...].astype(BF16)
    h_ref[...] = (_rms(x_ref[...]) * g_ref[...]).astype(BF16)
    cos = cos_ref[...]
    sin = sin_ref[...]

    def rope(a):
        return a * cos + pltpu.roll(a, HEAD_DIM // 2, 1) * sin

    q_scale = HEAD_DIM ** -0.5 * LOG2_E
    for src, dst in _ROW_MAJOR_CHUNKS:
        acc = jnp.dot(h_ref[...], w_ref[:, src:src + IN_TN], preferred_element_type=F32)
        for k in range(IN_TN // HEAD_DIM):
            col = src + k * HEAD_DIM
            a = acc[:, k * HEAD_DIM:(k + 1) * HEAD_DIM]
            if col < S_AK:
                a = rope(a) * q_scale
            elif col < S_AV:
                a = rope(a)
            proj_ref[:, dst + k * HEAD_DIM:dst + (k + 1) * HEAD_DIM] = a.astype(BF16)
    for c in range(PROJT_ROWS // IN_TN):
        acc = jnp.dot(h_ref[...], w_ref[:, S_MV + c * IN_TN:S_MV + (c + 1) * IN_TN],
                      preferred_element_type=F32)
        projt_ref[c * IN_TN:(c + 1) * IN_TN, :] = acc.T.astype(BF16)
    gate = jnp.dot(h_ref[...], w_ref[:, S_GATE:IN_COLS_PAD], preferred_element_type=F32)
    gatet_ref[...] = gate.T[:GATE_COLS] + gbias_ref[...]


def _in_proj(x2d, g, w_bf_all, gate_bias_col, cos_t, sin_t, w_up_all, w_out_all, layer, seq):
    t = x2d.shape[0]
    steps = t // IN_TM
    pos_blocks = seq // IN_TM
    up_rows = D_MODEL // steps
    const = lambda i: (0, 0)
    row = lambda i: (i, 0)
    layer_row = lambda i: (layer, i, 0)
    return pl.pallas_call(
        _in_proj_kernel,
        grid=(steps,),
        in_specs=[
            pl.BlockSpec((IN_TM, D_MODEL), row),
            pl.BlockSpec((1, D_MODEL), const),
            pl.BlockSpec((None, D_MODEL, IN_COLS_PAD), lambda i: (layer, 0, 0), pipeline_mode=pl.Buffered(1)),
            pl.BlockSpec((GATE_COLS, 1), const),
            pl.BlockSpec((IN_TM, HEAD_DIM), lambda i: (i % pos_blocks, 0)),
            pl.BlockSpec((IN_TM, HEAD_DIM), lambda i: (i % pos_blocks, 0)),
            pl.BlockSpec((None, up_rows, D_FF), layer_row),
            pl.BlockSpec((None, up_rows, D_MODEL), layer_row),
        ],
        out_specs=[
            pl.BlockSpec((IN_TM, PROJ_COLS), row),
            pl.BlockSpec((PROJT_ROWS, IN_TM), lambda i: (0, i)),
            pl.BlockSpec((GATE_COLS, IN_TM), lambda i: (0, i)),
            pl.BlockSpec((up_rows, D_FF), row),
            pl.BlockSpec((up_rows, D_MODEL), row),
        ],
        out_shape=[
            jax.ShapeDtypeStruct((t, PROJ_COLS), BF16),
            jax.ShapeDtypeStruct((PROJT_ROWS, t), BF16),
            jax.ShapeDtypeStruct((GATE_COLS, t), F32),
            jax.ShapeDtypeStruct((D_MODEL, D_FF), BF16),
            jax.ShapeDtypeStruct((D_MODEL, D_MODEL), BF16),
        ],
        scratch_shapes=[pltpu.VMEM((IN_TM, D_MODEL), BF16)],
        compiler_params=pltpu.CompilerParams(
            dimension_semantics=("parallel",), vmem_limit_bytes=VMEM_LIMIT),
        name="in_proj",
    )(x2d, g, w_bf_all, gate_bias_col, cos_t, sin_t, w_up_all, w_out_all)


def _qk_conv_kernel(x_ref, xp_ref, xn_ref, w_ref, k_ref, qt_ref, *, seq_blocks):
    n = CONV_SUB
    pos = pl.program_id(0) % seq_blocks
    has_prev = (pos > 0).astype(F32)
    has_next = (pos < seq_blocks - 1).astype(F32)
    ri = lax.broadcasted_iota(jnp.int32, (n, n), 0)
    ci = lax.broadcasted_iota(jnp.int32, (n, n), 1)
    shift_prev = (ci == ri - 1).astype(BF16)
    shift_next = (ci == ri + 1).astype(BF16)
    rowid = lax.broadcasted_iota(jnp.int32, (n, 1), 0)
    w0 = w_ref[0:1, :]
    w1 = w_ref[1:2, :]
    w2 = w_ref[2:3, :]
    n_sub = CONV_TM // n
    for sb in range(n_sub):
        xs = x_ref[sb * n:(sb + 1) * n, :]
        x_prev = jnp.dot(shift_prev, xs, preferred_element_type=F32)
        x_next = jnp.dot(shift_next, xs, preferred_element_type=F32)
        if sb == 0:
            prev_row = xp_ref[BF16_ROWS - 1:BF16_ROWS, :].astype(F32) * has_prev
        else:
            prev_row = x_ref[sb * n - BF16_ROWS:sb * n, :].astype(F32)[BF16_ROWS - 1:BF16_ROWS]
        if sb == n_sub - 1:
            next_row = xn_ref[0:1, :].astype(F32) * has_next
        else:
            next_row = x_ref[(sb + 1) * n:(sb + 1) * n + BF16_ROWS, :].astype(F32)[0:1]
        x_prev = jnp.where(rowid == 0, prev_row, x_prev)
        x_next = jnp.where(rowid == n - 1, next_row, x_next)
        y = x_prev * w0 + xs.astype(F32) * w1 + x_next * w2
        y = y * _sigmoid(y)
        k_ref[sb * n:(sb + 1) * n, :] = y[:, ML_QK_COLS:].astype(BF16)
        q = y[:, :ML_QK_COLS] * (ML_QK_DIM ** -0.5)
        qt_ref[:, sb * n:(sb + 1) * n] = q.T.astype(BF16)


def _qk_conv(proj, conv_w, seq):
    t = proj.shape[0]
    width = 2 * ML_QK_COLS
    col = C_MQK // width
    per = CONV_TM // BF16_ROWS
    last = t // BF16_ROWS - 1
    return pl.pallas_call(
        functools.partial(_qk_conv_kernel, seq_blocks=seq // CONV_TM),
        grid=(t // CONV_TM,),
        in_specs=[
            pl.BlockSpec((CONV_TM, width), lambda i: (i, col)),
            pl.BlockSpec((BF16_ROWS, width), lambda i: (jnp.maximum(i * per - 1, 0), col)),
            pl.BlockSpec((BF16_ROWS, width), lambda i: (jnp.minimum((i + 1) * per, last), col)),
            pl.BlockSpec((3, width), lambda i: (0, 0)),
        ],
        out_specs=[pl.BlockSpec((CONV_TM, ML_QK_COLS), lambda i: (i, 0)),
                   pl.BlockSpec((ML_QK_COLS, CONV_TM), lambda i: (0, i))],
        out_shape=[jax.ShapeDtypeStruct((t, ML_QK_COLS), BF16),
                   jax.ShapeDtypeStruct((ML_QK_COLS, t), BF16)],
        compiler_params=pltpu.CompilerParams(
            dimension_semantics=("parallel",), vmem_limit_bytes=VMEM_LIMIT),
        name="qk_conv",
    )(proj, proj, proj, conv_w)


def _attn_kernel(sink_ref, q_ref, kc_ref, kp_ref, kn_ref, vc_ref, vp_ref, vn_ref,
                 o_ref, kbuf, vbuf, *, n_blocks):
    w = ATT_WINDOW
    kbuf[0:w] = kp_ref[...]
    kbuf[w:w + ATT_TQ] = kc_ref[...]
    kbuf[w + ATT_TQ:] = kn_ref[...]
    for h in range(ATT_KV_HEADS):
        hs = slice(h * HEAD_DIM, (h + 1) * HEAD_DIM)
        vbuf[h, 0:w, 0:HEAD_DIM] = vp_ref[:, hs]
        vbuf[h, w:w + ATT_TQ, 0:HEAD_DIM] = vc_ref[:, hs]
        vbuf[h, w + ATT_TQ:, 0:HEAD_DIM] = vn_ref[:, hs]
        vbuf[h, :, HEAD_DIM:] = jnp.ones((ATT_TQ + 2 * w, HEAD_DIM), BF16)
    i = pl.program_id(1)
    rows = ATT_GROUP * w
    qi = lax.broadcasted_iota(jnp.int32, (rows, 1), 0) & (w - 1)
    blk = lax.broadcasted_iota(jnp.int32, (rows, 1), 0) // w
    kj = lax.broadcasted_iota(jnp.int32, (rows, 3 * w), 1)
    kj_row = lax.broadcasted_iota(jnp.int32, (1, 3 * w), 1)
    band = jnp.where((kj >= qi) & (kj <= qi + 2 * w), 0.0, ATT_MASKED)
    n_win = ATT_TQ // w
    sinks = []
    for h in range(ATT_KV_HEADS):
        sink = jnp.full((rows, 1), sink_ref[h * ATT_GROUP], F32)
        for g in range(1, ATT_GROUP):
            sink = jnp.where(blk == g, sink_ref[h * ATT_GROUP + g], sink)
        sinks.append(sink * LOG2_E)
    tiles = [(n, h) for n in range(n_win) for h in range(ATT_KV_HEADS)]
    scores = []
    for n, h in tiles:
        qs = jnp.concatenate(
            [q_ref[n * w:(n + 1) * w, (h * ATT_GROUP + g) * HEAD_DIM:(h * ATT_GROUP + g + 1) * HEAD_DIM]
             for g in range(ATT_GROUP)], axis=0)
        kw = kbuf[n * w:(n + 3) * w, h * HEAD_DIM:(h + 1) * HEAD_DIM]
        scores.append(lax.dot_general(qs, kw, _NT, preferred_element_type=F32))
    probs = []
    maxes = []
    for idx, (n, h) in enumerate(tiles):
        bias = band
        if n == 0:
            bias = bias + jnp.where(kj_row < w, jnp.where(i == 0, ATT_MASKED, 0.0), 0.0)
        if n == n_win - 1:
            bias = bias + jnp.where(kj_row >= 2 * w, jnp.where(i == n_blocks - 1, ATT_MASKED, 0.0), 0.0)
        s = scores[idx] + bias
        m = jnp.maximum(jnp.max(s, axis=-1, keepdims=True), sinks[h])
        maxes.append(m)
        probs.append(jnp.exp2(s - m).astype(BF16))
    for idx, (n, h) in enumerate(tiles):
        o_aug = jnp.dot(probs[idx], vbuf[h, n * w:(n + 3) * w, :], preferred_element_type=F32)
        denom = o_aug[:, HEAD_DIM:] + jnp.exp2(sinks[h] - maxes[idx])
        o = o_aug[:, :HEAD_DIM] * (1.0 / denom)
        for g in range(ATT_GROUP):
            col = (h * ATT_GROUP + g) * HEAD_DIM
            o_ref[n * w:(n + 1) * w, col:col + HEAD_DIM] = o[g * w:(g + 1) * w].astype(BF16)


def _attention(proj, sink, batch, seq):
    t = proj.shape[0]
    w = ATT_WINDOW
    nq = seq // ATT_TQ
    per = ATT_TQ // w
    last_blk = t // w - 1
    cur = lambda b, i: b * nq + i
    prev = lambda b, i: jnp.maximum((b * nq + i) * per - 1, 0)
    nxt = lambda b, i: jnp.minimum((b * nq + i + 1) * per, last_blk)
    kcol = C_AK // ATT_KV_COLS
    vcol = C_AV // ATT_KV_COLS
    return pl.pallas_call(
        functools.partial(_attn_kernel, n_blocks=nq),
        grid=(batch, nq),
        in_specs=[
            pl.BlockSpec(memory_space=pltpu.SMEM),
            pl.BlockSpec((ATT_TQ, ATT_Q_COLS), lambda b, i: (cur(b, i), C_AQ // ATT_Q_COLS)),
            pl.BlockSpec((ATT_TQ, ATT_KV_COLS), lambda b, i: (cur(b, i), kcol)),
            pl.BlockSpec((w, ATT_KV_COLS), lambda b, i: (prev(b, i), kcol)),
            pl.BlockSpec((w, ATT_KV_COLS), lambda b, i: (nxt(b, i), kcol)),
            pl.BlockSpec((ATT_TQ, ATT_KV_COLS), lambda b, i: (cur(b, i), vcol)),
            pl.BlockSpec((w, ATT_KV_COLS), lambda b, i: (prev(b, i), vcol)),
            pl.BlockSpec((w, ATT_KV_COLS), lambda b, i: (nxt(b, i), vcol)),
        ],
        out_specs=pl.BlockSpec((ATT_TQ, ATT_Q_COLS), lambda b, i: (cur(b, i), 0)),
        out_shape=jax.ShapeDtypeStruct((t, ATT_Q_COLS), BF16),
        scratch_shapes=[pltpu.VMEM((ATT_TQ + 2 * w, ATT_KV_COLS), BF16),
                        pltpu.VMEM((ATT_KV_HEADS, ATT_TQ + 2 * w, 2 * HEAD_DIM), BF16)],
        compiler_params=pltpu.CompilerParams(
            dimension_semantics=("parallel", "parallel"), vmem_limit_bytes=VMEM_LIMIT),
        name="attn",
    )(sink, proj, proj, proj, proj, proj, proj, proj)


def _rows_to_cols(x):
    length = x.shape[1]
    padded = jnp.concatenate([x, jnp.zeros((length - x.shape[0], length), x.dtype)], axis=0)
    return padded.T


def _ml_gate_rows(gates_f, gates_b, m_prev):
    L = ML_CHUNK
    H = ML_HEADS
    row8 = lax.broadcasted_iota(jnp.int32, (2 * H, L), 0)
    lane = lax.broadcasted_iota(jnp.int32, (2 * H, L), 1)
    is_fwd = row8 < H
    fwd_if = gates_f[0:2 * H]
    bwd_if = gates_b[2 * H:4 * H]
    gi = jnp.where(is_fwd, fwd_if, pltpu.roll(bwd_if, H, 0))
    gf = jnp.where(is_fwd, pltpu.roll(fwd_if, H, 0), bwd_if)
    ls = _log_sigmoid(gf)
    hi = ls.astype(BF16).astype(F32)
    rem = ls - hi
    mid = rem.astype(BF16).astype(F32)
    lo = rem - mid
    parts = jnp.concatenate([hi, mid, lo, jnp.zeros_like(hi)], axis=0).astype(BF16)
    si = lax.broadcasted_iota(jnp.int32, (L, L), 0)
    ti = lax.broadcasted_iota(jnp.int32, (L, L), 1)
    pre = jnp.dot(parts, (si <= ti).astype(BF16), preferred_element_type=F32)
    prefix = pre[0:2 * H] + pre[2 * H:4 * H] + pre[4 * H:6 * H]
    gtot = jnp.sum(ls, axis=1, keepdims=True)
    b = jnp.where(is_fwd, prefix, gtot - prefix + ls)
    r = gi - b
    cm = r
    sh = 1
    while sh < L:
        from_left = jnp.where(lane >= sh, pltpu.roll(cm, sh, 1), -jnp.inf)
        from_right = jnp.where(lane < L - sh, pltpu.roll(cm, L - sh, 1), -jnp.inf)
        cm = jnp.maximum(cm, jnp.where(is_fwd, from_left, from_right))
        sh *= 2
    top = jnp.maximum(m_prev, cm)
    a = gtot - b + gi
    m_loc = jnp.max(a, axis=1, keepdims=True)
    m_new = jnp.maximum(gtot + m_prev, m_loc)
    rows = jnp.concatenate([
        -top,
        jnp.exp(m_prev - top),
        jnp.exp(-(b + top)),
        r,
        jnp.exp(a - m_loc),
        jnp.exp(gtot + m_prev - m_new),
        jnp.exp(m_loc - m_new) + jnp.zeros_like(r),
    ], axis=0)
    return rows, m_new


def _mlstm_kernel(kf_ref, qtf_ref, vtf_ref, gtf_ref, gtfn_ref, kb_ref, qtb_ref, vtb_ref, gtb_ref,
                  gtbn_ref, hf_ref, hb_ref, c_sc, m_sc, rows_sc):
    L = ML_CHUNK
    H = ML_HEADS
    P = ML_PAIRS
    c = pl.program_id(1)
    slot = c % 2

    @pl.when(c == 0)
    def _():
        c_sc[...] = jnp.zeros_like(c_sc)
        rows0, m1 = _ml_gate_rows(gtf_ref[...], gtb_ref[...], jnp.full((P, L), M_INIT, F32))
        rows_sc[0] = rows0
        m_sc[...] = m1

    rows = rows_sc[slot]
    rows_next, m_next = _ml_gate_rows(gtfn_ref[...], gtbn_ref[...], m_sc[...])
    rows_sc[1 - slot] = rows_next
    m_sc[...] = m_next

    bm, w_inter, clamp, r, w_end, s_prev, s_loc = [rows[i * P:(i + 1) * P] for i in range(ML_ROW_KINDS)]
    r_cols = _rows_to_cols(r)
    w_end_cols = _rows_to_cols(w_end)
    si = lax.broadcasted_iota(jnp.int32, (L, L), 0)
    ti = lax.broadcasted_iota(jnp.int32, (L, L), 1)
    masks = (si <= ti, si >= ti)
    n_row = lax.broadcasted_iota(jnp.int32, (BF16_ROWS, 1), 0) == 0
    refs = ((kf_ref, qtf_ref, vtf_ref, hf_ref), (kb_ref, qtb_ref, vtb_ref, hb_ref))
    pairs = [(d, j) for d in range(2) for j in range(H)]

    def k_of(d, j):
        return refs[d][0][:, j * ML_QK_DIM:(j + 1) * ML_QK_DIM]

    def qt_of(d, j):
        return refs[d][1][j * ML_QK_DIM:(j + 1) * ML_QK_DIM, :]

    scores = []
    carried = []
    for d, j in pairs:
        p = d * H + j
        scores.append(jnp.dot(k_of(d, j), qt_of(d, j), preferred_element_type=F32))
        qtw = (qt_of(d, j).astype(F32) * w_inter[p:p + 1, :]).astype(BF16)
        carried.append(jnp.dot(c_sc[p].astype(BF16), qtw, preferred_element_type=F32))
    dens = []
    n_locs = []
    rhss = []
    for d, j in pairs:
        p = d * H + j
        arg = jnp.broadcast_to(r_cols[:, p:p + 1], (L, L)) + bm[p:p + 1, :]
        sc_t = scores[p] * jnp.exp(jnp.where(masks[d], arg, -jnp.inf))
        dens.append(jnp.sum(sc_t, axis=0, keepdims=True))
        kw = k_of(d, j).astype(F32) * jnp.broadcast_to(w_end_cols[:, p:p + 1], (L, ML_QK_DIM))
        n_locs.append(jnp.sum(kw, axis=0, keepdims=True))
        rhss.append(jnp.concatenate([sc_t.astype(BF16), kw.astype(BF16)], axis=1))
    boths = []
    for d, j in pairs:
        p = d * H + j
        vt_j = refs[d][2][j * ML_V_DIM:(j + 1) * ML_V_DIM, :]
        boths.append(jnp.dot(vt_j, rhss[p], preferred_element_type=F32))
    for d, j in pairs:
        p = d * H + j
        num = boths[p][:, :L] + carried[p][:ML_V_DIM]
        den = dens[p] + carried[p][ML_V_DIM:ML_V_DIM + 1]
        refs[d][3][j * ML_V_DIM:(j + 1) * ML_V_DIM, :] = num / jnp.maximum(jnp.abs(den), clamp[p:p + 1, :])
        c_prev = c_sc[p]
        sp = s_prev[p:p + 1, :]
        sl = s_loc[p:p + 1, :]
        c_sc[p, 0:ML_V_DIM, :] = sp * c_prev[:ML_V_DIM] + sl * boths[p][:, L:]
        c_sc[p, ML_V_DIM:, :] = sp * c_prev[ML_V_DIM:] + sl * jnp.where(n_row, n_locs[p], 0.0)


def _mlstm(k_conv, qt_conv, projt, gates_t, batch, seq):
    t = k_conv.shape[0]
    L = ML_CHUNK
    nc = seq // L
    fwd = lambda b, c: b * nc + c
    bwd = lambda b, c: b * nc + nc - 1 - c
    nxt = lambda c: jnp.minimum(c + 1, nc - 1)

    def dir_specs(ch):
        return [
            pl.BlockSpec((L, ML_QK_COLS), lambda b, c: (ch(b, c), 0)),
            pl.BlockSpec((ML_QK_COLS, L), lambda b, c: (0, ch(b, c))),
            pl.BlockSpec((ML_V_COLS, L), lambda b, c: (R_MV // ML_V_COLS, ch(b, c))),
            pl.BlockSpec((GATE_COLS, L), lambda b, c: (0, ch(b, c))),
            pl.BlockSpec((GATE_COLS, L), lambda b, c: (0, ch(b, nxt(c)))),
        ]

    return pl.pallas_call(
        _mlstm_kernel,
        grid=(batch, nc),
        in_specs=dir_specs(fwd) + dir_specs(bwd),
        out_specs=[pl.BlockSpec((ML_V_COLS, L), lambda b, c: (0, fwd(b, c))),
                   pl.BlockSpec((ML_V_COLS, L), lambda b, c: (0, bwd(b, c)))],
        out_shape=[jax.ShapeDtypeStruct((ML_V_COLS, t), F32),
                   jax.ShapeDtypeStruct((ML_V_COLS, t), F32)],
        scratch_shapes=[pltpu.VMEM((ML_PAIRS, ML_STATE_ROWS, ML_QK_DIM), F32),
                        pltpu.VMEM((ML_PAIRS, L), F32),
                        pltpu.VMEM((2, ML_ROW_KINDS * ML_PAIRS, L), F32)],
        compiler_params=pltpu.CompilerParams(
            dimension_semantics=("parallel", "arbitrary"), vmem_limit_bytes=VMEM_LIMIT),
        name="mlstm",
    )(k_conv, qt_conv, projt, gates_t, gates_t, k_conv, qt_conv, projt, gates_t, gates_t)


def _mix_kernel(att_ref, hft_ref, hbt_ref, mot_ref, x_ref, w_ref, gml_ref, gpost_ref, wdown_ref,
                o_ref, wdown_bf_ref, cat_ref):
    wdown_bf_ref[...] = wdown_ref[...].astype(BF16)
    cat_ref[:, 0:ATT_Q_COLS] = att_ref[...]
    for j in range(ML_HEADS):
        sl = slice(j * ML_V_DIM, (j + 1) * ML_V_DIM)
        h = _rms(hft_ref[sl, :] + hbt_ref[sl, :], axis=0)
        gain = jnp.concatenate([gml_ref[sl, :]] * (MIX_TM // LANES), axis=1)
        mem = h * gain * _sigmoid(mot_ref[sl, :].astype(F32))
        cat_ref[:, ATT_Q_COLS + j * ML_V_DIM:ATT_Q_COLS + (j + 1) * ML_V_DIM] = mem.T.astype(BF16)
    mix = jnp.dot(cat_ref[...], w_ref[...], preferred_element_type=F32)
    o_ref[...] = x_ref[...] + _rms(mix) * gpost_ref[...]


def _mix(att, hft, hbt, projt, x2d, w_out_bf, g_ml_b, g_post, w_down_all, layer):
    t = x2d.shape[0]
    steps = t // MIX_TM
    down_rows = D_FF // steps
    row = lambda i: (i, 0)
    col = lambda i: (0, i)
    const = lambda i: (0, 0)
    return pl.pallas_call(
        _mix_kernel,
        grid=(steps,),
        in_specs=[
            pl.BlockSpec((MIX_TM, ATT_Q_COLS), row),
            pl.BlockSpec((ML_V_COLS, MIX_TM), col),
            pl.BlockSpec((ML_V_COLS, MIX_TM), col),
            pl.BlockSpec((ML_V_COLS, MIX_TM), lambda i: (R_MO // ML_V_COLS, i)),
            pl.BlockSpec((MIX_TM, D_MODEL), row),
            pl.BlockSpec((D_MODEL, D_MODEL), const, pipeline_mode=pl.Buffered(1)),
            pl.BlockSpec((ML_V_COLS, LANES), const),
            pl.BlockSpec((1, D_MODEL), const),
            pl.BlockSpec((None, down_rows, D_MODEL), lambda i: (layer, i, 0)),
        ],
        out_specs=[pl.BlockSpec((MIX_TM, D_MODEL), row),
                   pl.BlockSpec((down_rows, D_MODEL), row)],
        out_shape=[jax.ShapeDtypeStruct((t, D_MODEL), F32),
                   jax.ShapeDtypeStruct((D_FF, D_MODEL), BF16)],
        scratch_shapes=[pltpu.VMEM((MIX_TM, D_MODEL), BF16)],
        compiler_params=pltpu.CompilerParams(
            dimension_semantics=("parallel",), vmem_limit_bytes=VMEM_LIMIT),
        name="mix",
    )(att, hft, hbt, projt, x2d, w_out_bf, g_ml_b, g_post, w_down_all)


def _pin_before_next_load(operand_ref, value):
    bits = pltpu.bitcast(value, jnp.uint32)
    tiles = [bits[r:r + SUBLANES, c:c + LANES]
             for r in range(0, value.shape[0], SUBLANES) for c in range(0, value.shape[1], LANES)]
    folded = functools.reduce(jnp.bitwise_or, tiles)
    half = jnp.uint32(16)
    zero = pltpu.bitcast(lax.shift_right_logical(lax.shift_right_logical(folded, half), half), F32)
    zero = jnp.concatenate([zero] * (BF16_ROWS // SUBLANES), axis=0).astype(BF16)
    operand_ref[0:BF16_ROWS, 0:LANES] = operand_ref[0:BF16_ROWS, 0:LANES] + zero


def _mlp_kernel(xn_ref, xp_ref, gpre_ref, wup_ref, wdown_ref, gpost_ref, o_ref,
                h_a, h_b, acc_a, acc_b, *, n_blocks):
    blk = pl.program_id(0) - 1
    j = pl.program_id(1)
    rows = pl.ds(pl.multiple_of(j * MLP_SUB, MLP_SUB), MLP_SUB)

    def pre_norm(h_next):
        h = _rms(xn_ref[...]) * gpre_ref[...]
        h_next[rows, :] = h.astype(BF16)
        return h

    def finish(acc_prev):
        y = xp_ref[...] + _rms(acc_prev[rows, :]) * gpost_ref[...]
        o_ref[...] = y
        return y

    def main(h_cur, h_next, acc_cur, acc_prev):
        n_chunks = MLP_TF // MLP_UP_CHUNK
        us = []
        for c in range(n_chunks):
            cols = slice(c * MLP_UP_CHUNK, (c + 1) * MLP_UP_CHUNK)
            u = jnp.dot(h_cur[...], wup_ref[:, cols], preferred_element_type=F32)
            us.append(jnp.square(jnp.maximum(u, 0.0)).astype(BF16))
            if c == 0:
                _pin_before_next_load(h_cur, finish(acc_prev))
            elif c == 1:
                _pin_before_next_load(h_cur, pre_norm(h_next))
        part = jnp.dot(jnp.concatenate(us, axis=1), wdown_ref[...], preferred_element_type=F32)
        acc_cur[...] = jnp.where(j > 0, acc_cur[...], 0.0) + part

    @pl.when(blk < 0)
    def _():
        pre_norm(h_a)
        acc_a[rows, :] = jnp.zeros((MLP_SUB, D_MODEL), F32)
        acc_b[rows, :] = jnp.zeros((MLP_SUB, D_MODEL), F32)

    in_range = (blk >= 0) & (blk < n_blocks)
    even = (blk % 2) == 0

    @pl.when(in_range & even)
    def _():
        main(h_a, h_b, acc_a, acc_b)

    @pl.when(in_range & jnp.logical_not(even))
    def _():
        main(h_b, h_a, acc_b, acc_a)

    @pl.when(blk == n_blocks)
    def _():
        finish(acc_a if (n_blocks - 1) % 2 == 0 else acc_b)


def _mlp(x2d, g_pre, w_up, w_down, g_post):
    t = x2d.shape[0]
    ni = t // MLP_TM
    nj = D_FF // MLP_TF
    clamp = lambda blk: jnp.clip(blk, 0, ni - 1)
    sub_row = lambda blk, j: (clamp(blk) * nj + j, 0)
    wj = lambda i, j: jnp.where(i < 1, 0, jnp.where(i > ni, nj - 1, j))
    return pl.pallas_call(
        functools.partial(_mlp_kernel, n_blocks=ni),
        grid=(ni + 2, nj),
        in_specs=[
            pl.BlockSpec((MLP_SUB, D_MODEL), lambda i, j: sub_row(i, j)),
            pl.BlockSpec((MLP_SUB, D_MODEL), lambda i, j: sub_row(i - 2, j)),
            pl.BlockSpec((1, D_MODEL), lambda i, j: (0, 0)),
            pl.BlockSpec((D_MODEL, MLP_TF), lambda i, j: (0, wj(i, j))),
            pl.BlockSpec((MLP_TF, D_MODEL), lambda i, j: (wj(i, j), 0)),
            pl.BlockSpec((1, D_MODEL), lambda i, j: (0, 0)),
        ],
        out_specs=pl.BlockSpec((MLP_SUB, D_MODEL), lambda i, j: (jnp.where(i < 2, 0, (i - 2) * nj + j), 0)),
        out_shape=jax.ShapeDtypeStruct((t, D_MODEL), F32),
        scratch_shapes=[pltpu.VMEM((MLP_TM, D_MODEL), BF16), pltpu.VMEM((MLP_TM, D_MODEL), BF16),
                        pltpu.VMEM((MLP_TM, D_MODEL), F32), pltpu.VMEM((MLP_TM, D_MODEL), F32)],
        compiler_params=pltpu.CompilerParams(
            dimension_semantics=("arbitrary", "arbitrary"), vmem_limit_bytes=BIG_VMEM_LIMIT),
        name="mlp",
    )(x2d, x2d, g_pre, w_up, w_down, g_post)


def _rope_tables(seq):
    half = HEAD_DIM // 2
    inv_freq = ROPE_THETA ** (-np.arange(half, dtype=np.float64) / half)
    ang = np.arange(seq, dtype=np.float64)[:, None] * inv_freq[None, :]
    cos = np.cos(ang)
    sin = np.sin(ang)
    cos_t = np.concatenate([cos, cos], axis=1).astype(np.float32)
    sin_t = np.concatenate([-sin, sin], axis=1).astype(np.float32)
    return jnp.asarray(cos_t), jnp.asarray(sin_t)


def kernel(x, w_in, conv_w, gate_bias, ml_norm_g, attn_sink, w_out, g_pre_mix, g_post_mix,
           g_pre_mlp, g_post_mlp, w_up, w_down):
    batch, seq, d = x.shape
    depth = w_in.shape[0]
    cos_t, sin_t = _rope_tables(seq)
    x2d = x.reshape(batch * seq, d)
    w_in_bf = jnp.pad(w_in, ((0, 0), (0, 0), (0, IN_COLS_PAD - IN_COLS))).astype(BF16)
    for l in range(depth):
        proj, projt, gates_t, w_up_bf, w_out_bf = _in_proj(
            x2d, g_pre_mix[l][None, :], w_in_bf, gate_bias[l][:, None], cos_t, sin_t,
            w_up, w_out, l, seq)
        k_conv, qt_conv = _qk_conv(proj, conv_w[l], seq)
        att = _attention(proj, attn_sink[l], batch, seq)
        hft, hbt = _mlstm(k_conv, qt_conv, projt, gates_t, batch, seq)
        g_ml_b = jnp.broadcast_to(ml_norm_g[l][:, None], (ML_V_COLS, LANES))
        x2d, w_down_bf = _mix(att, hft, hbt, projt, x2d, w_out_bf, g_ml_b, g_post_mix[l][None, :],
                              w_down, l)
        x2d = _mlp(x2d, g_pre_mlp[l][None, :], w_up_bf, w_down_bf, g_post_mlp[l][None, :])
    return x2d.reshape(batch, seq, d)
```

```python
import functools

import jax
import jax.numpy as jnp
import numpy as np
from jax import lax
from jax.experimental import pallas as pl
from jax.experimental.pallas import tpu as pltpu

F32 = jnp.float32
BF16 = jnp.bfloat16

D_MODEL = 2048
ATT_HEADS = 8
ATT_KV_HEADS = 2
ATT_GROUP = ATT_HEADS // ATT_KV_HEADS
HEAD_DIM = 128
ATT_WINDOW = 128
ROPE_THETA = 10000.0
ML_HEADS = 4
ML_V_DIM = 256
ML_QK_DIM = 128
ML_CHUNK = 128
M_INIT = -1e30
ATT_MASKED = -1e30
LOG2_E = 1.4426950408889634
D_FF = 4 * D_MODEL
NORM_EPS = 1e-6
GATE_COLS = 4 * ML_HEADS

ATT_Q_COLS = ATT_HEADS * HEAD_DIM
ATT_KV_COLS = ATT_KV_HEADS * HEAD_DIM
ML_QK_COLS = ML_HEADS * ML_QK_DIM
ML_V_COLS = ML_HEADS * ML_V_DIM

S_AK = ATT_Q_COLS
S_AV = S_AK + ATT_KV_COLS
S_MQ = S_AV + ATT_KV_COLS
S_MV = S_MQ + 2 * ML_QK_COLS
S_GATE = S_MV + 2 * ML_V_COLS
IN_COLS = S_GATE + GATE_COLS
IN_COLS_PAD = S_GATE + 128
C_AQ = 0
C_MQK = C_AQ + ATT_Q_COLS
C_AK = C_MQK + 2 * ML_QK_COLS
C_AV = C_AK + ATT_KV_COLS
PROJ_COLS = C_AV + ATT_KV_COLS
R_MV = 0
R_MO = R_MV + ML_V_COLS
PROJT_ROWS = R_MO + ML_V_COLS
LANES = 128
SUBLANES = 8
BF16_ROWS = 16

VMEM_LIMIT = 56 * 1024 * 1024
BIG_VMEM_LIMIT = 60 * 1024 * 1024

W_CAST_ROWS = 256
IN_TM = 512
IN_TN = 512
CONV_TM = 1024
CONV_SUB = 256
ATT_TQ = 1024
MIX_TM = 512
MLP_TM = 512
MLP_TF = 2048
MLP_UP_CHUNK = 512
MLP_SUB = MLP_TM // (D_FF // MLP_TF)
ML_STATE_ROWS = ML_V_DIM + BF16_ROWS
ML_PAIRS = 2 * ML_HEADS
ML_ROW_KINDS = 7
ML_SUB = 2

_NT = (((1,), (1,)), ((), ()))
_TN = (((0,), (0,)), ((), ()))


def _sigmoid(x):
    return 1.0 / (1.0 + jnp.exp(-x))


def _log_sigmoid(x):
    return jnp.minimum(x, 0.0) - jnp.log(1.0 + jnp.exp(-jnp.abs(x)))


def _rms(x, axis=-1):
    return x * lax.rsqrt(jnp.mean(x * x, axis=axis, keepdims=True) + NORM_EPS)


_ROW_MAJOR_CHUNKS = ((0, C_AQ), (IN_TN, C_AQ + IN_TN), (S_AK, C_AK), (S_MQ, C_MQK),
                     (S_MQ + IN_TN, C_MQK + IN_TN))


def _in_proj_kernel(x_ref, g_ref, w_ref, gbias_ref, cos_ref, sin_ref, wup_ref, wout_ref,
                    proj_ref, projt_ref, gatet_ref, wup_bf_ref, wout_bf_ref, h_ref):
    wup_bf_ref[...] = wup_ref[...].astype(BF16)
    wout_bf_ref[...] = wout_ref[...].astype(BF16)
    h_ref[...] = (_rms(x_ref[...]) * g_ref[...]).astype(BF16)
    cos = cos_ref[...]
    sin = sin_ref[...]

    def rope(a):
        return a * cos + pltpu.roll(a, HEAD_DIM // 2, 1) * sin

    q_scale = HEAD_DIM ** -0.5 * LOG2_E
    for src, dst in _ROW_MAJOR_CHUNKS:
        acc = jnp.dot(h_ref[...], w_ref[:, src:src + IN_TN], preferred_element_type=F32)
        for k in range(IN_TN // HEAD_DIM):
            col = src + k * HEAD_DIM
            a = acc[:, k * HEAD_DIM:(k + 1) * HEAD_DIM]
            if col < S_AK:
                a = rope(a) * q_scale
            elif col < S_AV:
                a = rope(a)
            proj_ref[:, dst + k * HEAD_DIM:dst + (k + 1) * HEAD_DIM] = a.astype(BF16)
    for c in range(PROJT_ROWS // IN_TN):
        acc = jnp.dot(h_ref[...], w_ref[:, S_MV + c * IN_TN:S_MV + (c + 1) * IN_TN],
                      preferred_element_type=F32)
        projt_ref[c * IN_TN:(c + 1) * IN_TN, :] = acc.T.astype(BF16)
    gate = jnp.dot(h_ref[...], w_ref[:, S_GATE:IN_COLS_PAD], preferred_element_type=F32)
    gatet_ref[...] = gate.T[:GATE_COLS] + gbias_ref[...]


def _in_proj(x2d, g, w_bf_all, gate_bias_col, cos_t, sin_t, w_up_all, w_out_all, layer, seq):
    t = x2d.shape[0]
    steps = t // IN_TM
    pos_blocks = seq // IN_TM
    up_rows = D_MODEL // steps
    const = lambda i: (0, 0)
    row = lambda i: (i, 0)
    layer_row = lambda i: (layer, i, 0)
    return pl.pallas_call(
        _in_proj_kernel,
        grid=(steps,),
        in_specs=[
            pl.BlockSpec((IN_TM, D_MODEL), row),
            pl.BlockSpec((1, D_MODEL), const),
            pl.BlockSpec((None, D_MODEL, IN_COLS_PAD), lambda i: (layer, 0, 0), pipeline_mode=pl.Buffered(1)),
            pl.BlockSpec((GATE_COLS, 1), const),
            pl.BlockSpec((IN_TM, HEAD_DIM), lambda i: (i % pos_blocks, 0)),
            pl.BlockSpec((IN_TM, HEAD_DIM), lambda i: (i % pos_blocks, 0)),
            pl.BlockSpec((None, up_rows, D_FF), layer_row),
            pl.BlockSpec((None, up_rows, D_MODEL), layer_row),
        ],
        out_specs=[
            pl.BlockSpec((IN_TM, PROJ_COLS), row),
            pl.BlockSpec((PROJT_ROWS, IN_TM), lambda i: (0, i)),
            pl.BlockSpec((GATE_COLS, IN_TM), lambda i: (0, i)),
            pl.BlockSpec((up_rows, D_FF), row),
            pl.BlockSpec((up_rows, D_MODEL), row),
        ],
        out_shape=[
            jax.ShapeDtypeStruct((t, PROJ_COLS), BF16),
            jax.ShapeDtypeStruct((PROJT_ROWS, t), BF16),
            jax.ShapeDtypeStruct((GATE_COLS, t), F32),
            jax.ShapeDtypeStruct((D_MODEL, D_FF), BF16),
            jax.ShapeDtypeStruct((D_MODEL, D_MODEL), BF16),
        ],
        scratch_shapes=[pltpu.VMEM((IN_TM, D_MODEL), BF16)],
        compiler_params=pltpu.CompilerParams(
            dimension_semantics=("parallel",), vmem_limit_bytes=VMEM_LIMIT),
        name="in_proj",
    )(x2d, g, w_bf_all, gate_bias_col, cos_t, sin_t, w_up_all, w_out_all)


def _qk_conv_kernel(x_ref, xp_ref, xn_ref, w_ref, k_ref, qt_ref, *, seq_blocks):
    n = CONV_SUB
    pos = pl.program_id(0) % seq_blocks
    has_prev = (pos > 0).astype(F32)
    has_next = (pos < seq_blocks - 1).astype(F32)
    ri = lax.broadcasted_iota(jnp.int32, (n, n), 0)
    ci = lax.broadcasted_iota(jnp.int32, (n, n), 1)
    shift_prev = (ci == ri - 1).astype(BF16)
    shift_next = (ci == ri + 1).astype(BF16)
    rowid = lax.broadcasted_iota(jnp.int32, (n, 1), 0)
    w0 = w_ref[0:1, :]
    w1 = w_ref[1:2, :]
    w2 = w_ref[2:3, :]
    n_sub = CONV_TM // n
    for sb in range(n_sub):
        xs = x_ref[sb * n:(sb + 1) * n, :]
        x_prev = jnp.dot(shift_prev, xs, preferred_element_type=F32)
        x_next = jnp.dot(shift_next, xs, preferred_element_type=F32)
        if sb == 0:
            prev_row = xp_ref[BF16_ROWS - 1:BF16_ROWS, :].astype(F32) * has_prev
        else:
            prev_row = x_ref[sb * n - BF16_ROWS:sb * n, :].astype(F32)[BF16_ROWS - 1:BF16_ROWS]
        if sb == n_sub - 1:
            next_row = xn_ref[0:1, :].astype(F32) * has_next
        else:
            next_row = x_ref[(sb + 1) * n:(sb + 1) * n + BF16_ROWS, :].astype(F32)[0:1]
        x_prev = jnp.where(rowid == 0, prev_row, x_prev)
        x_next = jnp.where(rowid == n - 1, next_row, x_next)
        y = x_prev * w0 + xs.astype(F32) * w1 + x_next * w2
        y = y * _sigmoid(y)
        k_ref[sb * n:(sb + 1) * n, :] = y[:, ML_QK_COLS:].astype(BF16)
        q = y[:, :ML_QK_COLS] * (ML_QK_DIM ** -0.5)
        qt_ref[:, sb * n:(sb + 1) * n] = q.T.astype(BF16)


def _qk_conv(proj, conv_w, seq):
    t = proj.shape[0]
    width = 2 * ML_QK_COLS
    col = C_MQK // width
    per = CONV_TM // BF16_ROWS
    last = t // BF16_ROWS - 1
    return pl.pallas_call(
        functools.partial(_qk_conv_kernel, seq_blocks=seq // CONV_TM),
        grid=(t // CONV_TM,),
        in_specs=[
            pl.BlockSpec((CONV_TM, width), lambda i: (i, col)),
            pl.BlockSpec((BF16_ROWS, width), lambda i: (jnp.maximum(i * per - 1, 0), col)),
            pl.BlockSpec((BF16_ROWS, width), lambda i: (jnp.minimum((i + 1) * per, last), col)),
            pl.BlockSpec((3, width), lambda i: (0, 0)),
        ],
        out_specs=[pl.BlockSpec((CONV_TM, ML_QK_COLS), lambda i: (i, 0)),
                   pl.BlockSpec((ML_QK_COLS, CONV_TM), lambda i: (0, i))],
        out_shape=[jax.ShapeDtypeStruct((t, ML_QK_COLS), BF16),
                   jax.ShapeDtypeStruct((ML_QK_COLS, t), BF16)],
        compiler_params=pltpu.CompilerParams(
            dimension_semantics=("parallel",), vmem_limit_bytes=VMEM_LIMIT),
        name="qk_conv",
    )(proj, proj, proj, conv_w)


def _attn_kernel(sink_ref, q_ref, kc_ref, kp_ref, kn_ref, vc_ref, vp_ref, vn_ref,
                 o_ref, kbuf, vbuf, *, n_blocks):
    w = ATT_WINDOW
    kbuf[0:w] = kp_ref[...]
    kbuf[w:w + ATT_TQ] = kc_ref[...]
    kbuf[w + ATT_TQ:] = kn_ref[...]
    for h in range(ATT_KV_HEADS):
        hs = slice(h * HEAD_DIM, (h + 1) * HEAD_DIM)
        vbuf[h, 0:w, 0:HEAD_DIM] = vp_ref[:, hs]
        vbuf[h, w:w + ATT_TQ, 0:HEAD_DIM] = vc_ref[:, hs]
        vbuf[h, w + ATT_TQ:, 0:HEAD_DIM] = vn_ref[:, hs]
        vbuf[h, :, HEAD_DIM:] = jnp.ones((ATT_TQ + 2 * w, HEAD_DIM), BF16)
    i = pl.program_id(1)
    rows = ATT_GROUP * w
    qi = lax.broadcasted_iota(jnp.int32, (rows, 1), 0) & (w - 1)
    blk = lax.broadcasted_iota(jnp.int32, (rows, 1), 0) // w
    kj = lax.broadcasted_iota(jnp.int32, (rows, 3 * w), 1)
    kj_row = lax.broadcasted_iota(jnp.int32, (1, 3 * w), 1)
    band = jnp.where((kj >= qi) & (kj <= qi + 2 * w), 0.0, ATT_MASKED)
    n_win = ATT_TQ // w
    sinks = []
    for h in range(ATT_KV_HEADS):
        sink = jnp.full((rows, 1), sink_ref[h * ATT_GROUP], F32)
        for g in range(1, ATT_GROUP):
            sink = jnp.where(blk == g, sink_ref[h * ATT_GROUP + g], sink)
        sinks.append(sink * LOG2_E)
    tiles = [(n, h) for n in range(n_win) for h in range(ATT_KV_HEADS)]
    scores = []
    for n, h in tiles:
        qs = jnp.concatenate(
            [q_ref[n * w:(n + 1) * w, (h * ATT_GROUP + g) * HEAD_DIM:(h * ATT_GROUP + g + 1) * HEAD_DIM]
             for g in range(ATT_GROUP)], axis=0)
        kw = kbuf[n * w:(n + 3) * w, h * HEAD_DIM:(h + 1) * HEAD_DIM]
        scores.append(lax.dot_general(qs, kw, _NT, preferred_element_type=F32))
    probs = []
    maxes = []
    for idx, (n, h) in enumerate(tiles):
        bias = band
        if n == 0:
            bias = bias + jnp.where(kj_row < w, jnp.where(i == 0, ATT_MASKED, 0.0), 0.0)
        if n == n_win - 1:
            bias = bias + jnp.where(kj_row >= 2 * w, jnp.where(i == n_blocks - 1, ATT_MASKED, 0.0), 0.0)
        s = scores[idx] + bias
        m = jnp.maximum(jnp.max(s, axis=-1, keepdims=True), sinks[h])
        maxes.append(m)
        probs.append(jnp.exp2(s - m).astype(BF16))
    for idx, (n, h) in enumerate(tiles):
        o_aug = jnp.dot(probs[idx], vbuf[h, n * w:(n + 3) * w, :], preferred_element_type=F32)
        denom = o_aug[:, HEAD_DIM:] + jnp.exp2(sinks[h] - maxes[idx])
        o = o_aug[:, :HEAD_DIM] * (1.0 / denom)
        for g in range(ATT_GROUP):
            col = (h * ATT_GROUP + g) * HEAD_DIM
            o_ref[n * w:(n + 1) * w, col:col + HEAD_DIM] = o[g * w:(g + 1) * w].astype(BF16)


def _attention(proj, sink, batch, seq):
    t = proj.shape[0]
    w = ATT_WINDOW
    nq = seq // ATT_TQ
    per = ATT_TQ // w
    last_blk = t // w - 1
    cur = lambda b, i: b * nq + i
    prev = lambda b, i: jnp.maximum((b * nq + i) * per - 1, 0)
    nxt = lambda b, i: jnp.minimum((b * nq + i + 1) * per, last_blk)
    kcol = C_AK // ATT_KV_COLS
    vcol = C_AV // ATT_KV_COLS
    return pl.pallas_call(
        functools.partial(_attn_kernel, n_blocks=nq),
        grid=(batch, nq),
        in_specs=[
            pl.BlockSpec(memory_space=pltpu.SMEM),
            pl.BlockSpec((ATT_TQ, ATT_Q_COLS), lambda b, i: (cur(b, i), C_AQ // ATT_Q_COLS)),
            pl.BlockSpec((ATT_TQ, ATT_KV_COLS), lambda b, i: (cur(b, i), kcol)),
            pl.BlockSpec((w, ATT_KV_COLS), lambda b, i: (prev(b, i), kcol)),
            pl.BlockSpec((w, ATT_KV_COLS), lambda b, i: (nxt(b, i), kcol)),
            pl.BlockSpec((ATT_TQ, ATT_KV_COLS), lambda b, i: (cur(b, i), vcol)),
            pl.BlockSpec((w, ATT_KV_COLS), lambda b, i: (prev(b, i), vcol)),
            pl.BlockSpec((w, ATT_KV_COLS), lambda b, i: (nxt(b, i), vcol)),
        ],
        out_specs=pl.BlockSpec((ATT_TQ, ATT_Q_COLS), lambda b, i: (cur(b, i), 0)),
        out_shape=jax.ShapeDtypeStruct((t, ATT_Q_COLS), BF16),
        scratch_shapes=[pltpu.VMEM((ATT_TQ + 2 * w, ATT_KV_COLS), BF16),
                        pltpu.VMEM((ATT_KV_HEADS, ATT_TQ + 2 * w, 2 * HEAD_DIM), BF16)],
        compiler_params=pltpu.CompilerParams(
            dimension_semantics=("parallel", "parallel"), vmem_limit_bytes=VMEM_LIMIT),
        name="attn",
    )(sink, proj, proj, proj, proj, proj, proj, proj)


def _rows_to_cols(x):
    length = x.shape[1]
    padded = jnp.concatenate([x, jnp.zeros((length - x.shape[0], length), x.dtype)], axis=0)
    return padded.T


def _ml_gate_rows(gates_f, gates_b, m_prev):
    L = ML_CHUNK
    H = ML_HEADS
    row8 = lax.broadcasted_iota(jnp.int32, (2 * H, L), 0)
    lane = lax.broadcasted_iota(jnp.int32, (2 * H, L), 1)
    is_fwd = row8 < H
    fwd_if = gates_f[0:2 * H]
    bwd_if = gates_b[2 * H:4 * H]
    gi = jnp.where(is_fwd, fwd_if, pltpu.roll(bwd_if, H, 0))
    gf = jnp.where(is_fwd, pltpu.roll(fwd_if, H, 0), bwd_if)
    ls = _log_sigmoid(gf)
    hi = ls.astype(BF16).astype(F32)
    rem = ls - hi
    mid = rem.astype(BF16).astype(F32)
    lo = rem - mid
    parts = jnp.concatenate([hi, mid, lo, jnp.zeros_like(hi)], axis=0).astype(BF16)
    si = lax.broadcasted_iota(jnp.int32, (L, L), 0)
    ti = lax.broadcasted_iota(jnp.int32, (L, L), 1)
    pre = jnp.dot(parts, (si <= ti).astype(BF16), preferred_element_type=F32)
    prefix = pre[0:2 * H] + pre[2 * H:4 * H] + pre[4 * H:6 * H]
    gtot = jnp.sum(ls, axis=1, keepdims=True)
    b = jnp.where(is_fwd, prefix, gtot - prefix + ls)
    r = gi - b
    cm = r
    sh = 1
    while sh < L:
        from_left = jnp.where(lane >= sh, pltpu.roll(cm, sh, 1), -jnp.inf)
        from_right = jnp.where(lane < L - sh, pltpu.roll(cm, L - sh, 1), -jnp.inf)
        cm = jnp.maximum(cm, jnp.where(is_fwd, from_left, from_right))
        sh *= 2
    top = jnp.maximum(m_prev, cm)
    a = gtot - b + gi
    m_loc = jnp.max(a, axis=1, keepdims=True)
    m_new = jnp.maximum(gtot + m_prev, m_loc)
    rows = jnp.concatenate([
        -top,
        jnp.exp(m_prev - top),
        jnp.exp(-(b + top)),
        r,
        jnp.exp(a - m_loc),
        jnp.exp(gtot + m_prev - m_new),
        jnp.exp(m_loc - m_new) + jnp.zeros_like(r),
    ], axis=0)
    return rows, m_new


def _mlstm_kernel(kf_ref, qtf_ref, vtf_ref, gtf_ref, gtfn_ref, kb_ref, qtb_ref, vtb_ref, gtb_ref,
                  gtbn_ref, hf_ref, hb_ref, c_sc, m_sc, rows_sc):
    L = ML_CHUNK
    H = ML_HEADS
    P = ML_PAIRS
    c = pl.program_id(1)
    slot = c % 2

    def tok(d, u):
        lo = u * L if d == 0 else (ML_SUB - 1 - u) * L
        return slice(lo, lo + L)

    def gate_rows(gf_ref, gb_ref, u, m_prev):
        return _ml_gate_rows(gf_ref[:, tok(0, u)], gb_ref[:, tok(1, u)], m_prev)

    @pl.when(c == 0)
    def _():
        c_sc[...] = jnp.zeros_like(c_sc)
        m = jnp.full((P, L), M_INIT, F32)
        for u in range(ML_SUB):
            rows0, m = gate_rows(gtf_ref, gtb_ref, u, m)
            rows_sc[0, u] = rows0
        m_sc[...] = m

    rows_now = [rows_sc[slot, u] for u in range(ML_SUB)]
    m = m_sc[...]
    for u in range(ML_SUB):
        rows_next, m = gate_rows(gtfn_ref, gtbn_ref, u, m)
        rows_sc[1 - slot, u] = rows_next
    m_sc[...] = m

    si = lax.broadcasted_iota(jnp.int32, (L, L), 0)
    ti = lax.broadcasted_iota(jnp.int32, (L, L), 1)
    masks = (si <= ti, si >= ti)
    n_row = lax.broadcasted_iota(jnp.int32, (BF16_ROWS, 1), 0) == 0
    refs = ((kf_ref, qtf_ref, vtf_ref, hf_ref), (kb_ref, qtb_ref, vtb_ref, hb_ref))
    pairs = [(d, j) for d in range(2) for j in range(H)]

    for u in range(ML_SUB):
        rows = rows_now[u]
        bm, w_inter, clamp, r, w_end, s_prev, s_loc = [rows[i * P:(i + 1) * P] for i in range(ML_ROW_KINDS)]
        r_cols = _rows_to_cols(r)
        w_end_cols = _rows_to_cols(w_end)

        def k_of(d, j):
            return refs[d][0][tok(d, u), j * ML_QK_DIM:(j + 1) * ML_QK_DIM]

        def qt_of(d, j):
            return refs[d][1][j * ML_QK_DIM:(j + 1) * ML_QK_DIM, tok(d, u)]

        scores = []
        carried = []
        for d, j in pairs:
            p = d * H + j
            scores.append(jnp.dot(k_of(d, j), qt_of(d, j), preferred_element_type=F32))
            qtw = (qt_of(d, j).astype(F32) * w_inter[p:p + 1, :]).astype(BF16)
            carried.append(jnp.dot(c_sc[p].astype(BF16), qtw, preferred_element_type=F32))
        dens = []
        n_locs = []
        rhss = []
        for d, j in pairs:
            p = d * H + j
            arg = jnp.broadcast_to(r_cols[:, p:p + 1], (L, L)) + bm[p:p + 1, :]
            sc_t = scores[p] * jnp.exp(jnp.where(masks[d], arg, -jnp.inf))
            dens.append(jnp.sum(sc_t, axis=0, keepdims=True))
            kw = k_of(d, j).astype(F32) * jnp.broadcast_to(w_end_cols[:, p:p + 1], (L, ML_QK_DIM))
            n_locs.append(jnp.sum(kw, axis=0, keepdims=True))
            rhss.append(jnp.concatenate([sc_t.astype(BF16), kw.astype(BF16)], axis=1))
        boths = []
        for d, j in pairs:
            p = d * H + j
            vt_j = refs[d][2][j * ML_V_DIM:(j + 1) * ML_V_DIM, tok(d, u)]
            boths.append(jnp.dot(vt_j, rhss[p], preferred_element_type=F32))
        for d, j in pairs:
            p = d * H + j
            num = boths[p][:, :L] + carried[p][:ML_V_DIM]
            den = dens[p] + carried[p][ML_V_DIM:ML_V_DIM + 1]
            refs[d][3][j * ML_V_DIM:(j + 1) * ML_V_DIM, tok(d, u)] = (
                num / jnp.maximum(jnp.abs(den), clamp[p:p + 1, :]))
            c_prev = c_sc[p]
            sp = s_prev[p:p + 1, :]
            sl = s_loc[p:p + 1, :]
            c_sc[p, 0:ML_V_DIM, :] = sp * c_prev[:ML_V_DIM] + sl * boths[p][:, L:]
            c_sc[p, ML_V_DIM:, :] = sp * c_prev[ML_V_DIM:] + sl * jnp.where(n_row, n_locs[p], 0.0)


def _mlstm(k_conv, qt_conv, projt, gates_t, batch, seq):
    t = k_conv.shape[0]
    blk = ML_SUB * ML_CHUNK
    nc = seq // blk
    fwd = lambda b, c: b * nc + c
    bwd = lambda b, c: b * nc + nc - 1 - c
    nxt = lambda c: jnp.minimum(c + 1, nc - 1)

    def dir_specs(ch):
        return [
            pl.BlockSpec((blk, ML_QK_COLS), lambda b, c: (ch(b, c), 0)),
            pl.BlockSpec((ML_QK_COLS, blk), lambda b, c: (0, ch(b, c))),
            pl.BlockSpec((ML_V_COLS, blk), lambda b, c: (R_MV // ML_V_COLS, ch(b, c))),
            pl.BlockSpec((GATE_COLS, blk), lambda b, c: (0, ch(b, c))),
            pl.BlockSpec((GATE_COLS, blk), lambda b, c: (0, ch(b, nxt(c)))),
        ]

    return pl.pallas_call(
        _mlstm_kernel,
        grid=(batch, nc),
        in_specs=dir_specs(fwd) + dir_specs(bwd),
        out_specs=[pl.BlockSpec((ML_V_COLS, blk), lambda b, c: (0, fwd(b, c))),
                   pl.BlockSpec((ML_V_COLS, blk), lambda b, c: (0, bwd(b, c)))],
        out_shape=[jax.ShapeDtypeStruct((ML_V_COLS, t), F32),
                   jax.ShapeDtypeStruct((ML_V_COLS, t), F32)],
        scratch_shapes=[pltpu.VMEM((ML_PAIRS, ML_STATE_ROWS, ML_QK_DIM), F32),
                        pltpu.VMEM((ML_PAIRS, ML_CHUNK), F32),
                        pltpu.VMEM((2, ML_SUB, ML_ROW_KINDS * ML_PAIRS, ML_CHUNK), F32)],
        compiler_params=pltpu.CompilerParams(
            dimension_semantics=("parallel", "arbitrary"), vmem_limit_bytes=VMEM_LIMIT),
        name="mlstm",
    )(k_conv, qt_conv, projt, gates_t, gates_t, k_conv, qt_conv, projt, gates_t, gates_t)


def _mix_kernel(att_ref, hft_ref, hbt_ref, mot_ref, x_ref, w_ref, gml_ref, gpost_ref, wdown_ref,
                o_ref, wdown_bf_ref, cat_ref):
    wdown_bf_ref[...] = wdown_ref[...].astype(BF16)
    cat_ref[:, 0:ATT_Q_COLS] = att_ref[...]
    for j in range(ML_HEADS):
        sl = slice(j * ML_V_DIM, (j + 1) * ML_V_DIM)
        h = _rms(hft_ref[sl, :] + hbt_ref[sl, :], axis=0)
        gain = jnp.concatenate([gml_ref[sl, :]] * (MIX_TM // LANES), axis=1)
        mem = h * gain * _sigmoid(mot_ref[sl, :].astype(F32))
        cat_ref[:, ATT_Q_COLS + j * ML_V_DIM:ATT_Q_COLS + (j + 1) * ML_V_DIM] = mem.T.astype(BF16)
    mix = jnp.dot(cat_ref[...], w_ref[...], preferred_element_type=F32)
    o_ref[...] = x_ref[...] + _rms(mix) * gpost_ref[...]


def _mix(att, hft, hbt, projt, x2d, w_out_bf, g_ml_b, g_post, w_down_all, layer):
    t = x2d.shape[0]
    steps = t // MIX_TM
    down_rows = D_FF // steps
    row = lambda i: (i, 0)
    col = lambda i: (0, i)
    const = lambda i: (0, 0)
    return pl.pallas_call(
        _mix_kernel,
        grid=(steps,),
        in_specs=[
            pl.BlockSpec((MIX_TM, ATT_Q_COLS), row),
            pl.BlockSpec((ML_V_COLS, MIX_TM), col),
            pl.BlockSpec((ML_V_COLS, MIX_TM), col),
            pl.BlockSpec((ML_V_COLS, MIX_TM), lambda i: (R_MO // ML_V_COLS, i)),
            pl.BlockSpec((MIX_TM, D_MODEL), row),
            pl.BlockSpec((D_MODEL, D_MODEL), const, pipeline_mode=pl.Buffered(1)),
            pl.BlockSpec((ML_V_COLS, LANES), const),
            pl.BlockSpec((1, D_MODEL), const),
            pl.BlockSpec((None, down_rows, D_MODEL), lambda i: (layer, i, 0)),
        ],
        out_specs=[pl.BlockSpec((MIX_TM, D_MODEL), row),
                   pl.BlockSpec((down_rows, D_MODEL), row)],
        out_shape=[jax.ShapeDtypeStruct((t, D_MODEL), F32),
                   jax.ShapeDtypeStruct((D_FF, D_MODEL), BF16)],
        scratch_shapes=[pltpu.VMEM((MIX_TM, D_MODEL), BF16)],
        compiler_params=pltpu.CompilerParams(
            dimension_semantics=("parallel",), vmem_limit_bytes=VMEM_LIMIT),
        name="mix",
    )(att, hft, hbt, projt, x2d, w_out_bf, g_ml_b, g_post, w_down_all)


def _pin_before_next_load(operand_ref, value):
    bits = pltpu.bitcast(value, jnp.uint32)
    tiles = [bits[r:r + SUBLANES, c:c + LANES]
             for r in range(0, value.shape[0], SUBLANES) for c in range(0, value.shape[1], LANES)]
    folded = functools.reduce(jnp.bitwise_or, tiles)
    half = jnp.uint32(16)
    zero = pltpu.bitcast(lax.shift_right_logical(lax.shift_right_logical(folded, half), half), F32)
    zero = jnp.concatenate([zero] * (BF16_ROWS // SUBLANES), axis=0).astype(BF16)
    operand_ref[0:BF16_ROWS, 0:LANES] = operand_ref[0:BF16_ROWS, 0:LANES] + zero


def _mlp_kernel(xn_ref, xp_ref, gpre_ref, wup_ref, wdown_ref, gpost_ref, o_ref,
                h_a, h_b, acc_a, acc_b, *, n_blocks):
    blk = pl.program_id(0) - 1
    j = pl.program_id(1)
    rows = pl.ds(pl.multiple_of(j * MLP_SUB, MLP_SUB), MLP_SUB)

    def pre_norm(h_next):
        h = _rms(xn_ref[...]) * gpre_ref[...]
        h_next[rows, :] = h.astype(BF16)
        return h

    def finish(acc_prev):
        y = xp_ref[...] + _rms(acc_prev[rows, :]) * gpost_ref[...]
        o_ref[...] = y
        return y

    def main(h_cur, h_next, acc_cur, acc_prev):
        n_chunks = MLP_TF // MLP_UP_CHUNK
        us = []
        for c in range(n_chunks):
            cols = slice(c * MLP_UP_CHUNK, (c + 1) * MLP_UP_CHUNK)
            u = jnp.dot(h_cur[...], wup_ref[:, cols], preferred_element_type=F32)
            us.append(jnp.square(jnp.maximum(u, 0.0)).astype(BF16))
            if c == 0:
                _pin_before_next_load(h_cur, finish(acc_prev))
            elif c == 1:
                _pin_before_next_load(h_cur, pre_norm(h_next))
        part = jnp.dot(jnp.concatenate(us, axis=1), wdown_ref[...], preferred_element_type=F32)
        acc_cur[...] = jnp.where(j > 0, acc_cur[...], 0.0) + part

    @pl.when(blk < 0)
    def _():
        pre_norm(h_a)
        acc_a[rows, :] = jnp.zeros((MLP_SUB, D_MODEL), F32)
        acc_b[rows, :] = jnp.zeros((MLP_SUB, D_MODEL), F32)

    in_range = (blk >= 0) & (blk < n_blocks)
    even = (blk % 2) == 0

    @pl.when(in_range & even)
    def _():
        main(h_a, h_b, acc_a, acc_b)

    @pl.when(in_range & jnp.logical_not(even))
    def _():
        main(h_b, h_a, acc_b, acc_a)

    @pl.when(blk == n_blocks)
    def _():
        finish(acc_a if (n_blocks - 1) % 2 == 0 else acc_b)


def _mlp(x2d, g_pre, w_up, w_down, g_post):
    t = x2d.shape[0]
    ni = t // MLP_TM
    nj = D_FF // MLP_TF
    clamp = lambda blk: jnp.clip(blk, 0, ni - 1)
    sub_row = lambda blk, j: (clamp(blk) * nj + j, 0)
    wj = lambda i, j: jnp.where(i < 1, 0, jnp.where(i > ni, nj - 1, j))
    return pl.pallas_call(
        functools.partial(_mlp_kernel, n_blocks=ni),
        grid=(ni + 2, nj),
        in_specs=[
            pl.BlockSpec((MLP_SUB, D_MODEL), lambda i, j: sub_row(i, j)),
            pl.BlockSpec((MLP_SUB, D_MODEL), lambda i, j: sub_row(i - 2, j)),
            pl.BlockSpec((1, D_MODEL), lambda i, j: (0, 0)),
            pl.BlockSpec((D_MODEL, MLP_TF), lambda i, j: (0, wj(i, j))),
            pl.BlockSpec((MLP_TF, D_MODEL), lambda i, j: (wj(i, j), 0)),
            pl.BlockSpec((1, D_MODEL), lambda i, j: (0, 0)),
        ],
        out_specs=pl.BlockSpec((MLP_SUB, D_MODEL), lambda i, j: (jnp.where(i < 2, 0, (i - 2) * nj + j), 0)),
        out_shape=jax.ShapeDtypeStruct((t, D_MODEL), F32),
        scratch_shapes=[pltpu.VMEM((MLP_TM, D_MODEL), BF16), pltpu.VMEM((MLP_TM, D_MODEL), BF16),
                        pltpu.VMEM((MLP_TM, D_MODEL), F32), pltpu.VMEM((MLP_TM, D_MODEL), F32)],
        compiler_params=pltpu.CompilerParams(
            dimension_semantics=("arbitrary", "arbitrary"), vmem_limit_bytes=BIG_VMEM_LIMIT),
        name="mlp",
    )(x2d, x2d, g_pre, w_up, w_down, g_post)


def _cast_w_in_kernel(w_ref, o_ref):
    o_ref[:, 0:IN_COLS] = w_ref[...].astype(BF16)
    o_ref[:, IN_COLS:] = jnp.zeros((W_CAST_ROWS, IN_COLS_PAD - IN_COLS), BF16)


def _cast_w_in(w_in):
    depth = w_in.shape[0]
    return pl.pallas_call(
        _cast_w_in_kernel,
        grid=(depth, D_MODEL // W_CAST_ROWS),
        in_specs=[pl.BlockSpec((None, W_CAST_ROWS, IN_COLS), lambda l, i: (l, i, 0))],
        out_specs=pl.BlockSpec((None, W_CAST_ROWS, IN_COLS_PAD), lambda l, i: (l, i, 0)),
        out_shape=jax.ShapeDtypeStruct((depth, D_MODEL, IN_COLS_PAD), BF16),
        compiler_params=pltpu.CompilerParams(
            dimension_semantics=("parallel", "parallel"), vmem_limit_bytes=VMEM_LIMIT),
        name="cast_w_in",
    )(w_in)


def _rope_tables(seq):
    half = HEAD_DIM // 2
    inv_freq = ROPE_THETA ** (-np.arange(half, dtype=np.float64) / half)
    ang = np.arange(seq, dtype=np.float64)[:, None] * inv_freq[None, :]
    cos = np.cos(ang)
    sin = np.sin(ang)
    cos_t = np.concatenate([cos, cos], axis=1).astype(np.float32)
    sin_t = np.concatenate([-sin, sin], axis=1).astype(np.float32)
    return jnp.asarray(cos_t), jnp.asarray(sin_t)


def kernel(x, w_in, conv_w, gate_bias, ml_norm_g, attn_sink, w_out, g_pre_mix, g_post_mix,
           g_pre_mlp, g_post_mlp, w_up, w_down):
    batch, seq, d = x.shape
    depth = w_in.shape[0]
    cos_t, sin_t = _rope_tables(seq)
    x2d = x.reshape(batch * seq, d)
    w_in_bf = _cast_w_in(w_in)
    for l in range(depth):
        proj, projt, gates_t, w_up_bf, w_out_bf = _in_proj(
            x2d, g_pre_mix[l][None, :], w_in_bf, gate_bias[l][:, None], cos_t, sin_t,
            w_up, w_out, l, seq)
        k_conv, qt_conv = _qk_conv(proj, conv_w[l], seq)
        att = _attention(proj, attn_sink[l], batch, seq)
        hft, hbt = _mlstm(k_conv, qt_conv, projt, gates_t, batch, seq)
        g_ml_b = jnp.broadcast_to(ml_norm_g[l][:, None], (ML_V_COLS, LANES))
        x2d, w_down_bf = _mix(att, hft, hbt, projt, x2d, w_out_bf, g_ml_b, g_post_mix[l][None, :],
                              w_down, l)
        x2d = _mlp(x2d, g_pre_mlp[l][None, :], w_up_bf, w_down_bf, g_post_mlp[l][None, :])
    return x2d.reshape(batch, seq, d)
```

```python
import functools

import jax
import jax.numpy as jnp
import numpy as np
from jax import lax
from jax.experimental import pallas as pl
from jax.experimental.pallas import tpu as pltpu

F32 = jnp.float32
BF16 = jnp.bfloat16

D_MODEL = 2048
ATT_HEADS = 8
ATT_KV_HEADS = 2
ATT_GROUP = ATT_HEADS // ATT_KV_HEADS
HEAD_DIM = 128
ATT_WINDOW = 128
ROPE_THETA = 10000.0
ML_HEADS = 4
ML_V_DIM = 256
ML_QK_DIM = 128
ML_CHUNK = 128
M_INIT = -1e30
ATT_MASKED = -1e30
LOG2_E = 1.4426950408889634
D_FF = 4 * D_MODEL
NORM_EPS = 1e-6
GATE_COLS = 4 * ML_HEADS

ATT_Q_COLS = ATT_HEADS * HEAD_DIM
ATT_KV_COLS = ATT_KV_HEADS * HEAD_DIM
ML_QK_COLS = ML_HEADS * ML_QK_DIM
ML_V_COLS = ML_HEADS * ML_V_DIM

S_AK = ATT_Q_COLS
S_AV = S_AK + ATT_KV_COLS
S_MQ = S_AV + ATT_KV_COLS
S_MV = S_MQ + 2 * ML_QK_COLS
S_GATE = S_MV + 2 * ML_V_COLS
IN_COLS = S_GATE + GATE_COLS
C_AQ = 0
C_MQK = C_AQ + ATT_Q_COLS
C_AK = C_MQK + 2 * ML_QK_COLS
C_AV = C_AK + ATT_KV_COLS
PROJ_COLS = C_AV + ATT_KV_COLS
R_MV = 0
R_MO = R_MV + ML_V_COLS
PROJT_ROWS = R_MO + ML_V_COLS
LANES = 128
SUBLANES = 8
BF16_ROWS = 16

VMEM_LIMIT = 56 * 1024 * 1024
BIG_VMEM_LIMIT = 60 * 1024 * 1024

IN_TM = 512
IN_TN = 512
CONV_TM = 1024
CONV_SUB = 256
ATT_TQ = 1024
MIX_TM = 512
MLP_TM = 512
MLP_TF = 2048
MLP_UP_CHUNK = 512
MLP_SUB = MLP_TM // (D_FF // MLP_TF)
ML_STATE_ROWS = ML_V_DIM + BF16_ROWS
ML_PAIRS = 2 * ML_HEADS
ML_ROW_KINDS = 7
ML_SUB = 2

_NT = (((1,), (1,)), ((), ()))
_TN = (((0,), (0,)), ((), ()))


def _sigmoid(x):
    return 1.0 / (1.0 + jnp.exp(-x))


def _log_sigmoid(x):
    return jnp.minimum(x, 0.0) - jnp.log(1.0 + jnp.exp(-jnp.abs(x)))


def _rms(x, axis=-1):
    return x * lax.rsqrt(jnp.mean(x * x, axis=axis, keepdims=True) + NORM_EPS)


_ROW_MAJOR_CHUNKS = ((0, C_AQ), (IN_TN, C_AQ + IN_TN), (S_AK, C_AK), (S_MQ, C_MQK),
                     (S_MQ + IN_TN, C_MQK + IN_TN))


def _in_proj_kernel(x_ref, g_ref, wt_ref, gbias_ref, cos_ref, sin_ref, wup_ref, wout_ref,
                    proj_ref, projt_ref, gatet_ref, wup_bf_ref, wout_bf_ref, h_ref):
    wup_bf_ref[...] = wup_ref[...].astype(BF16)
    wout_bf_ref[...] = wout_ref[...].astype(BF16)
    h_ref[...] = (_rms(x_ref[...]) * g_ref[...]).astype(BF16)
    cos = cos_ref[...]
    sin = sin_ref[...]

    def rope(a):
        return a * cos + pltpu.roll(a, HEAD_DIM // 2, 1) * sin

    q_scale = HEAD_DIM ** -0.5 * LOG2_E
    for src, dst in _ROW_MAJOR_CHUNKS:
        acc = lax.dot_general(h_ref[...], wt_ref[src:src + IN_TN, :], _NT, preferred_element_type=F32)
        for k in range(IN_TN // HEAD_DIM):
            col = src + k * HEAD_DIM
            a = acc[:, k * HEAD_DIM:(k + 1) * HEAD_DIM]
            if col < S_AK:
                a = rope(a) * q_scale
            elif col < S_AV:
                a = rope(a)
            proj_ref[:, dst + k * HEAD_DIM:dst + (k + 1) * HEAD_DIM] = a.astype(BF16)
    n_chunks = PROJT_ROWS // IN_TN
    for c in range(n_chunks):
        r0 = S_MV + c * IN_TN
        rows = IN_TN + (GATE_COLS if c == n_chunks - 1 else 0)
        acc = lax.dot_general(wt_ref[r0:r0 + rows, :], h_ref[...], _NT, preferred_element_type=F32)
        projt_ref[c * IN_TN:(c + 1) * IN_TN, :] = acc[:IN_TN].astype(BF16)
        if c == n_chunks - 1:
            gatet_ref[...] = acc[IN_TN:] + gbias_ref[...]


def _in_proj(x2d, g, wt_bf_all, gate_bias_col, cos_t, sin_t, w_up_all, w_out_all, layer, seq):
    t = x2d.shape[0]
    steps = t // IN_TM
    pos_blocks = seq // IN_TM
    up_rows = D_MODEL // steps
    const = lambda i: (0, 0)
    row = lambda i: (i, 0)
    layer_row = lambda i: (layer, i, 0)
    return pl.pallas_call(
        _in_proj_kernel,
        grid=(steps,),
        in_specs=[
            pl.BlockSpec((IN_TM, D_MODEL), row),
            pl.BlockSpec((1, D_MODEL), const),
            pl.BlockSpec((None, IN_COLS, D_MODEL), lambda i: (layer, 0, 0), pipeline_mode=pl.Buffered(1)),
            pl.BlockSpec((GATE_COLS, 1), const),
            pl.BlockSpec((IN_TM, HEAD_DIM), lambda i: (i % pos_blocks, 0)),
            pl.BlockSpec((IN_TM, HEAD_DIM), lambda i: (i % pos_blocks, 0)),
            pl.BlockSpec((None, up_rows, D_FF), layer_row),
            pl.BlockSpec((None, up_rows, D_MODEL), layer_row),
        ],
        out_specs=[
            pl.BlockSpec((IN_TM, PROJ_COLS), row),
            pl.BlockSpec((PROJT_ROWS, IN_TM), lambda i: (0, i)),
            pl.BlockSpec((GATE_COLS, IN_TM), lambda i: (0, i)),
            pl.BlockSpec((up_rows, D_FF), row),
            pl.BlockSpec((up_rows, D_MODEL), row),
        ],
        out_shape=[
            jax.ShapeDtypeStruct((t, PROJ_COLS), BF16),
            jax.ShapeDtypeStruct((PROJT_ROWS, t), BF16),
            jax.ShapeDtypeStruct((GATE_COLS, t), F32),
            jax.ShapeDtypeStruct((D_MODEL, D_FF), BF16),
            jax.ShapeDtypeStruct((D_MODEL, D_MODEL), BF16),
        ],
        scratch_shapes=[pltpu.VMEM((IN_TM, D_MODEL), BF16)],
        compiler_params=pltpu.CompilerParams(
            dimension_semantics=("parallel",), vmem_limit_bytes=VMEM_LIMIT),
        name="in_proj",
    )(x2d, g, wt_bf_all, gate_bias_col, cos_t, sin_t, w_up_all, w_out_all)


def _qk_conv_kernel(x_ref, xp_ref, xn_ref, w_ref, k_ref, qt_ref, *, seq_blocks):
    n = CONV_SUB
    pos = pl.program_id(0) % seq_blocks
    has_prev = (pos > 0).astype(F32)
    has_next = (pos < seq_blocks - 1).astype(F32)
    ri = lax.broadcasted_iota(jnp.int32, (n, n), 0)
    ci = lax.broadcasted_iota(jnp.int32, (n, n), 1)
    shift_prev = (ci == ri - 1).astype(BF16)
    shift_next = (ci == ri + 1).astype(BF16)
    rowid = lax.broadcasted_iota(jnp.int32, (n, 1), 0)
    w0 = w_ref[0:1, :]
    w1 = w_ref[1:2, :]
    w2 = w_ref[2:3, :]
    n_sub = CONV_TM // n
    for sb in range(n_sub):
        xs = x_ref[sb * n:(sb + 1) * n, :]
        x_prev = jnp.dot(shift_prev, xs, preferred_element_type=F32)
        x_next = jnp.dot(shift_next, xs, preferred_element_type=F32)
        if sb == 0:
            prev_row = xp_ref[BF16_ROWS - 1:BF16_ROWS, :].astype(F32) * has_prev
        else:
            prev_row = x_ref[sb * n - BF16_ROWS:sb * n, :].astype(F32)[BF16_ROWS - 1:BF16_ROWS]
        if sb == n_sub - 1:
            next_row = xn_ref[0:1, :].astype(F32) * has_next
        else:
            next_row = x_ref[(sb + 1) * n:(sb + 1) * n + BF16_ROWS, :].astype(F32)[0:1]
        x_prev = jnp.where(rowid == 0, prev_row, x_prev)
        x_next = jnp.where(rowid == n - 1, next_row, x_next)
        y = x_prev * w0 + xs.astype(F32) * w1 + x_next * w2
        y = y * _sigmoid(y)
        k_ref[sb * n:(sb + 1) * n, :] = y[:, ML_QK_COLS:].astype(BF16)
        q = y[:, :ML_QK_COLS] * (ML_QK_DIM ** -0.5)
        qt_ref[:, sb * n:(sb + 1) * n] = q.T.astype(BF16)


def _qk_conv(proj, conv_w, seq):
    t = proj.shape[0]
    width = 2 * ML_QK_COLS
    col = C_MQK // width
    per = CONV_TM // BF16_ROWS
    last = t // BF16_ROWS - 1
    return pl.pallas_call(
        functools.partial(_qk_conv_kernel, seq_blocks=seq // CONV_TM),
        grid=(t // CONV_TM,),
        in_specs=[
            pl.BlockSpec((CONV_TM, width), lambda i: (i, col)),
            pl.BlockSpec((BF16_ROWS, width), lambda i: (jnp.maximum(i * per - 1, 0), col)),
            pl.BlockSpec((BF16_ROWS, width), lambda i: (jnp.minimum((i + 1) * per, last), col)),
            pl.BlockSpec((3, width), lambda i: (0, 0)),
        ],
        out_specs=[pl.BlockSpec((CONV_TM, ML_QK_COLS), lambda i: (i, 0)),
                   pl.BlockSpec((ML_QK_COLS, CONV_TM), lambda i: (0, i))],
        out_shape=[jax.ShapeDtypeStruct((t, ML_QK_COLS), BF16),
                   jax.ShapeDtypeStruct((ML_QK_COLS, t), BF16)],
        compiler_params=pltpu.CompilerParams(
            dimension_semantics=("parallel",), vmem_limit_bytes=VMEM_LIMIT),
        name="qk_conv",
    )(proj, proj, proj, conv_w)


def _attn_kernel(sink_ref, q_ref, kc_ref, kp_ref, kn_ref, vc_ref, vp_ref, vn_ref,
                 o_ref, kbuf, vbuf, *, n_blocks):
    w = ATT_WINDOW
    kbuf[0:w] = kp_ref[...]
    kbuf[w:w + ATT_TQ] = kc_ref[...]
    kbuf[w + ATT_TQ:] = kn_ref[...]
    for h in range(ATT_KV_HEADS):
        hs = slice(h * HEAD_DIM, (h + 1) * HEAD_DIM)
        vbuf[h, 0:w, 0:HEAD_DIM] = vp_ref[:, hs]
        vbuf[h, w:w + ATT_TQ, 0:HEAD_DIM] = vc_ref[:, hs]
        vbuf[h, w + ATT_TQ:, 0:HEAD_DIM] = vn_ref[:, hs]
        vbuf[h, :, HEAD_DIM:] = jnp.ones((ATT_TQ + 2 * w, HEAD_DIM), BF16)
    i = pl.program_id(1)
    rows = ATT_GROUP * w
    qi = lax.broadcasted_iota(jnp.int32, (rows, 1), 0) & (w - 1)
    blk = lax.broadcasted_iota(jnp.int32, (rows, 1), 0) // w
    kj = lax.broadcasted_iota(jnp.int32, (rows, 3 * w), 1)
    kj_row = lax.broadcasted_iota(jnp.int32, (1, 3 * w), 1)
    band = jnp.where((kj >= qi) & (kj <= qi + 2 * w), 0.0, ATT_MASKED)
    n_win = ATT_TQ // w
    sinks = []
    for h in range(ATT_KV_HEADS):
        sink = jnp.full((rows, 1), sink_ref[h * ATT_GROUP], F32)
        for g in range(1, ATT_GROUP):
            sink = jnp.where(blk == g, sink_ref[h * ATT_GROUP + g], sink)
        sinks.append(sink * LOG2_E)
    tiles = [(n, h) for n in range(n_win) for h in range(ATT_KV_HEADS)]
    scores = []
    for n, h in tiles:
        qs = jnp.concatenate(
            [q_ref[n * w:(n + 1) * w, (h * ATT_GROUP + g) * HEAD_DIM:(h * ATT_GROUP + g + 1) * HEAD_DIM]
             for g in range(ATT_GROUP)], axis=0)
        kw = kbuf[n * w:(n + 3) * w, h * HEAD_DIM:(h + 1) * HEAD_DIM]
        scores.append(lax.dot_general(qs, kw, _NT, preferred_element_type=F32))
    probs = []
    maxes = []
    for idx, (n, h) in enumerate(tiles):
        bias = band
        if n == 0:
            bias = bias + jnp.where(kj_row < w, jnp.where(i == 0, ATT_MASKED, 0.0), 0.0)
        if n == n_win - 1:
            bias = bias + jnp.where(kj_row >= 2 * w, jnp.where(i == n_blocks - 1, ATT_MASKED, 0.0), 0.0)
        s = scores[idx] + bias
        m = jnp.maximum(jnp.max(s, axis=-1, keepdims=True), sinks[h])
        maxes.append(m)
        probs.append(jnp.exp2(s - m).astype(BF16))
    for idx, (n, h) in enumerate(tiles):
        o_aug = jnp.dot(probs[idx], vbuf[h, n * w:(n + 3) * w, :], preferred_element_type=F32)
        denom = o_aug[:, HEAD_DIM:] + jnp.exp2(sinks[h] - maxes[idx])
        o = o_aug[:, :HEAD_DIM] * (1.0 / denom)
        for g in range(ATT_GROUP):
            col = (h * ATT_GROUP + g) * HEAD_DIM
            o_ref[n * w:(n + 1) * w, col:col + HEAD_DIM] = o[g * w:(g + 1) * w].astype(BF16)


def _attention(proj, sink, batch, seq):
    t = proj.shape[0]
    w = ATT_WINDOW
    nq = seq // ATT_TQ
    per = ATT_TQ // w
    last_blk = t // w - 1
    cur = lambda b, i: b * nq + i
    prev = lambda b, i: jnp.maximum((b * nq + i) * per - 1, 0)
    nxt = lambda b, i: jnp.minimum((b * nq + i + 1) * per, last_blk)
    kcol = C_AK // ATT_KV_COLS
    vcol = C_AV // ATT_KV_COLS
    return pl.pallas_call(
        functools.partial(_attn_kernel, n_blocks=nq),
        grid=(batch, nq),
        in_specs=[
            pl.BlockSpec(memory_space=pltpu.SMEM),
            pl.BlockSpec((ATT_TQ, ATT_Q_COLS), lambda b, i: (cur(b, i), C_AQ // ATT_Q_COLS)),
            pl.BlockSpec((ATT_TQ, ATT_KV_COLS), lambda b, i: (cur(b, i), kcol)),
            pl.BlockSpec((w, ATT_KV_COLS), lambda b, i: (prev(b, i), kcol)),
            pl.BlockSpec((w, ATT_KV_COLS), lambda b, i: (nxt(b, i), kcol)),
            pl.BlockSpec((ATT_TQ, ATT_KV_COLS), lambda b, i: (cur(b, i), vcol)),
            pl.BlockSpec((w, ATT_KV_COLS), lambda b, i: (prev(b, i), vcol)),
            pl.BlockSpec((w, ATT_KV_COLS), lambda b, i: (nxt(b, i), vcol)),
        ],
        out_specs=pl.BlockSpec((ATT_TQ, ATT_Q_COLS), lambda b, i: (cur(b, i), 0)),
        out_shape=jax.ShapeDtypeStruct((t, ATT_Q_COLS), BF16),
        scratch_shapes=[pltpu.VMEM((ATT_TQ + 2 * w, ATT_KV_COLS), BF16),
                        pltpu.VMEM((ATT_KV_HEADS, ATT_TQ + 2 * w, 2 * HEAD_DIM), BF16)],
        compiler_params=pltpu.CompilerParams(
            dimension_semantics=("parallel", "parallel"), vmem_limit_bytes=VMEM_LIMIT),
        name="attn",
    )(sink, proj, proj, proj, proj, proj, proj, proj)


def _rows_to_cols(x):
    length = x.shape[1]
    padded = jnp.concatenate([x, jnp.zeros((length - x.shape[0], length), x.dtype)], axis=0)
    return padded.T


def _ml_gate_rows(gates_f, gates_b, m_prev):
    L = ML_CHUNK
    H = ML_HEADS
    row8 = lax.broadcasted_iota(jnp.int32, (2 * H, L), 0)
    lane = lax.broadcasted_iota(jnp.int32, (2 * H, L), 1)
    is_fwd = row8 < H
    fwd_if = gates_f[0:2 * H]
    bwd_if = gates_b[2 * H:4 * H]
    gi = jnp.where(is_fwd, fwd_if, pltpu.roll(bwd_if, H, 0))
    gf = jnp.where(is_fwd, pltpu.roll(fwd_if, H, 0), bwd_if)
    ls = _log_sigmoid(gf)
    hi = ls.astype(BF16).astype(F32)
    rem = ls - hi
    mid = rem.astype(BF16).astype(F32)
    lo = rem - mid
    parts = jnp.concatenate([hi, mid, lo, jnp.zeros_like(hi)], axis=0).astype(BF16)
    si = lax.broadcasted_iota(jnp.int32, (L, L), 0)
    ti = lax.broadcasted_iota(jnp.int32, (L, L), 1)
    pre = jnp.dot(parts, (si <= ti).astype(BF16), preferred_element_type=F32)
    prefix = pre[0:2 * H] + pre[2 * H:4 * H] + pre[4 * H:6 * H]
    gtot = jnp.sum(ls, axis=1, keepdims=True)
    b = jnp.where(is_fwd, prefix, gtot - prefix + ls)
    r = gi - b
    cm = r
    sh = 1
    while sh < L:
        from_left = jnp.where(lane >= sh, pltpu.roll(cm, sh, 1), -jnp.inf)
        from_right = jnp.where(lane < L - sh, pltpu.roll(cm, L - sh, 1), -jnp.inf)
        cm = jnp.maximum(cm, jnp.where(is_fwd, from_left, from_right))
        sh *= 2
    top = jnp.maximum(m_prev, cm)
    a = gtot - b + gi
    m_loc = jnp.max(a, axis=1, keepdims=True)
    m_new = jnp.maximum(gtot + m_prev, m_loc)
    rows = jnp.concatenate([
        -top,
        jnp.exp(m_prev - top),
        jnp.exp(-(b + top)),
        r,
        jnp.exp(a - m_loc),
        jnp.exp(gtot + m_prev - m_new),
        jnp.exp(m_loc - m_new) + jnp.zeros_like(r),
    ], axis=0)
    return rows, m_new


def _mlstm_kernel(kf_ref, qtf_ref, vtf_ref, gtf_ref, gtfn_ref, kb_ref, qtb_ref, vtb_ref, gtb_ref,
                  gtbn_ref, hf_ref, hb_ref, c_sc, m_sc, rows_sc):
    L = ML_CHUNK
    H = ML_HEADS
    P = ML_PAIRS
    c = pl.program_id(1)
    slot = c % 2

    def tok(d, u):
        lo = u * L if d == 0 else (ML_SUB - 1 - u) * L
        return slice(lo, lo + L)

    def gate_rows(gf_ref, gb_ref, u, m_prev):
        return _ml_gate_rows(gf_ref[:, tok(0, u)], gb_ref[:, tok(1, u)], m_prev)

    @pl.when(c == 0)
    def _():
        c_sc[...] = jnp.zeros_like(c_sc)
        m = jnp.full((P, L), M_INIT, F32)
        for u in range(ML_SUB):
            rows0, m = gate_rows(gtf_ref, gtb_ref, u, m)
            rows_sc[0, u] = rows0
        m_sc[...] = m

    rows_now = [rows_sc[slot, u] for u in range(ML_SUB)]
    m = m_sc[...]
    for u in range(ML_SUB):
        rows_next, m = gate_rows(gtfn_ref, gtbn_ref, u, m)
        rows_sc[1 - slot, u] = rows_next
    m_sc[...] = m

    si = lax.broadcasted_iota(jnp.int32, (L, L), 0)
    ti = lax.broadcasted_iota(jnp.int32, (L, L), 1)
    masks = (si <= ti, si >= ti)
    n_row = lax.broadcasted_iota(jnp.int32, (BF16_ROWS, 1), 0) == 0
    refs = ((kf_ref, qtf_ref, vtf_ref, hf_ref), (kb_ref, qtb_ref, vtb_ref, hb_ref))
    pairs = [(d, j) for d in range(2) for j in range(H)]

    for u in range(ML_SUB):
        rows = rows_now[u]
        bm, w_inter, clamp, r, w_end, s_prev, s_loc = [rows[i * P:(i + 1) * P] for i in range(ML_ROW_KINDS)]
        r_cols = _rows_to_cols(r)
        w_end_cols = _rows_to_cols(w_end)

        def k_of(d, j):
            return refs[d][0][tok(d, u), j * ML_QK_DIM:(j + 1) * ML_QK_DIM]

        def qt_of(d, j):
            return refs[d][1][j * ML_QK_DIM:(j + 1) * ML_QK_DIM, tok(d, u)]

        scores = []
        carried = []
        for d, j in pairs:
            p = d * H + j
            scores.append(jnp.dot(k_of(d, j), qt_of(d, j), preferred_element_type=F32))
            qtw = (qt_of(d, j).astype(F32) * w_inter[p:p + 1, :]).astype(BF16)
            carried.append(jnp.dot(c_sc[p].astype(BF16), qtw, preferred_element_type=F32))
        dens = []
        n_locs = []
        rhss = []
        for d, j in pairs:
            p = d * H + j
            arg = jnp.broadcast_to(r_cols[:, p:p + 1], (L, L)) + bm[p:p + 1, :]
            sc_t = scores[p] * jnp.exp(jnp.where(masks[d], arg, -jnp.inf))
            dens.append(jnp.sum(sc_t, axis=0, keepdims=True))
            kw = k_of(d, j).astype(F32) * jnp.broadcast_to(w_end_cols[:, p:p + 1], (L, ML_QK_DIM))
            n_locs.append(jnp.sum(kw, axis=0, keepdims=True))
            rhss.append(jnp.concatenate([sc_t.astype(BF16), kw.astype(BF16)], axis=1))
        boths = []
        for d, j in pairs:
            p = d * H + j
            vt_j = refs[d][2][j * ML_V_DIM:(j + 1) * ML_V_DIM, tok(d, u)]
            boths.append(jnp.dot(vt_j, rhss[p], preferred_element_type=F32))
        for d, j in pairs:
            p = d * H + j
            num = boths[p][:, :L] + carried[p][:ML_V_DIM]
            den = dens[p] + carried[p][ML_V_DIM:ML_V_DIM + 1]
            refs[d][3][j * ML_V_DIM:(j + 1) * ML_V_DIM, tok(d, u)] = (
                num / jnp.maximum(jnp.abs(den), clamp[p:p + 1, :]))
            c_prev = c_sc[p]
            sp = s_prev[p:p + 1, :]
            sl = s_loc[p:p + 1, :]
            c_sc[p, 0:ML_V_DIM, :] = sp * c_prev[:ML_V_DIM] + sl * boths[p][:, L:]
            c_sc[p, ML_V_DIM:, :] = sp * c_prev[ML_V_DIM:] + sl * jnp.where(n_row, n_locs[p], 0.0)


def _mlstm(k_conv, qt_conv, projt, gates_t, batch, seq):
    t = k_conv.shape[0]
    blk = ML_SUB * ML_CHUNK
    nc = seq // blk
    fwd = lambda b, c: b * nc + c
    bwd = lambda b, c: b * nc + nc - 1 - c
    nxt = lambda c: jnp.minimum(c + 1, nc - 1)

    def dir_specs(ch):
        return [
            pl.BlockSpec((blk, ML_QK_COLS), lambda b, c: (ch(b, c), 0)),
            pl.BlockSpec((ML_QK_COLS, blk), lambda b, c: (0, ch(b, c))),
            pl.BlockSpec((ML_V_COLS, blk), lambda b, c: (R_MV // ML_V_COLS, ch(b, c))),
            pl.BlockSpec((GATE_COLS, blk), lambda b, c: (0, ch(b, c))),
            pl.BlockSpec((GATE_COLS, blk), lambda b, c: (0, ch(b, nxt(c)))),
        ]

    return pl.pallas_call(
        _mlstm_kernel,
        grid=(batch, nc),
        in_specs=dir_specs(fwd) + dir_specs(bwd),
        out_specs=[pl.BlockSpec((ML_V_COLS, blk), lambda b, c: (0, fwd(b, c))),
                   pl.BlockSpec((ML_V_COLS, blk), lambda b, c: (0, bwd(b, c)))],
        out_shape=[jax.ShapeDtypeStruct((ML_V_COLS, t), F32),
                   jax.ShapeDtypeStruct((ML_V_COLS, t), F32)],
        scratch_shapes=[pltpu.VMEM((ML_PAIRS, ML_STATE_ROWS, ML_QK_DIM), F32),
                        pltpu.VMEM((ML_PAIRS, ML_CHUNK), F32),
                        pltpu.VMEM((2, ML_SUB, ML_ROW_KINDS * ML_PAIRS, ML_CHUNK), F32)],
        compiler_params=pltpu.CompilerParams(
            dimension_semantics=("parallel", "arbitrary"), vmem_limit_bytes=VMEM_LIMIT),
        name="mlstm",
    )(k_conv, qt_conv, projt, gates_t, gates_t, k_conv, qt_conv, projt, gates_t, gates_t)


def _mix_kernel(att_ref, hft_ref, hbt_ref, mot_ref, x_ref, w_ref, gml_ref, gpost_ref, wdown_ref,
                o_ref, wdown_bf_ref, cat_ref):
    wdown_bf_ref[...] = wdown_ref[...].astype(BF16)
    cat_ref[:, 0:ATT_Q_COLS] = att_ref[...]
    for j in range(ML_HEADS):
        sl = slice(j * ML_V_DIM, (j + 1) * ML_V_DIM)
        h = _rms(hft_ref[sl, :] + hbt_ref[sl, :], axis=0)
        gain = jnp.concatenate([gml_ref[sl, :]] * (MIX_TM // LANES), axis=1)
        mem = h * gain * _sigmoid(mot_ref[sl, :].astype(F32))
        cat_ref[:, ATT_Q_COLS + j * ML_V_DIM:ATT_Q_COLS + (j + 1) * ML_V_DIM] = mem.T.astype(BF16)
    mix = jnp.dot(cat_ref[...], w_ref[...], preferred_element_type=F32)
    o_ref[...] = x_ref[...] + _rms(mix) * gpost_ref[...]


def _mix(att, hft, hbt, projt, x2d, w_out_bf, g_ml_b, g_post, w_down_all, layer):
    t = x2d.shape[0]
    steps = t // MIX_TM
    down_rows = D_FF // steps
    row = lambda i: (i, 0)
    col = lambda i: (0, i)
    const = lambda i: (0, 0)
    return pl.pallas_call(
        _mix_kernel,
        grid=(steps,),
        in_specs=[
            pl.BlockSpec((MIX_TM, ATT_Q_COLS), row),
            pl.BlockSpec((ML_V_COLS, MIX_TM), col),
            pl.BlockSpec((ML_V_COLS, MIX_TM), col),
            pl.BlockSpec((ML_V_COLS, MIX_TM), lambda i: (R_MO // ML_V_COLS, i)),
            pl.BlockSpec((MIX_TM, D_MODEL), row),
            pl.BlockSpec((D_MODEL, D_MODEL), const, pipeline_mode=pl.Buffered(1)),
            pl.BlockSpec((ML_V_COLS, LANES), const),
            pl.BlockSpec((1, D_MODEL), const),
            pl.BlockSpec((None, down_rows, D_MODEL), lambda i: (layer, i, 0)),
        ],
        out_specs=[pl.BlockSpec((MIX_TM, D_MODEL), row),
                   pl.BlockSpec((down_rows, D_MODEL), row)],
        out_shape=[jax.ShapeDtypeStruct((t, D_MODEL), F32),
                   jax.ShapeDtypeStruct((D_FF, D_MODEL), BF16)],
        scratch_shapes=[pltpu.VMEM((MIX_TM, D_MODEL), BF16)],
        compiler_params=pltpu.CompilerParams(
            dimension_semantics=("parallel",), vmem_limit_bytes=VMEM_LIMIT),
        name="mix",
    )(att, hft, hbt, projt, x2d, w_out_bf, g_ml_b, g_post, w_down_all)


def _pin_before_next_load(operand_ref, value):
    bits = pltpu.bitcast(value, jnp.uint32)
    tiles = [bits[r:r + SUBLANES, c:c + LANES]
             for r in range(0, value.shape[0], SUBLANES) for c in range(0, value.shape[1], LANES)]
    folded = functools.reduce(jnp.bitwise_or, tiles)
    half = jnp.uint32(16)
    zero = pltpu.bitcast(lax.shift_right_logical(lax.shift_right_logical(folded, half), half), F32)
    zero = jnp.concatenate([zero] * (BF16_ROWS // SUBLANES), axis=0).astype(BF16)
    operand_ref[0:BF16_ROWS, 0:LANES] = operand_ref[0:BF16_ROWS, 0:LANES] + zero


def _mlp_kernel(xn_ref, xp_ref, gpre_ref, wup_ref, wdown_ref, gpost_ref, o_ref,
                h_a, h_b, acc_a, acc_b, *, n_blocks):
    blk = pl.program_id(0) - 1
    j = pl.program_id(1)
    rows = pl.ds(pl.multiple_of(j * MLP_SUB, MLP_SUB), MLP_SUB)

    def pre_norm(h_next):
        h = _rms(xn_ref[...]) * gpre_ref[...]
        h_next[rows, :] = h.astype(BF16)
        return h

    def finish(acc_prev):
        y = xp_ref[...] + _rms(acc_prev[rows, :]) * gpost_ref[...]
        o_ref[...] = y
        return y

    def main(h_cur, h_next, acc_cur, acc_prev):
        n_chunks = MLP_TF // MLP_UP_CHUNK
        us = []
        for c in range(n_chunks):
            cols = slice(c * MLP_UP_CHUNK, (c + 1) * MLP_UP_CHUNK)
            u = jnp.dot(h_cur[...], wup_ref[:, cols], preferred_element_type=F32)
            us.append(jnp.square(jnp.maximum(u, 0.0)).astype(BF16))
            if c == 0:
                _pin_before_next_load(h_cur, finish(acc_prev))
            elif c == 1:
                _pin_before_next_load(h_cur, pre_norm(h_next))
        part = jnp.dot(jnp.concatenate(us, axis=1), wdown_ref[...], preferred_element_type=F32)
        acc_cur[...] = jnp.where(j > 0, acc_cur[...], 0.0) + part

    @pl.when(blk < 0)
    def _():
        pre_norm(h_a)
        acc_a[rows, :] = jnp.zeros((MLP_SUB, D_MODEL), F32)
        acc_b[rows, :] = jnp.zeros((MLP_SUB, D_MODEL), F32)

    in_range = (blk >= 0) & (blk < n_blocks)
    even = (blk % 2) == 0

    @pl.when(in_range & even)
    def _():
        main(h_a, h_b, acc_a, acc_b)

    @pl.when(in_range & jnp.logical_not(even))
    def _():
        main(h_b, h_a, acc_b, acc_a)

    @pl.when(blk == n_blocks)
    def _():
        finish(acc_a if (n_blocks - 1) % 2 == 0 else acc_b)


def _mlp(x2d, g_pre, w_up, w_down, g_post):
    t = x2d.shape[0]
    ni = t // MLP_TM
    nj = D_FF // MLP_TF
    clamp = lambda blk: jnp.clip(blk, 0, ni - 1)
    sub_row = lambda blk, j: (clamp(blk) * nj + j, 0)
    wj = lambda i, j: jnp.where(i < 1, 0, jnp.where(i > ni, nj - 1, j))
    return pl.pallas_call(
        functools.partial(_mlp_kernel, n_blocks=ni),
        grid=(ni + 2, nj),
        in_specs=[
            pl.BlockSpec((MLP_SUB, D_MODEL), lambda i, j: sub_row(i, j)),
            pl.BlockSpec((MLP_SUB, D_MODEL), lambda i, j: sub_row(i - 2, j)),
            pl.BlockSpec((1, D_MODEL), lambda i, j: (0, 0)),
            pl.BlockSpec((D_MODEL, MLP_TF), lambda i, j: (0, wj(i, j))),
            pl.BlockSpec((MLP_TF, D_MODEL), lambda i, j: (wj(i, j), 0)),
            pl.BlockSpec((1, D_MODEL), lambda i, j: (0, 0)),
        ],
        out_specs=pl.BlockSpec((MLP_SUB, D_MODEL), lambda i, j: (jnp.where(i < 2, 0, (i - 2) * nj + j), 0)),
        out_shape=jax.ShapeDtypeStruct((t, D_MODEL), F32),
        scratch_shapes=[pltpu.VMEM((MLP_TM, D_MODEL), BF16), pltpu.VMEM((MLP_TM, D_MODEL), BF16),
                        pltpu.VMEM((MLP_TM, D_MODEL), F32), pltpu.VMEM((MLP_TM, D_MODEL), F32)],
        compiler_params=pltpu.CompilerParams(
            dimension_semantics=("arbitrary", "arbitrary"), vmem_limit_bytes=BIG_VMEM_LIMIT),
        name="mlp",
    )(x2d, x2d, g_pre, w_up, w_down, g_post)


def _rope_tables(seq):
    half = HEAD_DIM // 2
    inv_freq = ROPE_THETA ** (-np.arange(half, dtype=np.float64) / half)
    ang = np.arange(seq, dtype=np.float64)[:, None] * inv_freq[None, :]
    cos = np.cos(ang)
    sin = np.sin(ang)
    cos_t = np.concatenate([cos, cos], axis=1).astype(np.float32)
    sin_t = np.concatenate([-sin, sin], axis=1).astype(np.float32)
    return jnp.asarray(cos_t), jnp.asarray(sin_t)


def kernel(x, w_in, conv_w, gate_bias, ml_norm_g, attn_sink, w_out, g_pre_mix, g_post_mix,
           g_pre_mlp, g_post_mlp, w_up, w_down):
    batch, seq, d = x.shape
    depth = w_in.shape[0]
    cos_t, sin_t = _rope_tables(seq)
    x2d = x.reshape(batch * seq, d)
    w_in_bf = jnp.swapaxes(w_in, 1, 2).astype(BF16)
    for l in range(depth):
        proj, projt, gates_t, w_up_bf, w_out_bf = _in_proj(
            x2d, g_pre_mix[l][None, :], w_in_bf, gate_bias[l][:, None], cos_t, sin_t,
            w_up, w_out, l, seq)
        k_conv, qt_conv = _qk_conv(proj, conv_w[l], seq)
        att = _attention(proj, attn_sink[l], batch, seq)
        hft, hbt = _mlstm(k_conv, qt_conv, projt, gates_t, batch, seq)
        g_ml_b = jnp.broadcast_to(ml_norm_g[l][:, None], (ML_V_COLS, LANES))
        x2d, w_down_bf = _mix(att, hft, hbt, projt, x2d, w_out_bf, g_ml_b, g_post_mix[l][None, :],
                              w_down, l)
        x2d = _mlp(x2d, g_pre_mlp[l][None, :], w_up_bf, w_down_bf, g_post_mlp[l][None, :])
    return x2d.reshape(batch, seq, d)
```

```python
import functools

import jax
import jax.numpy as jnp
import numpy as np
from jax import lax
from jax.experimental import pallas as pl
from jax.experimental.pallas import tpu as pltpu

F32 = jnp.float32
BF16 = jnp.bfloat16

D_MODEL = 2048
ATT_HEADS = 8
ATT_KV_HEADS = 2
ATT_GROUP = ATT_HEADS // ATT_KV_HEADS
HEAD_DIM = 128
ATT_WINDOW = 128
ROPE_THETA = 10000.0
ML_HEADS = 4
ML_V_DIM = 256
ML_QK_DIM = 128
ML_CHUNK = 128
M_INIT = -1e30
ATT_MASKED = -1e30
LOG2_E = 1.4426950408889634
D_FF = 4 * D_MODEL
NORM_EPS = 1e-6
GATE_COLS = 4 * ML_HEADS

ATT_Q_COLS = ATT_HEADS * HEAD_DIM
ATT_KV_COLS = ATT_KV_HEADS * HEAD_DIM
ML_QK_COLS = ML_HEADS * ML_QK_DIM
ML_V_COLS = ML_HEADS * ML_V_DIM

S_AK = ATT_Q_COLS
S_AV = S_AK + ATT_KV_COLS
S_MQ = S_AV + ATT_KV_COLS
S_MV = S_MQ + 2 * ML_QK_COLS
S_GATE = S_MV + 2 * ML_V_COLS
IN_COLS = S_GATE + GATE_COLS
C_AQ = 0
C_MQK = C_AQ + ATT_Q_COLS
C_AK = C_MQK + 2 * ML_QK_COLS
C_AV = C_AK + ATT_KV_COLS
PROJ_COLS = C_AV + ATT_KV_COLS
R_MV = 0
R_MO = R_MV + ML_V_COLS
PROJT_ROWS = R_MO + ML_V_COLS
LANES = 128
SUBLANES = 8
BF16_ROWS = 16

VMEM_LIMIT = 56 * 1024 * 1024
BIG_VMEM_LIMIT = 60 * 1024 * 1024

IN_TM = 512
IN_TN = 512
CONV_TM = 1024
CONV_SUB = 256
ATT_TQ = 1024
MIX_TM = 512
MLP_TM = 512
MLP_TF = 2048
MLP_UP_CHUNK = 512
MLP_SUB = MLP_TM // (D_FF // MLP_TF)
ML_STATE_ROWS = ML_V_DIM + BF16_ROWS
ML_PAIRS = 2 * ML_HEADS
ML_ROW_KINDS = 7
ML_SUB = 4

_NT = (((1,), (1,)), ((), ()))
_TN = (((0,), (0,)), ((), ()))


def _sigmoid(x):
    return 1.0 / (1.0 + jnp.exp(-x))


def _log_sigmoid(x):
    return jnp.minimum(x, 0.0) - jnp.log(1.0 + jnp.exp(-jnp.abs(x)))


def _rms(x, axis=-1):
    return x * lax.rsqrt(jnp.mean(x * x, axis=axis, keepdims=True) + NORM_EPS)


_ROW_MAJOR_CHUNKS = ((0, C_AQ), (IN_TN, C_AQ + IN_TN), (S_AK, C_AK), (S_MQ, C_MQK),
                     (S_MQ + IN_TN, C_MQK + IN_TN))


def _in_proj_kernel(x_ref, g_ref, wt_ref, gbias_ref, cos_ref, sin_ref, wup_ref, wout_ref, wdown_ref,
                    proj_ref, projt_ref, gatet_ref, wup_bf_ref, wout_bf_ref, wdown_bf_ref, h_ref):
    wup_bf_ref[...] = wup_ref[...].astype(BF16)
    wout_bf_ref[...] = wout_ref[...].astype(BF16)
    wdown_bf_ref[...] = wdown_ref[...].astype(BF16)
    h_ref[...] = (_rms(x_ref[...]) * g_ref[...]).astype(BF16)
    cos = cos_ref[...]
    sin = sin_ref[...]

    def rope(a):
        return a * cos + pltpu.roll(a, HEAD_DIM // 2, 1) * sin

    q_scale = HEAD_DIM ** -0.5 * LOG2_E
    for src, dst in _ROW_MAJOR_CHUNKS:
        acc = lax.dot_general(h_ref[...], wt_ref[src:src + IN_TN, :], _NT, preferred_element_type=F32)
        for k in range(IN_TN // HEAD_DIM):
            col = src + k * HEAD_DIM
            a = acc[:, k * HEAD_DIM:(k + 1) * HEAD_DIM]
            if col < S_AK:
                a = rope(a) * q_scale
            elif col < S_AV:
                a = rope(a)
            proj_ref[:, dst + k * HEAD_DIM:dst + (k + 1) * HEAD_DIM] = a.astype(BF16)
    n_chunks = PROJT_ROWS // IN_TN
    for c in range(n_chunks):
        r0 = S_MV + c * IN_TN
        rows = IN_TN + (GATE_COLS if c == n_chunks - 1 else 0)
        acc = lax.dot_general(wt_ref[r0:r0 + rows, :], h_ref[...], _NT, preferred_element_type=F32)
        projt_ref[c * IN_TN:(c + 1) * IN_TN, :] = acc[:IN_TN].astype(BF16)
        if c == n_chunks - 1:
            gatet_ref[...] = acc[IN_TN:] + gbias_ref[...]


def _in_proj(x2d, g, wt_bf_all, gate_bias_col, cos_t, sin_t, w_up_all, w_out_all, w_down_all, layer, seq):
    t = x2d.shape[0]
    steps = t // IN_TM
    pos_blocks = seq // IN_TM
    up_rows = D_MODEL // steps
    down_rows = D_FF // steps
    const = lambda i: (0, 0)
    row = lambda i: (i, 0)
    layer_row = lambda i: (layer, i, 0)
    return pl.pallas_call(
        _in_proj_kernel,
        grid=(steps,),
        in_specs=[
            pl.BlockSpec((IN_TM, D_MODEL), row),
            pl.BlockSpec((1, D_MODEL), const),
            pl.BlockSpec((None, IN_COLS, D_MODEL), lambda i: (layer, 0, 0), pipeline_mode=pl.Buffered(1)),
            pl.BlockSpec((GATE_COLS, 1), const),
            pl.BlockSpec((IN_TM, HEAD_DIM), lambda i: (i % pos_blocks, 0)),
            pl.BlockSpec((IN_TM, HEAD_DIM), lambda i: (i % pos_blocks, 0)),
            pl.BlockSpec((None, up_rows, D_FF), layer_row),
            pl.BlockSpec((None, up_rows, D_MODEL), layer_row),
            pl.BlockSpec((None, down_rows, D_MODEL), layer_row),
        ],
        out_specs=[
            pl.BlockSpec((IN_TM, PROJ_COLS), row),
            pl.BlockSpec((PROJT_ROWS, IN_TM), lambda i: (0, i)),
            pl.BlockSpec((GATE_COLS, IN_TM), lambda i: (0, i)),
            pl.BlockSpec((up_rows, D_FF), row),
            pl.BlockSpec((up_rows, D_MODEL), row),
            pl.BlockSpec((down_rows, D_MODEL), row),
        ],
        out_shape=[
            jax.ShapeDtypeStruct((t, PROJ_COLS), BF16),
            jax.ShapeDtypeStruct((PROJT_ROWS, t), BF16),
            jax.ShapeDtypeStruct((GATE_COLS, t), F32),
            jax.ShapeDtypeStruct((D_MODEL, D_FF), BF16),
            jax.ShapeDtypeStruct((D_MODEL, D_MODEL), BF16),
            jax.ShapeDtypeStruct((D_FF, D_MODEL), BF16),
        ],
        scratch_shapes=[pltpu.VMEM((IN_TM, D_MODEL), BF16)],
        compiler_params=pltpu.CompilerParams(
            dimension_semantics=("parallel",), vmem_limit_bytes=BIG_VMEM_LIMIT),
        name="in_proj",
    )(x2d, g, wt_bf_all, gate_bias_col, cos_t, sin_t, w_up_all, w_out_all, w_down_all)


def _qk_conv_kernel(x_ref, xp_ref, xn_ref, w_ref, k_ref, qt_ref, *, seq_blocks):
    n = CONV_SUB
    pos = pl.program_id(0) % seq_blocks
    has_prev = (pos > 0).astype(F32)
    has_next = (pos < seq_blocks - 1).astype(F32)
    ri = lax.broadcasted_iota(jnp.int32, (n, n), 0)
    ci = lax.broadcasted_iota(jnp.int32, (n, n), 1)
    shift_prev = (ci == ri - 1).astype(BF16)
    shift_next = (ci == ri + 1).astype(BF16)
    rowid = lax.broadcasted_iota(jnp.int32, (n, 1), 0)
    w0 = w_ref[0:1, :]
    w1 = w_ref[1:2, :]
    w2 = w_ref[2:3, :]
    n_sub = CONV_TM // n
    for sb in range(n_sub):
        xs = x_ref[sb * n:(sb + 1) * n, :]
        x_prev = jnp.dot(shift_prev, xs, preferred_element_type=F32)
        x_next = jnp.dot(shift_next, xs, preferred_element_type=F32)
        if sb == 0:
            prev_row = xp_ref[BF16_ROWS - 1:BF16_ROWS, :].astype(F32) * has_prev
        else:
            prev_row = x_ref[sb * n - BF16_ROWS:sb * n, :].astype(F32)[BF16_ROWS - 1:BF16_ROWS]
        if sb == n_sub - 1:
            next_row = xn_ref[0:1, :].astype(F32) * has_next
        else:
            next_row = x_ref[(sb + 1) * n:(sb + 1) * n + BF16_ROWS, :].astype(F32)[0:1]
        x_prev = jnp.where(rowid == 0, prev_row, x_prev)
        x_next = jnp.where(rowid == n - 1, next_row, x_next)
        y = x_prev * w0 + xs.astype(F32) * w1 + x_next * w2
        y = y * _sigmoid(y)
        k_ref[sb * n:(sb + 1) * n, :] = y[:, ML_QK_COLS:].astype(BF16)
        q = y[:, :ML_QK_COLS] * (ML_QK_DIM ** -0.5)
        qt_ref[:, sb * n:(sb + 1) * n] = q.T.astype(BF16)


def _qk_conv(proj, conv_w, seq):
    t = proj.shape[0]
    width = 2 * ML_QK_COLS
    col = C_MQK // width
    per = CONV_TM // BF16_ROWS
    last = t // BF16_ROWS - 1
    return pl.pallas_call(
        functools.partial(_qk_conv_kernel, seq_blocks=seq // CONV_TM),
        grid=(t // CONV_TM,),
        in_specs=[
            pl.BlockSpec((CONV_TM, width), lambda i: (i, col)),
            pl.BlockSpec((BF16_ROWS, width), lambda i: (jnp.maximum(i * per - 1, 0), col)),
            pl.BlockSpec((BF16_ROWS, width), lambda i: (jnp.minimum((i + 1) * per, last), col)),
            pl.BlockSpec((3, width), lambda i: (0, 0)),
        ],
        out_specs=[pl.BlockSpec((CONV_TM, ML_QK_COLS), lambda i: (i, 0)),
                   pl.BlockSpec((ML_QK_COLS, CONV_TM), lambda i: (0, i))],
        out_shape=[jax.ShapeDtypeStruct((t, ML_QK_COLS), BF16),
                   jax.ShapeDtypeStruct((ML_QK_COLS, t), BF16)],
        compiler_params=pltpu.CompilerParams(
            dimension_semantics=("parallel",), vmem_limit_bytes=VMEM_LIMIT),
        name="qk_conv",
    )(proj, proj, proj, conv_w)


def _attn_kernel(sink_ref, q_ref, kc_ref, kp_ref, kn_ref, vc_ref, vp_ref, vn_ref,
                 o_ref, kbuf, vbuf, *, n_blocks):
    w = ATT_WINDOW
    kbuf[0:w] = kp_ref[...]
    kbuf[w:w + ATT_TQ] = kc_ref[...]
    kbuf[w + ATT_TQ:] = kn_ref[...]
    for h in range(ATT_KV_HEADS):
        hs = slice(h * HEAD_DIM, (h + 1) * HEAD_DIM)
        vbuf[h, 0:w, 0:HEAD_DIM] = vp_ref[:, hs]
        vbuf[h, w:w + ATT_TQ, 0:HEAD_DIM] = vc_ref[:, hs]
        vbuf[h, w + ATT_TQ:, 0:HEAD_DIM] = vn_ref[:, hs]
        vbuf[h, :, HEAD_DIM:] = jnp.ones((ATT_TQ + 2 * w, HEAD_DIM), BF16)
    i = pl.program_id(1)
    rows = ATT_GROUP * w
    qi = lax.broadcasted_iota(jnp.int32, (rows, 1), 0) & (w - 1)
    blk = lax.broadcasted_iota(jnp.int32, (rows, 1), 0) // w
    kj = lax.broadcasted_iota(jnp.int32, (rows, 3 * w), 1)
    kj_row = lax.broadcasted_iota(jnp.int32, (1, 3 * w), 1)
    band = jnp.where((kj >= qi) & (kj <= qi + 2 * w), 0.0, ATT_MASKED)
    n_win = ATT_TQ // w
    sinks = []
    for h in range(ATT_KV_HEADS):
        sink = jnp.full((rows, 1), sink_ref[h * ATT_GROUP], F32)
        for g in range(1, ATT_GROUP):
            sink = jnp.where(blk == g, sink_ref[h * ATT_GROUP + g], sink)
        sinks.append(sink * LOG2_E)
    tiles = [(n, h) for n in range(n_win) for h in range(ATT_KV_HEADS)]
    scores = []
    for n, h in tiles:
        qs = jnp.concatenate(
            [q_ref[n * w:(n + 1) * w, (h * ATT_GROUP + g) * HEAD_DIM:(h * ATT_GROUP + g + 1) * HEAD_DIM]
             for g in range(ATT_GROUP)], axis=0)
        kw = kbuf[n * w:(n + 3) * w, h * HEAD_DIM:(h + 1) * HEAD_DIM]
        scores.append(lax.dot_general(qs, kw, _NT, preferred_element_type=F32))
    probs = []
    maxes = []
    for idx, (n, h) in enumerate(tiles):
        bias = band
        if n == 0:
            bias = bias + jnp.where(kj_row < w, jnp.where(i == 0, ATT_MASKED, 0.0), 0.0)
        if n == n_win - 1:
            bias = bias + jnp.where(kj_row >= 2 * w, jnp.where(i == n_blocks - 1, ATT_MASKED, 0.0), 0.0)
        s = scores[idx] + bias
        m = jnp.maximum(jnp.max(s, axis=-1, keepdims=True), sinks[h])
        maxes.append(m)
        probs.append(jnp.exp2(s - m).astype(BF16))
    for idx, (n, h) in enumerate(tiles):
        o_aug = jnp.dot(probs[idx], vbuf[h, n * w:(n + 3) * w, :], preferred_element_type=F32)
        denom = o_aug[:, HEAD_DIM:] + jnp.exp2(sinks[h] - maxes[idx])
        o = o_aug[:, :HEAD_DIM] * (1.0 / denom)
        for g in range(ATT_GROUP):
            col = (h * ATT_GROUP + g) * HEAD_DIM
            o_ref[n * w:(n + 1) * w, col:col + HEAD_DIM] = o[g * w:(g + 1) * w].astype(BF16)


def _attention(proj, sink, batch, seq):
    t = proj.shape[0]
    w = ATT_WINDOW
    nq = seq // ATT_TQ
    per = ATT_TQ // w
    last_blk = t // w - 1
    cur = lambda b, i: b * nq + i
    prev = lambda b, i: jnp.maximum((b * nq + i) * per - 1, 0)
    nxt = lambda b, i: jnp.minimum((b * nq + i + 1) * per, last_blk)
    kcol = C_AK // ATT_KV_COLS
    vcol = C_AV // ATT_KV_COLS
    return pl.pallas_call(
        functools.partial(_attn_kernel, n_blocks=nq),
        grid=(batch, nq),
        in_specs=[
            pl.BlockSpec(memory_space=pltpu.SMEM),
            pl.BlockSpec((ATT_TQ, ATT_Q_COLS), lambda b, i: (cur(b, i), C_AQ // ATT_Q_COLS)),
            pl.BlockSpec((ATT_TQ, ATT_KV_COLS), lambda b, i: (cur(b, i), kcol)),
            pl.BlockSpec((w, ATT_KV_COLS), lambda b, i: (prev(b, i), kcol)),
            pl.BlockSpec((w, ATT_KV_COLS), lambda b, i: (nxt(b, i), kcol)),
            pl.BlockSpec((ATT_TQ, ATT_KV_COLS), lambda b, i: (cur(b, i), vcol)),
            pl.BlockSpec((w, ATT_KV_COLS), lambda b, i: (prev(b, i), vcol)),
            pl.BlockSpec((w, ATT_KV_COLS), lambda b, i: (nxt(b, i), vcol)),
        ],
        out_specs=pl.BlockSpec((ATT_TQ, ATT_Q_COLS), lambda b, i: (cur(b, i), 0)),
        out_shape=jax.ShapeDtypeStruct((t, ATT_Q_COLS), BF16),
        scratch_shapes=[pltpu.VMEM((ATT_TQ + 2 * w, ATT_KV_COLS), BF16),
                        pltpu.VMEM((ATT_KV_HEADS, ATT_TQ + 2 * w, 2 * HEAD_DIM), BF16)],
        compiler_params=pltpu.CompilerParams(
            dimension_semantics=("parallel", "parallel"), vmem_limit_bytes=VMEM_LIMIT),
        name="attn",
    )(sink, proj, proj, proj, proj, proj, proj, proj)


def _rows_to_cols(x):
    length = x.shape[1]
    padded = jnp.concatenate([x, jnp.zeros((length - x.shape[0], length), x.dtype)], axis=0)
    return padded.T


def _ml_gate_rows(gates_f, gates_b, m_prev):
    L = ML_CHUNK
    H = ML_HEADS
    row8 = lax.broadcasted_iota(jnp.int32, (2 * H, L), 0)
    lane = lax.broadcasted_iota(jnp.int32, (2 * H, L), 1)
    is_fwd = row8 < H
    fwd_if = gates_f[0:2 * H]
    bwd_if = gates_b[2 * H:4 * H]
    gi = jnp.where(is_fwd, fwd_if, pltpu.roll(bwd_if, H, 0))
    gf = jnp.where(is_fwd, pltpu.roll(fwd_if, H, 0), bwd_if)
    ls = _log_sigmoid(gf)
    hi = ls.astype(BF16).astype(F32)
    rem = ls - hi
    mid = rem.astype(BF16).astype(F32)
    lo = rem - mid
    parts = jnp.concatenate([hi, mid, lo, jnp.zeros_like(hi)], axis=0).astype(BF16)
    si = lax.broadcasted_iota(jnp.int32, (L, L), 0)
    ti = lax.broadcasted_iota(jnp.int32, (L, L), 1)
    pre = jnp.dot(parts, (si <= ti).astype(BF16), preferred_element_type=F32)
    prefix = pre[0:2 * H] + pre[2 * H:4 * H] + pre[4 * H:6 * H]
    gtot = jnp.sum(ls, axis=1, keepdims=True)
    b = jnp.where(is_fwd, prefix, gtot - prefix + ls)
    r = gi - b
    cm = r
    sh = 1
    while sh < L:
        from_left = jnp.where(lane >= sh, pltpu.roll(cm, sh, 1), -jnp.inf)
        from_right = jnp.where(lane < L - sh, pltpu.roll(cm, L - sh, 1), -jnp.inf)
        cm = jnp.maximum(cm, jnp.where(is_fwd, from_left, from_right))
        sh *= 2
    top = jnp.maximum(m_prev, cm)
    a = gtot - b + gi
    m_loc = jnp.max(a, axis=1, keepdims=True)
    m_new = jnp.maximum(gtot + m_prev, m_loc)
    rows = jnp.concatenate([
        -top,
        jnp.exp(m_prev - top),
        jnp.exp(-(b + top)),
        r,
        jnp.exp(a - m_loc),
        jnp.exp(gtot + m_prev - m_new),
        jnp.exp(m_loc - m_new) + jnp.zeros_like(r),
    ], axis=0)
    return rows, m_new


def _mlstm_kernel(kf_ref, qtf_ref, vtf_ref, gtf_ref, gtfn_ref, kb_ref, qtb_ref, vtb_ref, gtb_ref,
                  gtbn_ref, hf_ref, hb_ref, c_sc, m_sc, rows_sc):
    L = ML_CHUNK
    H = ML_HEADS
    P = ML_PAIRS
    c = pl.program_id(1)
    slot = c % 2

    def tok(d, u):
        lo = u * L if d == 0 else (ML_SUB - 1 - u) * L
        return slice(lo, lo + L)

    def gate_rows(gf_ref, gb_ref, u, m_prev):
        return _ml_gate_rows(gf_ref[:, tok(0, u)], gb_ref[:, tok(1, u)], m_prev)

    @pl.when(c == 0)
    def _():
        c_sc[...] = jnp.zeros_like(c_sc)
        m = jnp.full((P, L), M_INIT, F32)
        for u in range(ML_SUB):
            rows0, m = gate_rows(gtf_ref, gtb_ref, u, m)
            rows_sc[0, u] = rows0
        m_sc[...] = m

    rows_now = [rows_sc[slot, u] for u in range(ML_SUB)]
    m = m_sc[...]
    for u in range(ML_SUB):
        rows_next, m = gate_rows(gtfn_ref, gtbn_ref, u, m)
        rows_sc[1 - slot, u] = rows_next
    m_sc[...] = m

    si = lax.broadcasted_iota(jnp.int32, (L, L), 0)
    ti = lax.broadcasted_iota(jnp.int32, (L, L), 1)
    masks = (si <= ti, si >= ti)
    n_row = lax.broadcasted_iota(jnp.int32, (BF16_ROWS, 1), 0) == 0
    refs = ((kf_ref, qtf_ref, vtf_ref, hf_ref), (kb_ref, qtb_ref, vtb_ref, hb_ref))
    pairs = [(d, j) for d in range(2) for j in range(H)]

    for u in range(ML_SUB):
        rows = rows_now[u]
        bm, w_inter, clamp, r, w_end, s_prev, s_loc = [rows[i * P:(i + 1) * P] for i in range(ML_ROW_KINDS)]
        r_cols = _rows_to_cols(r)
        w_end_cols = _rows_to_cols(w_end)

        def k_of(d, j):
            return refs[d][0][tok(d, u), j * ML_QK_DIM:(j + 1) * ML_QK_DIM]

        def qt_of(d, j):
            return refs[d][1][j * ML_QK_DIM:(j + 1) * ML_QK_DIM, tok(d, u)]

        scores = []
        carried = []
        for d, j in pairs:
            p = d * H + j
            scores.append(jnp.dot(k_of(d, j), qt_of(d, j), preferred_element_type=F32))
            qtw = (qt_of(d, j).astype(F32) * w_inter[p:p + 1, :]).astype(BF16)
            carried.append(jnp.dot(c_sc[p].astype(BF16), qtw, preferred_element_type=F32))
        dens = []
        n_locs = []
        rhss = []
        for d, j in pairs:
            p = d * H + j
            arg = jnp.broadcast_to(r_cols[:, p:p + 1], (L, L)) + bm[p:p + 1, :]
            sc_t = scores[p] * jnp.exp(jnp.where(masks[d], arg, -jnp.inf))
            dens.append(jnp.sum(sc_t, axis=0, keepdims=True))
            kw = k_of(d, j).astype(F32) * jnp.broadcast_to(w_end_cols[:, p:p + 1], (L, ML_QK_DIM))
            n_locs.append(jnp.sum(kw, axis=0, keepdims=True))
            rhss.append(jnp.concatenate([sc_t.astype(BF16), kw.astype(BF16)], axis=1))
        boths = []
        for d, j in pairs:
            p = d * H + j
            vt_j = refs[d][2][j * ML_V_DIM:(j + 1) * ML_V_DIM, tok(d, u)]
            boths.append(jnp.dot(vt_j, rhss[p], preferred_element_type=F32))
        for d, j in pairs:
            p = d * H + j
            num = boths[p][:, :L] + carried[p][:ML_V_DIM]
            den = dens[p] + carried[p][ML_V_DIM:ML_V_DIM + 1]
            refs[d][3][j * ML_V_DIM:(j + 1) * ML_V_DIM, tok(d, u)] = (
                num / jnp.maximum(jnp.abs(den), clamp[p:p + 1, :])).astype(BF16)
            c_prev = c_sc[p]
            sp = s_prev[p:p + 1, :]
            sl = s_loc[p:p + 1, :]
            c_sc[p, 0:ML_V_DIM, :] = sp * c_prev[:ML_V_DIM] + sl * boths[p][:, L:]
            c_sc[p, ML_V_DIM:, :] = sp * c_prev[ML_V_DIM:] + sl * jnp.where(n_row, n_locs[p], 0.0)


def _mlstm(k_conv, qt_conv, projt, gates_t, batch, seq):
    t = k_conv.shape[0]
    blk = ML_SUB * ML_CHUNK
    nc = seq // blk
    fwd = lambda b, c: b * nc + c
    bwd = lambda b, c: b * nc + nc - 1 - c
    nxt = lambda c: jnp.minimum(c + 1, nc - 1)

    def dir_specs(ch):
        return [
            pl.BlockSpec((blk, ML_QK_COLS), lambda b, c: (ch(b, c), 0)),
            pl.BlockSpec((ML_QK_COLS, blk), lambda b, c: (0, ch(b, c))),
            pl.BlockSpec((ML_V_COLS, blk), lambda b, c: (R_MV // ML_V_COLS, ch(b, c))),
            pl.BlockSpec((GATE_COLS, blk), lambda b, c: (0, ch(b, c))),
            pl.BlockSpec((GATE_COLS, blk), lambda b, c: (0, ch(b, nxt(c)))),
        ]

    return pl.pallas_call(
        _mlstm_kernel,
        grid=(batch, nc),
        in_specs=dir_specs(fwd) + dir_specs(bwd),
        out_specs=[pl.BlockSpec((ML_V_COLS, blk), lambda b, c: (0, fwd(b, c))),
                   pl.BlockSpec((ML_V_COLS, blk), lambda b, c: (0, bwd(b, c)))],
        out_shape=[jax.ShapeDtypeStruct((ML_V_COLS, t), BF16),
                   jax.ShapeDtypeStruct((ML_V_COLS, t), BF16)],
        scratch_shapes=[pltpu.VMEM((ML_PAIRS, ML_STATE_ROWS, ML_QK_DIM), F32),
                        pltpu.VMEM((ML_PAIRS, ML_CHUNK), F32),
                        pltpu.VMEM((2, ML_SUB, ML_ROW_KINDS * ML_PAIRS, ML_CHUNK), F32)],
        compiler_params=pltpu.CompilerParams(
            dimension_semantics=("parallel", "arbitrary"), vmem_limit_bytes=VMEM_LIMIT),
        name="mlstm",
    )(k_conv, qt_conv, projt, gates_t, gates_t, k_conv, qt_conv, projt, gates_t, gates_t)


def _mix_kernel(att_ref, hft_ref, hbt_ref, mot_ref, x_ref, w_ref, gml_ref, gpost_ref, o_ref, cat_ref):
    cat_ref[:, 0:ATT_Q_COLS] = att_ref[...]
    for j in range(ML_HEADS):
        sl = slice(j * ML_V_DIM, (j + 1) * ML_V_DIM)
        h = _rms(hft_ref[sl, :].astype(F32) + hbt_ref[sl, :].astype(F32), axis=0)
        gain = jnp.concatenate([gml_ref[sl, :]] * (MIX_TM // LANES), axis=1)
        mem = h * gain * _sigmoid(mot_ref[sl, :].astype(F32))
        cat_ref[:, ATT_Q_COLS + j * ML_V_DIM:ATT_Q_COLS + (j + 1) * ML_V_DIM] = mem.T.astype(BF16)
    mix = jnp.dot(cat_ref[...], w_ref[...], preferred_element_type=F32)
    o_ref[...] = x_ref[...] + _rms(mix) * gpost_ref[...]


def _mix(att, hft, hbt, projt, x2d, w_out_bf, g_ml_b, g_post):
    t = x2d.shape[0]
    steps = t // MIX_TM
    row = lambda i: (i, 0)
    col = lambda i: (0, i)
    const = lambda i: (0, 0)
    return pl.pallas_call(
        _mix_kernel,
        grid=(steps,),
        in_specs=[
            pl.BlockSpec((MIX_TM, ATT_Q_COLS), row),
            pl.BlockSpec((ML_V_COLS, MIX_TM), col),
            pl.BlockSpec((ML_V_COLS, MIX_TM), col),
            pl.BlockSpec((ML_V_COLS, MIX_TM), lambda i: (R_MO // ML_V_COLS, i)),
            pl.BlockSpec((MIX_TM, D_MODEL), row),
            pl.BlockSpec((D_MODEL, D_MODEL), const, pipeline_mode=pl.Buffered(1)),
            pl.BlockSpec((ML_V_COLS, LANES), const),
            pl.BlockSpec((1, D_MODEL), const),
        ],
        out_specs=pl.BlockSpec((MIX_TM, D_MODEL), row),
        out_shape=jax.ShapeDtypeStruct((t, D_MODEL), F32),
        scratch_shapes=[pltpu.VMEM((MIX_TM, D_MODEL), BF16)],
        compiler_params=pltpu.CompilerParams(
            dimension_semantics=("parallel",), vmem_limit_bytes=VMEM_LIMIT),
        name="mix",
    )(att, hft, hbt, projt, x2d, w_out_bf, g_ml_b, g_post)


def _pin_before_next_load(operand_ref, value):
    bits = pltpu.bitcast(value, jnp.uint32)
    tiles = [bits[r:r + SUBLANES, c:c + LANES]
             for r in range(0, value.shape[0], SUBLANES) for c in range(0, value.shape[1], LANES)]
    folded = functools.reduce(jnp.bitwise_or, tiles)
    half = jnp.uint32(16)
    zero = pltpu.bitcast(lax.shift_right_logical(lax.shift_right_logical(folded, half), half), F32)
    zero = jnp.concatenate([zero] * (BF16_ROWS // SUBLANES), axis=0).astype(BF16)
    operand_ref[0:BF16_ROWS, 0:LANES] = operand_ref[0:BF16_ROWS, 0:LANES] + zero


def _mlp_kernel(xn_ref, xp_ref, gpre_ref, wup_ref, wdown_ref, gpost_ref, o_ref,
                h_a, h_b, acc_a, acc_b, *, n_blocks):
    blk = pl.program_id(0) - 1
    j = pl.program_id(1)
    rows = pl.ds(pl.multiple_of(j * MLP_SUB, MLP_SUB), MLP_SUB)

    def pre_norm(h_next):
        h = _rms(xn_ref[...]) * gpre_ref[...]
        h_next[rows, :] = h.astype(BF16)
        return h

    def finish(acc_prev):
        y = xp_ref[...] + _rms(acc_prev[rows, :]) * gpost_ref[...]
        o_ref[...] = y
        return y

    def main(h_cur, h_next, acc_cur, acc_prev):
        n_chunks = MLP_TF // MLP_UP_CHUNK
        us = []
        for c in range(n_chunks):
            cols = slice(c * MLP_UP_CHUNK, (c + 1) * MLP_UP_CHUNK)
            u = jnp.dot(h_cur[...], wup_ref[:, cols], preferred_element_type=F32)
            us.append(jnp.square(jnp.maximum(u, 0.0)).astype(BF16))
            if c == 0:
                _pin_before_next_load(h_cur, finish(acc_prev))
            elif c == 1:
                _pin_before_next_load(h_cur, pre_norm(h_next))
        part = jnp.dot(jnp.concatenate(us, axis=1), wdown_ref[...], preferred_element_type=F32)
        acc_cur[...] = jnp.where(j > 0, acc_cur[...], 0.0) + part

    @pl.when(blk < 0)
    def _():
        pre_norm(h_a)
        acc_a[rows, :] = jnp.zeros((MLP_SUB, D_MODEL), F32)
        acc_b[rows, :] = jnp.zeros((MLP_SUB, D_MODEL), F32)

    in_range = (blk >= 0) & (blk < n_blocks)
    even = (blk % 2) == 0

    @pl.when(in_range & even)
    def _():
        main(h_a, h_b, acc_a, acc_b)

    @pl.when(in_range & jnp.logical_not(even))
    def _():
        main(h_b, h_a, acc_b, acc_a)

    @pl.when(blk == n_blocks)
    def _():
        finish(acc_a if (n_blocks - 1) % 2 == 0 else acc_b)


def _mlp(x2d, g_pre, w_up, w_down, g_post):
    t = x2d.shape[0]
    ni = t // MLP_TM
    nj = D_FF // MLP_TF
    clamp = lambda blk: jnp.clip(blk, 0, ni - 1)
    sub_row = lambda blk, j: (clamp(blk) * nj + j, 0)
    wj = lambda i, j: jnp.where(i < 1, 0, jnp.where(i > ni, nj - 1, j))
    return pl.pallas_call(
        functools.partial(_mlp_kernel, n_blocks=ni),
        grid=(ni + 2, nj),
        in_specs=[
            pl.BlockSpec((MLP_SUB, D_MODEL), lambda i, j: sub_row(i, j)),
            pl.BlockSpec((MLP_SUB, D_MODEL), lambda i, j: sub_row(i - 2, j)),
            pl.BlockSpec((1, D_MODEL), lambda i, j: (0, 0)),
            pl.BlockSpec((D_MODEL, MLP_TF), lambda i, j: (0, wj(i, j))),
            pl.BlockSpec((MLP_TF, D_MODEL), lambda i, j: (wj(i, j), 0)),
            pl.BlockSpec((1, D_MODEL), lambda i, j: (0, 0)),
        ],
        out_specs=pl.BlockSpec((MLP_SUB, D_MODEL), lambda i, j: (jnp.where(i < 2, 0, (i - 2) * nj + j), 0)),
        out_shape=jax.ShapeDtypeStruct((t, D_MODEL), F32),
        scratch_shapes=[pltpu.VMEM((MLP_TM, D_MODEL), BF16), pltpu.VMEM((MLP_TM, D_MODEL), BF16),
                        pltpu.VMEM((MLP_TM, D_MODEL), F32), pltpu.VMEM((MLP_TM, D_MODEL), F32)],
        compiler_params=pltpu.CompilerParams(
            dimension_semantics=("arbitrary", "arbitrary"), vmem_limit_bytes=BIG_VMEM_LIMIT),
        name="mlp",
    )(x2d, x2d, g_pre, w_up, w_down, g_post)


def _rope_tables(seq):
    half = HEAD_DIM // 2
    inv_freq = ROPE_THETA ** (-np.arange(half, dtype=np.float64) / half)
    ang = np.arange(seq, dtype=np.float64)[:, None] * inv_freq[None, :]
    cos = np.cos(ang)
    sin = np.sin(ang)
    cos_t = np.concatenate([cos, cos], axis=1).astype(np.float32)
    sin_t = np.concatenate([-sin, sin], axis=1).astype(np.float32)
    return jnp.asarray(cos_t), jnp.asarray(sin_t)


def kernel(x, w_in, conv_w, gate_bias, ml_norm_g, attn_sink, w_out, g_pre_mix, g_post_mix,
           g_pre_mlp, g_post_mlp, w_up, w_down):
    batch, seq, d = x.shape
    depth = w_in.shape[0]
    cos_t, sin_t = _rope_tables(seq)
    x2d = x.reshape(batch * seq, d)
    w_in_bf = jnp.swapaxes(w_in, 1, 2).astype(BF16)
    for l in range(depth):
        proj, projt, gates_t, w_up_bf, w_out_bf, w_down_bf = _in_proj(
            x2d, g_pre_mix[l][None, :], w_in_bf, gate_bias[l][:, None], cos_t, sin_t,
            w_up, w_out, w_down, l, seq)
        k_conv, qt_conv = _qk_conv(proj, conv_w[l], seq)
        att = _attention(proj, attn_sink[l], batch, seq)
        hft, hbt = _mlstm(k_conv, qt_conv, projt, gates_t, batch, seq)
        g_ml_b = jnp.broadcast_to(ml_norm_g[l][:, None], (ML_V_COLS, LANES))
        x2d = _mix(att, hft, hbt, projt, x2d, w_out_bf, g_ml_b, g_post_mix[l][None, :])
        x2d = _mlp(x2d, g_pre_mlp[l][None, :], w_up_bf, w_down_bf, g_post_mlp[l][None, :])
    return x2d.reshape(batch, seq, d)
```

```python
import functools

import jax
import jax.numpy as jnp
import numpy as np
from jax import lax
from jax.experimental import pallas as pl
from jax.experimental.pallas import tpu as pltpu

F32 = jnp.float32
BF16 = jnp.bfloat16

D_MODEL = 2048
ATT_HEADS = 8
ATT_KV_HEADS = 2
ATT_GROUP = ATT_HEADS // ATT_KV_HEADS
HEAD_DIM = 128
ATT_WINDOW = 128
ROPE_THETA = 10000.0
ML_HEADS = 4
ML_V_DIM = 256
ML_QK_DIM = 128
ML_CHUNK = 128
M_INIT = -1e30
ATT_MASKED = -1e30
LOG2_E = 1.4426950408889634
D_FF = 4 * D_MODEL
NORM_EPS = 1e-6
GATE_COLS = 4 * ML_HEADS

ATT_Q_COLS = ATT_HEADS * HEAD_DIM
ATT_KV_COLS = ATT_KV_HEADS * HEAD_DIM
ML_QK_COLS = ML_HEADS * ML_QK_DIM
ML_V_COLS = ML_HEADS * ML_V_DIM

S_AK = ATT_Q_COLS
S_AV = S_AK + ATT_KV_COLS
S_MQ = S_AV + ATT_KV_COLS
S_MV = S_MQ + 2 * ML_QK_COLS
S_GATE = S_MV + 2 * ML_V_COLS
IN_COLS = S_GATE + GATE_COLS
C_AQ = 0
C_MQK = C_AQ + ATT_Q_COLS
C_AK = C_MQK + 2 * ML_QK_COLS
C_AV = C_AK + ATT_KV_COLS
PROJ_COLS = C_AV + ATT_KV_COLS
R_MV = 0
R_MO = R_MV + ML_V_COLS
PROJT_ROWS = R_MO + ML_V_COLS
LANES = 128
SUBLANES = 8
BF16_ROWS = 16

VMEM_LIMIT = 56 * 1024 * 1024
BIG_VMEM_LIMIT = 60 * 1024 * 1024

IN_TM = 512
IN_TN = 512
CONV_TM = 1024
CONV_SUB = 256
ATT_TQ = 2048
MIX_TM = 512
MLP_TM = 512
MLP_TF = 2048
MLP_UP_CHUNK = 512
MLP_SUB = MLP_TM // (D_FF // MLP_TF)
ML_STATE_ROWS = ML_V_DIM + BF16_ROWS
ML_PAIRS = 2 * ML_HEADS
ML_ROW_KINDS = 7
ML_SUB = 4

_NT = (((1,), (1,)), ((), ()))
_TN = (((0,), (0,)), ((), ()))


def _sigmoid(x):
    return 1.0 / (1.0 + jnp.exp2(x * (-LOG2_E)))


def _log_sigmoid(x):
    return jnp.minimum(x, 0.0) - jnp.log(1.0 + jnp.exp(-jnp.abs(x)))


def _rms(x, axis=-1):
    return x * lax.rsqrt(jnp.mean(x * x, axis=axis, keepdims=True) + NORM_EPS)


_ROW_MAJOR_CHUNKS = ((0, C_AQ), (IN_TN, C_AQ + IN_TN), (S_AK, C_AK), (S_MQ, C_MQK),
                     (S_MQ + IN_TN, C_MQK + IN_TN))


def _in_proj_kernel(x_ref, g_ref, wt_ref, gbias_ref, cos_ref, sin_ref, wup_ref, wout_ref, wdown_ref,
                    proj_ref, projt_ref, gatet_ref, wup_bf_ref, wout_bf_ref, wdown_bf_ref, h_ref):
    wup_bf_ref[...] = wup_ref[...].astype(BF16)
    wout_bf_ref[...] = wout_ref[...].astype(BF16)
    wdown_bf_ref[...] = wdown_ref[...].astype(BF16)
    h_ref[...] = (_rms(x_ref[...]) * g_ref[...]).astype(BF16)
    cos = cos_ref[...]
    sin = sin_ref[...]

    def rope(a):
        return a * cos + pltpu.roll(a, HEAD_DIM // 2, 1) * sin

    q_scale = HEAD_DIM ** -0.5 * LOG2_E
    for src, dst in _ROW_MAJOR_CHUNKS:
        acc = lax.dot_general(h_ref[...], wt_ref[src:src + IN_TN, :], _NT, preferred_element_type=F32)
        for k in range(IN_TN // HEAD_DIM):
            col = src + k * HEAD_DIM
            a = acc[:, k * HEAD_DIM:(k + 1) * HEAD_DIM]
            if col < S_AK:
                a = rope(a) * q_scale
            elif col < S_AV:
                a = rope(a)
            proj_ref[:, dst + k * HEAD_DIM:dst + (k + 1) * HEAD_DIM] = a.astype(BF16)
    n_chunks = PROJT_ROWS // IN_TN
    for c in range(n_chunks):
        r0 = S_MV + c * IN_TN
        rows = IN_TN + (GATE_COLS if c == n_chunks - 1 else 0)
        acc = lax.dot_general(wt_ref[r0:r0 + rows, :], h_ref[...], _NT, preferred_element_type=F32)
        projt_ref[c * IN_TN:(c + 1) * IN_TN, :] = acc[:IN_TN].astype(BF16)
        if c == n_chunks - 1:
            gatet_ref[...] = acc[IN_TN:] + gbias_ref[...]


def _in_proj(x2d, g, wt_bf_all, gate_bias_col, cos_t, sin_t, w_up_all, w_out_all, w_down_all, layer, seq):
    t = x2d.shape[0]
    steps = t // IN_TM
    pos_blocks = seq // IN_TM
    up_rows = D_MODEL // steps
    down_rows = D_FF // steps
    const = lambda i: (0, 0)
    row = lambda i: (i, 0)
    layer_row = lambda i: (layer, i, 0)
    return pl.pallas_call(
        _in_proj_kernel,
        grid=(steps,),
        in_specs=[
            pl.BlockSpec((IN_TM, D_MODEL), row),
            pl.BlockSpec((1, D_MODEL), const),
            pl.BlockSpec((None, IN_COLS, D_MODEL), lambda i: (layer, 0, 0), pipeline_mode=pl.Buffered(1)),
            pl.BlockSpec((GATE_COLS, 1), const),
            pl.BlockSpec((IN_TM, HEAD_DIM), lambda i: (i % pos_blocks, 0)),
            pl.BlockSpec((IN_TM, HEAD_DIM), lambda i: (i % pos_blocks, 0)),
            pl.BlockSpec((None, up_rows, D_FF), layer_row),
            pl.BlockSpec((None, up_rows, D_MODEL), layer_row),
            pl.BlockSpec((None, down_rows, D_MODEL), layer_row),
        ],
        out_specs=[
            pl.BlockSpec((IN_TM, PROJ_COLS), row),
            pl.BlockSpec((PROJT_ROWS, IN_TM), lambda i: (0, i)),
            pl.BlockSpec((GATE_COLS, IN_TM), lambda i: (0, i)),
            pl.BlockSpec((up_rows, D_FF), row),
            pl.BlockSpec((up_rows, D_MODEL), row),
            pl.BlockSpec((down_rows, D_MODEL), row),
        ],
        out_shape=[
            jax.ShapeDtypeStruct((t, PROJ_COLS), BF16),
            jax.ShapeDtypeStruct((PROJT_ROWS, t), BF16),
            jax.ShapeDtypeStruct((GATE_COLS, t), F32),
            jax.ShapeDtypeStruct((D_MODEL, D_FF), BF16),
            jax.ShapeDtypeStruct((D_MODEL, D_MODEL), BF16),
            jax.ShapeDtypeStruct((D_FF, D_MODEL), BF16),
        ],
        scratch_shapes=[pltpu.VMEM((IN_TM, D_MODEL), BF16)],
        compiler_params=pltpu.CompilerParams(
            dimension_semantics=("parallel",), vmem_limit_bytes=BIG_VMEM_LIMIT),
        name="in_proj",
    )(x2d, g, wt_bf_all, gate_bias_col, cos_t, sin_t, w_up_all, w_out_all, w_down_all)


def _qk_conv_kernel(x_ref, xp_ref, xn_ref, w_ref, k_ref, qt_ref, *, seq_blocks):
    n = CONV_SUB
    pos = pl.program_id(0) % seq_blocks
    has_prev = (pos > 0).astype(F32)
    has_next = (pos < seq_blocks - 1).astype(F32)
    ri = lax.broadcasted_iota(jnp.int32, (n, n), 0)
    ci = lax.broadcasted_iota(jnp.int32, (n, n), 1)
    shift_prev = (ci == ri - 1).astype(BF16)
    shift_next = (ci == ri + 1).astype(BF16)
    rowid = lax.broadcasted_iota(jnp.int32, (n, 1), 0)
    w0 = w_ref[0:1, :]
    w1 = w_ref[1:2, :]
    w2 = w_ref[2:3, :]
    n_sub = CONV_TM // n
    for sb in range(n_sub):
        xs = x_ref[sb * n:(sb + 1) * n, :]
        x_prev = jnp.dot(shift_prev, xs, preferred_element_type=F32)
        x_next = jnp.dot(shift_next, xs, preferred_element_type=F32)
        if sb == 0:
            prev_row = xp_ref[BF16_ROWS - 1:BF16_ROWS, :].astype(F32) * has_prev
        else:
            prev_row = x_ref[sb * n - BF16_ROWS:sb * n, :].astype(F32)[BF16_ROWS - 1:BF16_ROWS]
        if sb == n_sub - 1:
            next_row = xn_ref[0:1, :].astype(F32) * has_next
        else:
            next_row = x_ref[(sb + 1) * n:(sb + 1) * n + BF16_ROWS, :].astype(F32)[0:1]
        x_prev = jnp.where(rowid == 0, prev_row, x_prev)
        x_next = jnp.where(rowid == n - 1, next_row, x_next)
        y = x_prev * w0 + xs.astype(F32) * w1 + x_next * w2
        y = y * _sigmoid(y)
        k_ref[sb * n:(sb + 1) * n, :] = y[:, ML_QK_COLS:].astype(BF16)
        q = y[:, :ML_QK_COLS] * (ML_QK_DIM ** -0.5)
        qt_ref[:, sb * n:(sb + 1) * n] = q.T.astype(BF16)


def _qk_conv(proj, conv_w, seq):
    t = proj.shape[0]
    width = 2 * ML_QK_COLS
    col = C_MQK // width
    per = CONV_TM // BF16_ROWS
    last = t // BF16_ROWS - 1
    return pl.pallas_call(
        functools.partial(_qk_conv_kernel, seq_blocks=seq // CONV_TM),
        grid=(t // CONV_TM,),
        in_specs=[
            pl.BlockSpec((CONV_TM, width), lambda i: (i, col)),
            pl.BlockSpec((BF16_ROWS, width), lambda i: (jnp.maximum(i * per - 1, 0), col)),
            pl.BlockSpec((BF16_ROWS, width), lambda i: (jnp.minimum((i + 1) * per, last), col)),
            pl.BlockSpec((3, width), lambda i: (0, 0)),
        ],
        out_specs=[pl.BlockSpec((CONV_TM, ML_QK_COLS), lambda i: (i, 0)),
                   pl.BlockSpec((ML_QK_COLS, CONV_TM), lambda i: (0, i))],
        out_shape=[jax.ShapeDtypeStruct((t, ML_QK_COLS), BF16),
                   jax.ShapeDtypeStruct((ML_QK_COLS, t), BF16)],
        compiler_params=pltpu.CompilerParams(
            dimension_semantics=("parallel",), vmem_limit_bytes=VMEM_LIMIT),
        name="qk_conv",
    )(proj, proj, proj, conv_w)


def _attn_kernel(sink_ref, q_ref, kc_ref, kp_ref, kn_ref, vc_ref, vp_ref, vn_ref,
                 o_ref, kbuf, vbuf, *, n_blocks):
    w = ATT_WINDOW
    kbuf[0:w] = kp_ref[...]
    kbuf[w:w + ATT_TQ] = kc_ref[...]
    kbuf[w + ATT_TQ:] = kn_ref[...]
    for h in range(ATT_KV_HEADS):
        hs = slice(h * HEAD_DIM, (h + 1) * HEAD_DIM)
        vbuf[h, 0:w, 0:HEAD_DIM] = vp_ref[:, hs]
        vbuf[h, w:w + ATT_TQ, 0:HEAD_DIM] = vc_ref[:, hs]
        vbuf[h, w + ATT_TQ:, 0:HEAD_DIM] = vn_ref[:, hs]
        vbuf[h, :, HEAD_DIM:] = jnp.ones((ATT_TQ + 2 * w, HEAD_DIM), BF16)
    i = pl.program_id(1)
    rows = ATT_GROUP * w
    qi = lax.broadcasted_iota(jnp.int32, (rows, 1), 0) & (w - 1)
    blk = lax.broadcasted_iota(jnp.int32, (rows, 1), 0) // w
    kj = lax.broadcasted_iota(jnp.int32, (rows, 3 * w), 1)
    kj_row = lax.broadcasted_iota(jnp.int32, (1, 3 * w), 1)
    band = jnp.where((kj >= qi) & (kj <= qi + 2 * w), 0.0, ATT_MASKED)
    n_win = ATT_TQ // w
    sinks = []
    for h in range(ATT_KV_HEADS):
        sink = jnp.full((rows, 1), sink_ref[h * ATT_GROUP], F32)
        for g in range(1, ATT_GROUP):
            sink = jnp.where(blk == g, sink_ref[h * ATT_GROUP + g], sink)
        sinks.append(sink * LOG2_E)
    tiles = [(n, h) for n in range(n_win) for h in range(ATT_KV_HEADS)]
    scores = []
    for n, h in tiles:
        qs = jnp.concatenate(
            [q_ref[n * w:(n + 1) * w, (h * ATT_GROUP + g) * HEAD_DIM:(h * ATT_GROUP + g + 1) * HEAD_DIM]
             for g in range(ATT_GROUP)], axis=0)
        kw = kbuf[n * w:(n + 3) * w, h * HEAD_DIM:(h + 1) * HEAD_DIM]
        scores.append(lax.dot_general(qs, kw, _NT, preferred_element_type=F32))
    probs = []
    maxes = []
    for idx, (n, h) in enumerate(tiles):
        bias = band
        if n == 0:
            bias = bias + jnp.where(kj_row < w, jnp.where(i == 0, ATT_MASKED, 0.0), 0.0)
        if n == n_win - 1:
            bias = bias + jnp.where(kj_row >= 2 * w, jnp.where(i == n_blocks - 1, ATT_MASKED, 0.0), 0.0)
        s = scores[idx] + bias
        m = jnp.maximum(jnp.max(s, axis=-1, keepdims=True), sinks[h])
        maxes.append(m)
        probs.append(jnp.exp2(s - m).astype(BF16))
    for idx, (n, h) in enumerate(tiles):
        o_aug = jnp.dot(probs[idx], vbuf[h, n * w:(n + 3) * w, :], preferred_element_type=F32)
        denom = o_aug[:, HEAD_DIM:] + jnp.exp2(sinks[h] - maxes[idx])
        o = o_aug[:, :HEAD_DIM] * (1.0 / denom)
        for g in range(ATT_GROUP):
            col = (h * ATT_GROUP + g) * HEAD_DIM
            o_ref[n * w:(n + 1) * w, col:col + HEAD_DIM] = o[g * w:(g + 1) * w].astype(BF16)


def _attention(proj, sink, batch, seq):
    t = proj.shape[0]
    w = ATT_WINDOW
    nq = seq // ATT_TQ
    per = ATT_TQ // w
    last_blk = t // w - 1
    cur = lambda b, i: b * nq + i
    prev = lambda b, i: jnp.maximum((b * nq + i) * per - 1, 0)
    nxt = lambda b, i: jnp.minimum((b * nq + i + 1) * per, last_blk)
    kcol = C_AK // ATT_KV_COLS
    vcol = C_AV // ATT_KV_COLS
    return pl.pallas_call(
        functools.partial(_attn_kernel, n_blocks=nq),
        grid=(batch, nq),
        in_specs=[
            pl.BlockSpec(memory_space=pltpu.SMEM),
            pl.BlockSpec((ATT_TQ, ATT_Q_COLS), lambda b, i: (cur(b, i), C_AQ // ATT_Q_COLS)),
            pl.BlockSpec((ATT_TQ, ATT_KV_COLS), lambda b, i: (cur(b, i), kcol)),
            pl.BlockSpec((w, ATT_KV_COLS), lambda b, i: (prev(b, i), kcol)),
            pl.BlockSpec((w, ATT_KV_COLS), lambda b, i: (nxt(b, i), kcol)),
            pl.BlockSpec((ATT_TQ, ATT_KV_COLS), lambda b, i: (cur(b, i), vcol)),
            pl.BlockSpec((w, ATT_KV_COLS), lambda b, i: (prev(b, i), vcol)),
            pl.BlockSpec((w, ATT_KV_COLS), lambda b, i: (nxt(b, i), vcol)),
        ],
        out_specs=pl.BlockSpec((ATT_TQ, ATT_Q_COLS), lambda b, i: (cur(b, i), 0)),
        out_shape=jax.ShapeDtypeStruct((t, ATT_Q_COLS), BF16),
        scratch_shapes=[pltpu.VMEM((ATT_TQ + 2 * w, ATT_KV_COLS), BF16),
                        pltpu.VMEM((ATT_KV_HEADS, ATT_TQ + 2 * w, 2 * HEAD_DIM), BF16)],
        compiler_params=pltpu.CompilerParams(
            dimension_semantics=("parallel", "parallel"), vmem_limit_bytes=VMEM_LIMIT),
        name="attn",
    )(sink, proj, proj, proj, proj, proj, proj, proj)


def _rows_to_cols(x):
    length = x.shape[1]
    padded = jnp.concatenate([x, jnp.zeros((length - x.shape[0], length), x.dtype)], axis=0)
    return padded.T


def _ml_gate_rows(gates_f, gates_b, m_prev):
    L = ML_CHUNK
    H = ML_HEADS
    row8 = lax.broadcasted_iota(jnp.int32, (2 * H, L), 0)
    lane = lax.broadcasted_iota(jnp.int32, (2 * H, L), 1)
    is_fwd = row8 < H
    fwd_if = gates_f[0:2 * H]
    bwd_if = gates_b[2 * H:4 * H]
    gi = jnp.where(is_fwd, fwd_if, pltpu.roll(bwd_if, H, 0))
    gf = jnp.where(is_fwd, pltpu.roll(fwd_if, H, 0), bwd_if)
    ls = _log_sigmoid(gf)
    hi = ls.astype(BF16).astype(F32)
    rem = ls - hi
    mid = rem.astype(BF16).astype(F32)
    lo = rem - mid
    parts = jnp.concatenate([hi, mid, lo, jnp.zeros_like(hi)], axis=0).astype(BF16)
    si = lax.broadcasted_iota(jnp.int32, (L, L), 0)
    ti = lax.broadcasted_iota(jnp.int32, (L, L), 1)
    pre = jnp.dot(parts, (si <= ti).astype(BF16), preferred_element_type=F32)
    prefix = pre[0:2 * H] + pre[2 * H:4 * H] + pre[4 * H:6 * H]
    gtot = jnp.sum(ls, axis=1, keepdims=True)
    b = jnp.where(is_fwd, prefix, gtot - prefix + ls)
    r = gi - b
    cm = r
    sh = 1
    while sh < L:
        from_left = jnp.where(lane >= sh, pltpu.roll(cm, sh, 1), -jnp.inf)
        from_right = jnp.where(lane < L - sh, pltpu.roll(cm, L - sh, 1), -jnp.inf)
        cm = jnp.maximum(cm, jnp.where(is_fwd, from_left, from_right))
        sh *= 2
    top = jnp.maximum(m_prev, cm)
    a = gtot - b + gi
    m_loc = jnp.max(a, axis=1, keepdims=True)
    m_new = jnp.maximum(gtot + m_prev, m_loc)
    rows = jnp.concatenate([
        -top,
        jnp.exp(m_prev - top),
        jnp.exp(-(b + top)),
        r,
        jnp.exp(a - m_loc),
        jnp.exp(gtot + m_prev - m_new),
        jnp.exp(m_loc - m_new) + jnp.zeros_like(r),
    ], axis=0)
    return rows, m_new


def _mlstm_kernel(kf_ref, qtf_ref, vtf_ref, gtf_ref, gtfn_ref, kb_ref, qtb_ref, vtb_ref, gtb_ref,
                  gtbn_ref, hf_ref, hb_ref, c_sc, m_sc, rows_sc):
    L = ML_CHUNK
    H = ML_HEADS
    P = ML_PAIRS
    c = pl.program_id(1)
    slot = c % 2

    def tok(d, u):
        lo = u * L if d == 0 else (ML_SUB - 1 - u) * L
        return slice(lo, lo + L)

    def gate_rows(gf_ref, gb_ref, u, m_prev):
        return _ml_gate_rows(gf_ref[:, tok(0, u)], gb_ref[:, tok(1, u)], m_prev)

    @pl.when(c == 0)
    def _():
        c_sc[...] = jnp.zeros_like(c_sc)
        m = jnp.full((P, L), M_INIT, F32)
        for u in range(ML_SUB):
            rows0, m = gate_rows(gtf_ref, gtb_ref, u, m)
            rows_sc[0, u] = rows0
        m_sc[...] = m

    rows_now = [rows_sc[slot, u] for u in range(ML_SUB)]
    m = m_sc[...]
    for u in range(ML_SUB):
        rows_next, m = gate_rows(gtfn_ref, gtbn_ref, u, m)
        rows_sc[1 - slot, u] = rows_next
    m_sc[...] = m

    si = lax.broadcasted_iota(jnp.int32, (L, L), 0)
    ti = lax.broadcasted_iota(jnp.int32, (L, L), 1)
    masks = (si <= ti, si >= ti)
    n_row = lax.broadcasted_iota(jnp.int32, (BF16_ROWS, 1), 0) == 0
    refs = ((kf_ref, qtf_ref, vtf_ref, hf_ref), (kb_ref, qtb_ref, vtb_ref, hb_ref))
    pairs = [(d, j) for d in range(2) for j in range(H)]

    for u in range(ML_SUB):
        rows = rows_now[u]
        bm, w_inter, clamp, r, w_end, s_prev, s_loc = [rows[i * P:(i + 1) * P] for i in range(ML_ROW_KINDS)]
        r_cols = _rows_to_cols(r)
        w_end_cols = _rows_to_cols(w_end)

        def k_of(d, j):
            return refs[d][0][tok(d, u), j * ML_QK_DIM:(j + 1) * ML_QK_DIM]

        def qt_of(d, j):
            return refs[d][1][j * ML_QK_DIM:(j + 1) * ML_QK_DIM, tok(d, u)]

        scores = []
        carried = []
        for d, j in pairs:
            p = d * H + j
            scores.append(jnp.dot(k_of(d, j), qt_of(d, j), preferred_element_type=F32))
            qtw = (qt_of(d, j).astype(F32) * w_inter[p:p + 1, :]).astype(BF16)
            carried.append(jnp.dot(c_sc[p].astype(BF16), qtw, preferred_element_type=F32))
        dens = []
        n_locs = []
        rhss = []
        for d, j in pairs:
            p = d * H + j
            arg = jnp.broadcast_to(r_cols[:, p:p + 1], (L, L)) + bm[p:p + 1, :]
            sc_t = scores[p] * jnp.exp(jnp.where(masks[d], arg, -jnp.inf))
            dens.append(jnp.sum(sc_t, axis=0, keepdims=True))
            kw = k_of(d, j).astype(F32) * jnp.broadcast_to(w_end_cols[:, p:p + 1], (L, ML_QK_DIM))
            n_locs.append(jnp.sum(kw, axis=0, keepdims=True))
            rhss.append(jnp.concatenate([sc_t.astype(BF16), kw.astype(BF16)], axis=1))
        boths = []
        for d, j in pairs:
            p = d * H + j
            vt_j = refs[d][2][j * ML_V_DIM:(j + 1) * ML_V_DIM, tok(d, u)]
            boths.append(jnp.dot(vt_j, rhss[p], preferred_element_type=F32))
        for d, j in pairs:
            p = d * H + j
            num = boths[p][:, :L] + carried[p][:ML_V_DIM]
            den = dens[p] + carried[p][ML_V_DIM:ML_V_DIM + 1]
            refs[d][3][j * ML_V_DIM:(j + 1) * ML_V_DIM, tok(d, u)] = (
                num / jnp.maximum(jnp.abs(den), clamp[p:p + 1, :])).astype(BF16)
            c_prev = c_sc[p]
            sp = s_prev[p:p + 1, :]
            sl = s_loc[p:p + 1, :]
            c_sc[p, 0:ML_V_DIM, :] = sp * c_prev[:ML_V_DIM] + sl * boths[p][:, L:]
            c_sc[p, ML_V_DIM:, :] = sp * c_prev[ML_V_DIM:] + sl * jnp.where(n_row, n_locs[p], 0.0)


def _mlstm(k_conv, qt_conv, projt, gates_t, batch, seq):
    t = k_conv.shape[0]
    blk = ML_SUB * ML_CHUNK
    nc = seq // blk
    fwd = lambda b, c: b * nc + c
    bwd = lambda b, c: b * nc + nc - 1 - c
    nxt = lambda c: jnp.minimum(c + 1, nc - 1)

    def dir_specs(ch):
        return [
            pl.BlockSpec((blk, ML_QK_COLS), lambda b, c: (ch(b, c), 0)),
            pl.BlockSpec((ML_QK_COLS, blk), lambda b, c: (0, ch(b, c))),
            pl.BlockSpec((ML_V_COLS, blk), lambda b, c: (R_MV // ML_V_COLS, ch(b, c))),
            pl.BlockSpec((GATE_COLS, blk), lambda b, c: (0, ch(b, c))),
            pl.BlockSpec((GATE_COLS, blk), lambda b, c: (0, ch(b, nxt(c)))),
        ]

    return pl.pallas_call(
        _mlstm_kernel,
        grid=(batch, nc),
        in_specs=dir_specs(fwd) + dir_specs(bwd),
        out_specs=[pl.BlockSpec((ML_V_COLS, blk), lambda b, c: (0, fwd(b, c))),
                   pl.BlockSpec((ML_V_COLS, blk), lambda b, c: (0, bwd(b, c)))],
        out_shape=[jax.ShapeDtypeStruct((ML_V_COLS, t), BF16),
                   jax.ShapeDtypeStruct((ML_V_COLS, t), BF16)],
        scratch_shapes=[pltpu.VMEM((ML_PAIRS, ML_STATE_ROWS, ML_QK_DIM), F32),
                        pltpu.VMEM((ML_PAIRS, ML_CHUNK), F32),
                        pltpu.VMEM((2, ML_SUB, ML_ROW_KINDS * ML_PAIRS, ML_CHUNK), F32)],
        compiler_params=pltpu.CompilerParams(
            dimension_semantics=("parallel", "arbitrary"), vmem_limit_bytes=VMEM_LIMIT),
        name="mlstm",
    )(k_conv, qt_conv, projt, gates_t, gates_t, k_conv, qt_conv, projt, gates_t, gates_t)


def _mix_kernel(att_ref, hft_ref, hbt_ref, mot_ref, x_ref, w_ref, gml_ref, gpost_ref, o_ref, cat_ref):
    cat_ref[:, 0:ATT_Q_COLS] = att_ref[...]
    for j in range(ML_HEADS):
        sl = slice(j * ML_V_DIM, (j + 1) * ML_V_DIM)
        h = _rms(hft_ref[sl, :].astype(F32) + hbt_ref[sl, :].astype(F32), axis=0)
        gain = jnp.concatenate([gml_ref[sl, :]] * (MIX_TM // LANES), axis=1)
        mem = h * gain * _sigmoid(mot_ref[sl, :].astype(F32))
        cat_ref[:, ATT_Q_COLS + j * ML_V_DIM:ATT_Q_COLS + (j + 1) * ML_V_DIM] = mem.T.astype(BF16)
    mix = jnp.dot(cat_ref[...], w_ref[...], preferred_element_type=F32)
    o_ref[...] = x_ref[...] + _rms(mix) * gpost_ref[...]


def _mix(att, hft, hbt, projt, x2d, w_out_bf, g_ml_b, g_post):
    t = x2d.shape[0]
    steps = t // MIX_TM
    row = lambda i: (i, 0)
    col = lambda i: (0, i)
    const = lambda i: (0, 0)
    return pl.pallas_call(
        _mix_kernel,
        grid=(steps,),
        in_specs=[
            pl.BlockSpec((MIX_TM, ATT_Q_COLS), row),
            pl.BlockSpec((ML_V_COLS, MIX_TM), col),
            pl.BlockSpec((ML_V_COLS, MIX_TM), col),
            pl.BlockSpec((ML_V_COLS, MIX_TM), lambda i: (R_MO // ML_V_COLS, i)),
            pl.BlockSpec((MIX_TM, D_MODEL), row),
            pl.BlockSpec((D_MODEL, D_MODEL), const, pipeline_mode=pl.Buffered(1)),
            pl.BlockSpec((ML_V_COLS, LANES), const),
            pl.BlockSpec((1, D_MODEL), const),
        ],
        out_specs=pl.BlockSpec((MIX_TM, D_MODEL), row),
        out_shape=jax.ShapeDtypeStruct((t, D_MODEL), F32),
        scratch_shapes=[pltpu.VMEM((MIX_TM, D_MODEL), BF16)],
        compiler_params=pltpu.CompilerParams(
            dimension_semantics=("parallel",), vmem_limit_bytes=VMEM_LIMIT),
        name="mix",
    )(att, hft, hbt, projt, x2d, w_out_bf, g_ml_b, g_post)


def _pin_before_next_load(operand_ref, value):
    bits = pltpu.bitcast(value, jnp.uint32)
    tiles = [bits[r:r + SUBLANES, c:c + LANES]
             for r in range(0, value.shape[0], SUBLANES) for c in range(0, value.shape[1], LANES)]
    folded = functools.reduce(jnp.bitwise_or, tiles)
    half = jnp.uint32(16)
    zero = pltpu.bitcast(lax.shift_right_logical(lax.shift_right_logical(folded, half), half), F32)
    zero = jnp.concatenate([zero] * (BF16_ROWS // SUBLANES), axis=0).astype(BF16)
    operand_ref[0:BF16_ROWS, 0:LANES] = operand_ref[0:BF16_ROWS, 0:LANES] + zero


def _mlp_kernel(xn_ref, xp_ref, gpre_ref, wup_ref, wdown_ref, gpost_ref, o_ref,
                h_a, h_b, acc_a, acc_b, *, n_blocks):
    blk = pl.program_id(0) - 1
    j = pl.program_id(1)
    rows = pl.ds(pl.multiple_of(j * MLP_SUB, MLP_SUB), MLP_SUB)

    def pre_norm(h_next):
        h = _rms(xn_ref[...]) * gpre_ref[...]
        h_next[rows, :] = h.astype(BF16)
        return h

    def finish(acc_prev):
        y = xp_ref[...] + _rms(acc_prev[rows, :]) * gpost_ref[...]
        o_ref[...] = y
        return y

    def main(h_cur, h_next, acc_cur, acc_prev):
        n_chunks = MLP_TF // MLP_UP_CHUNK
        us = []
        for c in range(n_chunks):
            cols = slice(c * MLP_UP_CHUNK, (c + 1) * MLP_UP_CHUNK)
            u = jnp.dot(h_cur[...], wup_ref[:, cols], preferred_element_type=F32)
            us.append(jnp.square(jnp.maximum(u, 0.0)).astype(BF16))
            if c == 0:
                _pin_before_next_load(h_cur, finish(acc_prev))
            elif c == 1:
                _pin_before_next_load(h_cur, pre_norm(h_next))
        part = jnp.dot(jnp.concatenate(us, axis=1), wdown_ref[...], preferred_element_type=F32)
        acc_cur[...] = jnp.where(j > 0, acc_cur[...], 0.0) + part

    @pl.when(blk < 0)
    def _():
        pre_norm(h_a)
        acc_a[rows, :] = jnp.zeros((MLP_SUB, D_MODEL), F32)
        acc_b[rows, :] = jnp.zeros((MLP_SUB, D_MODEL), F32)

    in_range = (blk >= 0) & (blk < n_blocks)
    even = (blk % 2) == 0

    @pl.when(in_range & even)
    def _():
        main(h_a, h_b, acc_a, acc_b)

    @pl.when(in_range & jnp.logical_not(even))
    def _():
        main(h_b, h_a, acc_b, acc_a)

    @pl.when(blk == n_blocks)
    def _():
        finish(acc_a if (n_blocks - 1) % 2 == 0 else acc_b)


def _mlp(x2d, g_pre, w_up, w_down, g_post):
    t = x2d.shape[0]
    ni = t // MLP_TM
    nj = D_FF // MLP_TF
    clamp = lambda blk: jnp.clip(blk, 0, ni - 1)
    sub_row = lambda blk, j: (clamp(blk) * nj + j, 0)
    wj = lambda i, j: jnp.where(i < 1, 0, jnp.where(i > ni, nj - 1, j))
    return pl.pallas_call(
        functools.partial(_mlp_kernel, n_blocks=ni),
        grid=(ni + 2, nj),
        in_specs=[
            pl.BlockSpec((MLP_SUB, D_MODEL), lambda i, j: sub_row(i, j)),
            pl.BlockSpec((MLP_SUB, D_MODEL), lambda i, j: sub_row(i - 2, j)),
            pl.BlockSpec((1, D_MODEL), lambda i, j: (0, 0)),
            pl.BlockSpec((D_MODEL, MLP_TF), lambda i, j: (0, wj(i, j))),
            pl.BlockSpec((MLP_TF, D_MODEL), lambda i, j: (wj(i, j), 0)),
            pl.BlockSpec((1, D_MODEL), lambda i, j: (0, 0)),
        ],
        out_specs=pl.BlockSpec((MLP_SUB, D_MODEL), lambda i, j: (jnp.where(i < 2, 0, (i - 2) * nj + j), 0)),
        out_shape=jax.ShapeDtypeStruct((t, D_MODEL), F32),
        scratch_shapes=[pltpu.VMEM((MLP_TM, D_MODEL), BF16), pltpu.VMEM((MLP_TM, D_MODEL), BF16),
                        pltpu.VMEM((MLP_TM, D_MODEL), F32), pltpu.VMEM((MLP_TM, D_MODEL), F32)],
        compiler_params=pltpu.CompilerParams(
            dimension_semantics=("arbitrary", "arbitrary"), vmem_limit_bytes=BIG_VMEM_LIMIT),
        name="mlp",
    )(x2d, x2d, g_pre, w_up, w_down, g_post)


def _rope_tables(seq):
    half = HEAD_DIM // 2
    inv_freq = ROPE_THETA ** (-np.arange(half, dtype=np.float64) / half)
    ang = np.arange(seq, dtype=np.float64)[:, None] * inv_freq[None, :]
    cos = np.cos(ang)
    sin = np.sin(ang)
    cos_t = np.concatenate([cos, cos], axis=1).astype(np.float32)
    sin_t = np.concatenate([-sin, sin], axis=1).astype(np.float32)
    return jnp.asarray(cos_t), jnp.asarray(sin_t)


def kernel(x, w_in, conv_w, gate_bias, ml_norm_g, attn_sink, w_out, g_pre_mix, g_post_mix,
           g_pre_mlp, g_post_mlp, w_up, w_down):
    batch, seq, d = x.shape
    depth = w_in.shape[0]
    cos_t, sin_t = _rope_tables(seq)
    x2d = x.reshape(batch * seq, d)
    w_in_bf = jnp.swapaxes(w_in, 1, 2).astype(BF16)
    for l in range(depth):
        proj, projt, gates_t, w_up_bf, w_out_bf, w_down_bf = _in_proj(
            x2d, g_pre_mix[l][None, :], w_in_bf, gate_bias[l][:, None], cos_t, sin_t,
            w_up, w_out, w_down, l, seq)
        k_conv, qt_conv = _qk_conv(proj, conv_w[l], seq)
        att = _attention(proj, attn_sink[l], batch, seq)
        hft, hbt = _mlstm(k_conv, qt_conv, projt, gates_t, batch, seq)
        g_ml_b = jnp.broadcast_to(ml_norm_g[l][:, None], (ML_V_COLS, LANES))
        x2d = _mix(att, hft, hbt, projt, x2d, w_out_bf, g_ml_b, g_post_mix[l][None, :])
        x2d = _mlp(x2d, g_pre_mlp[l][None, :], w_up_bf, w_down_bf, g_post_mlp[l][None, :])
    return x2d.reshape(batch, seq, d)
```

```python
import functools

import jax
import jax.numpy as jnp
import numpy as np
from jax import lax
from jax.experimental import pallas as pl
from jax.experimental.pallas import tpu as pltpu

F32 = jnp.float32
BF16 = jnp.bfloat16

D_MODEL = 2048
ATT_HEADS = 8
ATT_KV_HEADS = 2
ATT_GROUP = ATT_HEADS // ATT_KV_HEADS
HEAD_DIM = 128
ATT_WINDOW = 128
ROPE_THETA = 10000.0
ML_HEADS = 4
ML_V_DIM = 256
ML_QK_DIM = 128
ML_CHUNK = 256
M_INIT = -1e30
ATT_MASKED = -1e30
LOG2_E = 1.4426950408889634
D_FF = 4 * D_MODEL
NORM_EPS = 1e-6
GATE_COLS = 4 * ML_HEADS

ATT_Q_COLS = ATT_HEADS * HEAD_DIM
ATT_KV_COLS = ATT_KV_HEADS * HEAD_DIM
ML_QK_COLS = ML_HEADS * ML_QK_DIM
ML_V_COLS = ML_HEADS * ML_V_DIM

S_AK = ATT_Q_COLS
S_AV = S_AK + ATT_KV_COLS
S_MQ = S_AV + ATT_KV_COLS
S_MV = S_MQ + 2 * ML_QK_COLS
S_GATE = S_MV + 2 * ML_V_COLS
IN_COLS = S_GATE + GATE_COLS
C_AQ = 0
C_MQK = C_AQ + ATT_Q_COLS
C_AK = C_MQK + 2 * ML_QK_COLS
C_AV = C_AK + ATT_KV_COLS
PROJ_COLS = C_AV + ATT_KV_COLS
R_MV = 0
R_MO = R_MV + ML_V_COLS
PROJT_ROWS = R_MO + ML_V_COLS
LANES = 128
SUBLANES = 8
BF16_ROWS = 16

VMEM_LIMIT = 56 * 1024 * 1024
BIG_VMEM_LIMIT = 60 * 1024 * 1024

IN_TM = 512
IN_TN = 512
CONV_TM = 1024
CONV_SUB = 256
ATT_TQ = 2048
MIX_TM = 512
MLP_TM = 512
MLP_TF = 2048
MLP_UP_CHUNK = 512
MLP_SUB = MLP_TM // (D_FF // MLP_TF)
ML_STATE_ROWS = ML_V_DIM + BF16_ROWS
ML_PAIRS = 2 * ML_HEADS
ML_ROW_KINDS = 7
ML_SUB = 4

_NT = (((1,), (1,)), ((), ()))
_TN = (((0,), (0,)), ((), ()))


def _sigmoid(x):
    return 1.0 / (1.0 + jnp.exp2(x * (-LOG2_E)))


def _log_sigmoid(x):
    return jnp.minimum(x, 0.0) - jnp.log(1.0 + jnp.exp(-jnp.abs(x)))


def _rms(x, axis=-1):
    return x * lax.rsqrt(jnp.mean(x * x, axis=axis, keepdims=True) + NORM_EPS)


_ROW_MAJOR_CHUNKS = ((0, C_AQ), (IN_TN, C_AQ + IN_TN), (S_AK, C_AK), (S_MQ, C_MQK),
                     (S_MQ + IN_TN, C_MQK + IN_TN))


def _in_proj_kernel(x_ref, g_ref, wt_ref, gbias_ref, cos_ref, sin_ref, wup_ref, wout_ref, wdown_ref,
                    proj_ref, projt_ref, gatet_ref, wup_bf_ref, wout_bf_ref, wdown_bf_ref, h_ref):
    wup_bf_ref[...] = wup_ref[...].astype(BF16)
    wout_bf_ref[...] = wout_ref[...].astype(BF16)
    wdown_bf_ref[...] = wdown_ref[...].astype(BF16)
    h_ref[...] = (_rms(x_ref[...]) * g_ref[...]).astype(BF16)
    cos = cos_ref[...]
    sin = sin_ref[...]

    def rope(a):
        return a * cos + pltpu.roll(a, HEAD_DIM // 2, 1) * sin

    q_scale = HEAD_DIM ** -0.5 * LOG2_E
    for src, dst in _ROW_MAJOR_CHUNKS:
        acc = lax.dot_general(h_ref[...], wt_ref[src:src + IN_TN, :], _NT, preferred_element_type=F32)
        for k in range(IN_TN // HEAD_DIM):
            col = src + k * HEAD_DIM
            a = acc[:, k * HEAD_DIM:(k + 1) * HEAD_DIM]
            if col < S_AK:
                a = rope(a) * q_scale
            elif col < S_AV:
                a = rope(a)
            proj_ref[:, dst + k * HEAD_DIM:dst + (k + 1) * HEAD_DIM] = a.astype(BF16)
    n_chunks = PROJT_ROWS // IN_TN
    for c in range(n_chunks):
        r0 = S_MV + c * IN_TN
        rows = IN_TN + (GATE_COLS if c == n_chunks - 1 else 0)
        acc = lax.dot_general(wt_ref[r0:r0 + rows, :], h_ref[...], _NT, preferred_element_type=F32)
        projt_ref[c * IN_TN:(c + 1) * IN_TN, :] = acc[:IN_TN].astype(BF16)
        if c == n_chunks - 1:
            gatet_ref[...] = acc[IN_TN:] + gbias_ref[...]


def _in_proj(x2d, g, wt_bf_all, gate_bias_col, cos_t, sin_t, w_up_all, w_out_all, w_down_all, layer, seq):
    t = x2d.shape[0]
    steps = t // IN_TM
    pos_blocks = seq // IN_TM
    up_rows = D_MODEL // steps
    down_rows = D_FF // steps
    const = lambda i: (0, 0)
    row = lambda i: (i, 0)
    layer_row = lambda i: (layer, i, 0)
    return pl.pallas_call(
        _in_proj_kernel,
        grid=(steps,),
        in_specs=[
            pl.BlockSpec((IN_TM, D_MODEL), row),
            pl.BlockSpec((1, D_MODEL), const),
            pl.BlockSpec((None, IN_COLS, D_MODEL), lambda i: (layer, 0, 0), pipeline_mode=pl.Buffered(1)),
            pl.BlockSpec((GATE_COLS, 1), const),
            pl.BlockSpec((IN_TM, HEAD_DIM), lambda i: (i % pos_blocks, 0)),
            pl.BlockSpec((IN_TM, HEAD_DIM), lambda i: (i % pos_blocks, 0)),
            pl.BlockSpec((None, up_rows, D_FF), layer_row),
            pl.BlockSpec((None, up_rows, D_MODEL), layer_row),
            pl.BlockSpec((None, down_rows, D_MODEL), layer_row),
        ],
        out_specs=[
            pl.BlockSpec((IN_TM, PROJ_COLS), row),
            pl.BlockSpec((PROJT_ROWS, IN_TM), lambda i: (0, i)),
            pl.BlockSpec((GATE_COLS, IN_TM), lambda i: (0, i)),
            pl.BlockSpec((up_rows, D_FF), row),
            pl.BlockSpec((up_rows, D_MODEL), row),
            pl.BlockSpec((down_rows, D_MODEL), row),
        ],
        out_shape=[
            jax.ShapeDtypeStruct((t, PROJ_COLS), BF16),
            jax.ShapeDtypeStruct((PROJT_ROWS, t), BF16),
            jax.ShapeDtypeStruct((GATE_COLS, t), F32),
            jax.ShapeDtypeStruct((D_MODEL, D_FF), BF16),
            jax.ShapeDtypeStruct((D_MODEL, D_MODEL), BF16),
            jax.ShapeDtypeStruct((D_FF, D_MODEL), BF16),
        ],
        scratch_shapes=[pltpu.VMEM((IN_TM, D_MODEL), BF16)],
        compiler_params=pltpu.CompilerParams(
            dimension_semantics=("parallel",), vmem_limit_bytes=BIG_VMEM_LIMIT),
        name="in_proj",
    )(x2d, g, wt_bf_all, gate_bias_col, cos_t, sin_t, w_up_all, w_out_all, w_down_all)


def _qk_conv_kernel(x_ref, xp_ref, xn_ref, w_ref, k_ref, qt_ref, *, seq_blocks):
    n = CONV_SUB
    pos = pl.program_id(0) % seq_blocks
    has_prev = (pos > 0).astype(F32)
    has_next = (pos < seq_blocks - 1).astype(F32)
    ri = lax.broadcasted_iota(jnp.int32, (n, n), 0)
    ci = lax.broadcasted_iota(jnp.int32, (n, n), 1)
    shift_prev = (ci == ri - 1).astype(BF16)
    shift_next = (ci == ri + 1).astype(BF16)
    rowid = lax.broadcasted_iota(jnp.int32, (n, 1), 0)
    w0 = w_ref[0:1, :]
    w1 = w_ref[1:2, :]
    w2 = w_ref[2:3, :]
    n_sub = CONV_TM // n
    for sb in range(n_sub):
        xs = x_ref[sb * n:(sb + 1) * n, :]
        x_prev = jnp.dot(shift_prev, xs, preferred_element_type=F32)
        x_next = jnp.dot(shift_next, xs, preferred_element_type=F32)
        if sb == 0:
            prev_row = xp_ref[BF16_ROWS - 1:BF16_ROWS, :].astype(F32) * has_prev
        else:
            prev_row = x_ref[sb * n - BF16_ROWS:sb * n, :].astype(F32)[BF16_ROWS - 1:BF16_ROWS]
        if sb == n_sub - 1:
            next_row = xn_ref[0:1, :].astype(F32) * has_next
        else:
            next_row = x_ref[(sb + 1) * n:(sb + 1) * n + BF16_ROWS, :].astype(F32)[0:1]
        x_prev = jnp.where(rowid == 0, prev_row, x_prev)
        x_next = jnp.where(rowid == n - 1, next_row, x_next)
        y = x_prev * w0 + xs.astype(F32) * w1 + x_next * w2
        y = y * _sigmoid(y)
        k_ref[sb * n:(sb + 1) * n, :] = y[:, ML_QK_COLS:].astype(BF16)
        q = y[:, :ML_QK_COLS] * (ML_QK_DIM ** -0.5)
        qt_ref[:, sb * n:(sb + 1) * n] = q.T.astype(BF16)


def _qk_conv(proj, conv_w, seq):
    t = proj.shape[0]
    width = 2 * ML_QK_COLS
    col = C_MQK // width
    per = CONV_TM // BF16_ROWS
    last = t // BF16_ROWS - 1
    return pl.pallas_call(
        functools.partial(_qk_conv_kernel, seq_blocks=seq // CONV_TM),
        grid=(t // CONV_TM,),
        in_specs=[
            pl.BlockSpec((CONV_TM, width), lambda i: (i, col)),
            pl.BlockSpec((BF16_ROWS, width), lambda i: (jnp.maximum(i * per - 1, 0), col)),
            pl.BlockSpec((BF16_ROWS, width), lambda i: (jnp.minimum((i + 1) * per, last), col)),
            pl.BlockSpec((3, width), lambda i: (0, 0)),
        ],
        out_specs=[pl.BlockSpec((CONV_TM, ML_QK_COLS), lambda i: (i, 0)),
                   pl.BlockSpec((ML_QK_COLS, CONV_TM), lambda i: (0, i))],
        out_shape=[jax.ShapeDtypeStruct((t, ML_QK_COLS), BF16),
                   jax.ShapeDtypeStruct((ML_QK_COLS, t), BF16)],
        compiler_params=pltpu.CompilerParams(
            dimension_semantics=("parallel",), vmem_limit_bytes=VMEM_LIMIT),
        name="qk_conv",
    )(proj, proj, proj, conv_w)


def _attn_kernel(sink_ref, q_ref, kc_ref, kp_ref, kn_ref, vc_ref, vp_ref, vn_ref,
                 o_ref, kbuf, vbuf, *, n_blocks):
    w = ATT_WINDOW
    kbuf[0:w] = kp_ref[...]
    kbuf[w:w + ATT_TQ] = kc_ref[...]
    kbuf[w + ATT_TQ:] = kn_ref[...]
    for h in range(ATT_KV_HEADS):
        hs = slice(h * HEAD_DIM, (h + 1) * HEAD_DIM)
        vbuf[h, 0:w, 0:HEAD_DIM] = vp_ref[:, hs]
        vbuf[h, w:w + ATT_TQ, 0:HEAD_DIM] = vc_ref[:, hs]
        vbuf[h, w + ATT_TQ:, 0:HEAD_DIM] = vn_ref[:, hs]
        vbuf[h, :, HEAD_DIM:] = jnp.ones((ATT_TQ + 2 * w, HEAD_DIM), BF16)
    i = pl.program_id(1)
    rows = ATT_GROUP * w
    qi = lax.broadcasted_iota(jnp.int32, (rows, 1), 0) & (w - 1)
    blk = lax.broadcasted_iota(jnp.int32, (rows, 1), 0) // w
    kj = lax.broadcasted_iota(jnp.int32, (rows, 3 * w), 1)
    kj_row = lax.broadcasted_iota(jnp.int32, (1, 3 * w), 1)
    band = jnp.where((kj >= qi) & (kj <= qi + 2 * w), 0.0, ATT_MASKED)
    n_win = ATT_TQ // w
    sinks = []
    for h in range(ATT_KV_HEADS):
        sink = jnp.full((rows, 1), sink_ref[h * ATT_GROUP], F32)
        for g in range(1, ATT_GROUP):
            sink = jnp.where(blk == g, sink_ref[h * ATT_GROUP + g], sink)
        sinks.append(sink * LOG2_E)
    tiles = [(n, h) for n in range(n_win) for h in range(ATT_KV_HEADS)]
    scores = []
    for n, h in tiles:
        qs = jnp.concatenate(
            [q_ref[n * w:(n + 1) * w, (h * ATT_GROUP + g) * HEAD_DIM:(h * ATT_GROUP + g + 1) * HEAD_DIM]
             for g in range(ATT_GROUP)], axis=0)
        kw = kbuf[n * w:(n + 3) * w, h * HEAD_DIM:(h + 1) * HEAD_DIM]
        scores.append(lax.dot_general(qs, kw, _NT, preferred_element_type=F32))
    probs = []
    maxes = []
    for idx, (n, h) in enumerate(tiles):
        bias = band
        if n == 0:
            bias = bias + jnp.where(kj_row < w, jnp.where(i == 0, ATT_MASKED, 0.0), 0.0)
        if n == n_win - 1:
            bias = bias + jnp.where(kj_row >= 2 * w, jnp.where(i == n_blocks - 1, ATT_MASKED, 0.0), 0.0)
        s = scores[idx] + bias
        m = jnp.maximum(jnp.max(s, axis=-1, keepdims=True), sinks[h])
        maxes.append(m)
        probs.append(jnp.exp2(s - m).astype(BF16))
    for idx, (n, h) in enumerate(tiles):
        o_aug = jnp.dot(probs[idx], vbuf[h, n * w:(n + 3) * w, :], preferred_element_type=F32)
        denom = o_aug[:, HEAD_DIM:] + jnp.exp2(sinks[h] - maxes[idx])
        o = o_aug[:, :HEAD_DIM] * (1.0 / denom)
        for g in range(ATT_GROUP):
            col = (h * ATT_GROUP + g) * HEAD_DIM
            o_ref[n * w:(n + 1) * w, col:col + HEAD_DIM] = o[g * w:(g + 1) * w].astype(BF16)


def _attention(proj, sink, batch, seq):
    t = proj.shape[0]
    w = ATT_WINDOW
    nq = seq // ATT_TQ
    per = ATT_TQ // w
    last_blk = t // w - 1
    cur = lambda b, i: b * nq + i
    prev = lambda b, i: jnp.maximum((b * nq + i) * per - 1, 0)
    nxt = lambda b, i: jnp.minimum((b * nq + i + 1) * per, last_blk)
    kcol = C_AK // ATT_KV_COLS
    vcol = C_AV // ATT_KV_COLS
    return pl.pallas_call(
        functools.partial(_attn_kernel, n_blocks=nq),
        grid=(batch, nq),
        in_specs=[
            pl.BlockSpec(memory_space=pltpu.SMEM),
            pl.BlockSpec((ATT_TQ, ATT_Q_COLS), lambda b, i: (cur(b, i), C_AQ // ATT_Q_COLS)),
            pl.BlockSpec((ATT_TQ, ATT_KV_COLS), lambda b, i: (cur(b, i), kcol)),
            pl.BlockSpec((w, ATT_KV_COLS), lambda b, i: (prev(b, i), kcol)),
            pl.BlockSpec((w, ATT_KV_COLS), lambda b, i: (nxt(b, i), kcol)),
            pl.BlockSpec((ATT_TQ, ATT_KV_COLS), lambda b, i: (cur(b, i), vcol)),
            pl.BlockSpec((w, ATT_KV_COLS), lambda b, i: (prev(b, i), vcol)),
            pl.BlockSpec((w, ATT_KV_COLS), lambda b, i: (nxt(b, i), vcol)),
        ],
        out_specs=pl.BlockSpec((ATT_TQ, ATT_Q_COLS), lambda b, i: (cur(b, i), 0)),
        out_shape=jax.ShapeDtypeStruct((t, ATT_Q_COLS), BF16),
        scratch_shapes=[pltpu.VMEM((ATT_TQ + 2 * w, ATT_KV_COLS), BF16),
                        pltpu.VMEM((ATT_KV_HEADS, ATT_TQ + 2 * w, 2 * HEAD_DIM), BF16)],
        compiler_params=pltpu.CompilerParams(
            dimension_semantics=("parallel", "parallel"), vmem_limit_bytes=VMEM_LIMIT),
        name="attn",
    )(sink, proj, proj, proj, proj, proj, proj, proj)


def _rows_to_cols(x):
    length = x.shape[1]
    padded = jnp.concatenate([x, jnp.zeros((length - x.shape[0], length), x.dtype)], axis=0)
    return padded.T


def _ml_gate_rows(gates_f, gates_b, m_prev):
    L = ML_CHUNK
    H = ML_HEADS
    row8 = lax.broadcasted_iota(jnp.int32, (2 * H, L), 0)
    lane = lax.broadcasted_iota(jnp.int32, (2 * H, L), 1)
    is_fwd = row8 < H
    fwd_if = gates_f[0:2 * H]
    bwd_if = gates_b[2 * H:4 * H]
    gi = jnp.where(is_fwd, fwd_if, pltpu.roll(bwd_if, H, 0))
    gf = jnp.where(is_fwd, pltpu.roll(fwd_if, H, 0), bwd_if)
    ls = _log_sigmoid(gf)
    hi = ls.astype(BF16).astype(F32)
    rem = ls - hi
    mid = rem.astype(BF16).astype(F32)
    lo = rem - mid
    parts = jnp.concatenate([hi, mid, lo, jnp.zeros_like(hi)], axis=0).astype(BF16)
    si = lax.broadcasted_iota(jnp.int32, (L, L), 0)
    ti = lax.broadcasted_iota(jnp.int32, (L, L), 1)
    pre = jnp.dot(parts, (si <= ti).astype(BF16), preferred_element_type=F32)
    prefix = pre[0:2 * H] + pre[2 * H:4 * H] + pre[4 * H:6 * H]
    gtot = jnp.sum(ls, axis=1, keepdims=True)
    b = jnp.where(is_fwd, prefix, gtot - prefix + ls)
    r = gi - b
    cm = r
    sh = 1
    while sh < L:
        from_left = jnp.where(lane >= sh, pltpu.roll(cm, sh, 1), -jnp.inf)
        from_right = jnp.where(lane < L - sh, pltpu.roll(cm, L - sh, 1), -jnp.inf)
        cm = jnp.maximum(cm, jnp.where(is_fwd, from_left, from_right))
        sh *= 2
    top = jnp.maximum(m_prev, cm)
    a = gtot - b + gi
    m_loc = jnp.max(a, axis=1, keepdims=True)
    m_new = jnp.maximum(gtot + m_prev, m_loc)
    rows = jnp.concatenate([
        -top,
        jnp.exp(m_prev - top),
        jnp.exp(-(b + top)),
        r,
        jnp.exp(a - m_loc),
        jnp.exp(gtot + m_prev - m_new),
        jnp.exp(m_loc - m_new) + jnp.zeros_like(r),
    ], axis=0)
    return rows, m_new


def _mlstm_kernel(kf_ref, qtf_ref, vtf_ref, gtf_ref, gtfn_ref, kb_ref, qtb_ref, vtb_ref, gtb_ref,
                  gtbn_ref, hf_ref, hb_ref, c_sc, m_sc, rows_sc):
    L = ML_CHUNK
    H = ML_HEADS
    P = ML_PAIRS
    c = pl.program_id(1)
    slot = c % 2

    def tok(d, u):
        lo = u * L if d == 0 else (ML_SUB - 1 - u) * L
        return slice(lo, lo + L)

    def gate_rows(gf_ref, gb_ref, u, m_prev):
        return _ml_gate_rows(gf_ref[:, tok(0, u)], gb_ref[:, tok(1, u)], m_prev)

    @pl.when(c == 0)
    def _():
        c_sc[...] = jnp.zeros_like(c_sc)
        m = jnp.full((P, L), M_INIT, F32)
        for u in range(ML_SUB):
            rows0, m = gate_rows(gtf_ref, gtb_ref, u, m)
            rows_sc[0, u] = rows0
        m_sc[...] = m

    rows_now = [rows_sc[slot, u] for u in range(ML_SUB)]
    m = m_sc[...]
    for u in range(ML_SUB):
        rows_next, m = gate_rows(gtfn_ref, gtbn_ref, u, m)
        rows_sc[1 - slot, u] = rows_next
    m_sc[...] = m

    si = lax.broadcasted_iota(jnp.int32, (L, L), 0)
    ti = lax.broadcasted_iota(jnp.int32, (L, L), 1)
    masks = (si <= ti, si >= ti)
    n_row = lax.broadcasted_iota(jnp.int32, (BF16_ROWS, 1), 0) == 0
    refs = ((kf_ref, qtf_ref, vtf_ref, hf_ref), (kb_ref, qtb_ref, vtb_ref, hb_ref))
    pairs = [(d, j) for d in range(2) for j in range(H)]

    for u in range(ML_SUB):
        rows = rows_now[u]
        bm, w_inter, clamp, r, w_end, s_prev, s_loc = [rows[i * P:(i + 1) * P] for i in range(ML_ROW_KINDS)]
        r_cols = _rows_to_cols(r)
        w_end_cols = _rows_to_cols(w_end)

        def k_of(d, j):
            return refs[d][0][tok(d, u), j * ML_QK_DIM:(j + 1) * ML_QK_DIM]

        def qt_of(d, j):
            return refs[d][1][j * ML_QK_DIM:(j + 1) * ML_QK_DIM, tok(d, u)]

        scores = []
        carried = []
        for d, j in pairs:
            p = d * H + j
            scores.append(jnp.dot(k_of(d, j), qt_of(d, j), preferred_element_type=F32))
            qtw = (qt_of(d, j).astype(F32) * w_inter[p:p + 1, :]).astype(BF16)
            carried.append(jnp.dot(c_sc[p].astype(BF16), qtw, preferred_element_type=F32))
        dens = []
        n_locs = []
        rhss = []
        for d, j in pairs:
            p = d * H + j
            arg = jnp.broadcast_to(r_cols[:, p:p + 1], (L, L)) + bm[p:p + 1, :]
            sc_t = scores[p] * jnp.exp(jnp.where(masks[d], arg, -jnp.inf))
            dens.append(jnp.sum(sc_t, axis=0, keepdims=True))
            kw = k_of(d, j).astype(F32) * jnp.broadcast_to(w_end_cols[:, p:p + 1], (L, ML_QK_DIM))
            n_locs.append(jnp.sum(kw, axis=0, keepdims=True))
            rhss.append(jnp.concatenate([sc_t.astype(BF16), kw.astype(BF16)], axis=1))
        boths = []
        for d, j in pairs:
            p = d * H + j
            vt_j = refs[d][2][j * ML_V_DIM:(j + 1) * ML_V_DIM, tok(d, u)]
            boths.append(jnp.dot(vt_j, rhss[p], preferred_element_type=F32))
        for d, j in pairs:
            p = d * H + j
            num = boths[p][:, :L] + carried[p][:ML_V_DIM]
            den = dens[p] + carried[p][ML_V_DIM:ML_V_DIM + 1]
            refs[d][3][j * ML_V_DIM:(j + 1) * ML_V_DIM, tok(d, u)] = (
                num / jnp.maximum(jnp.abs(den), clamp[p:p + 1, :])).astype(BF16)
            c_prev = c_sc[p]
            sp = s_prev[p:p + 1, :ML_QK_DIM]
            sl = s_loc[p:p + 1, :ML_QK_DIM]
            c_sc[p, 0:ML_V_DIM, :] = sp * c_prev[:ML_V_DIM] + sl * boths[p][:, L:]
            c_sc[p, ML_V_DIM:, :] = sp * c_prev[ML_V_DIM:] + sl * jnp.where(n_row, n_locs[p], 0.0)


def _mlstm(k_conv, qt_conv, projt, gates_t, batch, seq):
    t = k_conv.shape[0]
    blk = ML_SUB * ML_CHUNK
    nc = seq // blk
    fwd = lambda b, c: b * nc + c
    bwd = lambda b, c: b * nc + nc - 1 - c
    nxt = lambda c: jnp.minimum(c + 1, nc - 1)

    def dir_specs(ch):
        return [
            pl.BlockSpec((blk, ML_QK_COLS), lambda b, c: (ch(b, c), 0)),
            pl.BlockSpec((ML_QK_COLS, blk), lambda b, c: (0, ch(b, c))),
            pl.BlockSpec((ML_V_COLS, blk), lambda b, c: (R_MV // ML_V_COLS, ch(b, c))),
            pl.BlockSpec((GATE_COLS, blk), lambda b, c: (0, ch(b, c))),
            pl.BlockSpec((GATE_COLS, blk), lambda b, c: (0, ch(b, nxt(c)))),
        ]

    return pl.pallas_call(
        _mlstm_kernel,
        grid=(batch, nc),
        in_specs=dir_specs(fwd) + dir_specs(bwd),
        out_specs=[pl.BlockSpec((ML_V_COLS, blk), lambda b, c: (0, fwd(b, c))),
                   pl.BlockSpec((ML_V_COLS, blk), lambda b, c: (0, bwd(b, c)))],
        out_shape=[jax.ShapeDtypeStruct((ML_V_COLS, t), BF16),
                   jax.ShapeDtypeStruct((ML_V_COLS, t), BF16)],
        scratch_shapes=[pltpu.VMEM((ML_PAIRS, ML_STATE_ROWS, ML_QK_DIM), F32),
                        pltpu.VMEM((ML_PAIRS, ML_CHUNK), F32),
                        pltpu.VMEM((2, ML_SUB, ML_ROW_KINDS * ML_PAIRS, ML_CHUNK), F32)],
        compiler_params=pltpu.CompilerParams(
            dimension_semantics=("parallel", "arbitrary"), vmem_limit_bytes=VMEM_LIMIT),
        name="mlstm",
    )(k_conv, qt_conv, projt, gates_t, gates_t, k_conv, qt_conv, projt, gates_t, gates_t)


def _mix_kernel(att_ref, hft_ref, hbt_ref, mot_ref, x_ref, w_ref, gml_ref, gpost_ref, o_ref, cat_ref):
    cat_ref[:, 0:ATT_Q_COLS] = att_ref[...]
    for j in range(ML_HEADS):
        sl = slice(j * ML_V_DIM, (j + 1) * ML_V_DIM)
        h = _rms(hft_ref[sl, :].astype(F32) + hbt_ref[sl, :].astype(F32), axis=0)
        gain = jnp.concatenate([gml_ref[sl, :]] * (MIX_TM // LANES), axis=1)
        mem = h * gain * _sigmoid(mot_ref[sl, :].astype(F32))
        cat_ref[:, ATT_Q_COLS + j * ML_V_DIM:ATT_Q_COLS + (j + 1) * ML_V_DIM] = mem.T.astype(BF16)
    mix = jnp.dot(cat_ref[...], w_ref[...], preferred_element_type=F32)
    o_ref[...] = x_ref[...] + _rms(mix) * gpost_ref[...]


def _mix(att, hft, hbt, projt, x2d, w_out_bf, g_ml_b, g_post):
    t = x2d.shape[0]
    steps = t // MIX_TM
    row = lambda i: (i, 0)
    col = lambda i: (0, i)
    const = lambda i: (0, 0)
    return pl.pallas_call(
        _mix_kernel,
        grid=(steps,),
        in_specs=[
            pl.BlockSpec((MIX_TM, ATT_Q_COLS), row),
            pl.BlockSpec((ML_V_COLS, MIX_TM), col),
            pl.BlockSpec((ML_V_COLS, MIX_TM), col),
            pl.BlockSpec((ML_V_COLS, MIX_TM), lambda i: (R_MO // ML_V_COLS, i)),
            pl.BlockSpec((MIX_TM, D_MODEL), row),
            pl.BlockSpec((D_MODEL, D_MODEL), const, pipeline_mode=pl.Buffered(1)),
            pl.BlockSpec((ML_V_COLS, LANES), const),
            pl.BlockSpec((1, D_MODEL), const),
        ],
        out_specs=pl.BlockSpec((MIX_TM, D_MODEL), row),
        out_shape=jax.ShapeDtypeStruct((t, D_MODEL), F32),
        scratch_shapes=[pltpu.VMEM((MIX_TM, D_MODEL), BF16)],
        compiler_params=pltpu.CompilerParams(
            dimension_semantics=("parallel",), vmem_limit_bytes=VMEM_LIMIT),
        name="mix",
    )(att, hft, hbt, projt, x2d, w_out_bf, g_ml_b, g_post)


def _pin_before_next_load(operand_ref, value):
    bits = pltpu.bitcast(value, jnp.uint32)
    tiles = [bits[r:r + SUBLANES, c:c + LANES]
             for r in range(0, value.shape[0], SUBLANES) for c in range(0, value.shape[1], LANES)]
    folded = functools.reduce(jnp.bitwise_or, tiles)
    half = jnp.uint32(16)
    zero = pltpu.bitcast(lax.shift_right_logical(lax.shift_right_logical(folded, half), half), F32)
    zero = jnp.concatenate([zero] * (BF16_ROWS // SUBLANES), axis=0).astype(BF16)
    operand_ref[0:BF16_ROWS, 0:LANES] = operand_ref[0:BF16_ROWS, 0:LANES] + zero


def _mlp_kernel(xn_ref, xp_ref, gpre_ref, wup_ref, wdown_ref, gpost_ref, o_ref,
                h_a, h_b, acc_a, acc_b, *, n_blocks):
    blk = pl.program_id(0) - 1
    j = pl.program_id(1)
    rows = pl.ds(pl.multiple_of(j * MLP_SUB, MLP_SUB), MLP_SUB)

    def pre_norm(h_next):
        h = _rms(xn_ref[...]) * gpre_ref[...]
        h_next[rows, :] = h.astype(BF16)
        return h

    def finish(acc_prev):
        y = xp_ref[...] + _rms(acc_prev[rows, :]) * gpost_ref[...]
        o_ref[...] = y
        return y

    def main(h_cur, h_next, acc_cur, acc_prev):
        n_chunks = MLP_TF // MLP_UP_CHUNK
        us = []
        for c in range(n_chunks):
            cols = slice(c * MLP_UP_CHUNK, (c + 1) * MLP_UP_CHUNK)
            u = jnp.dot(h_cur[...], wup_ref[:, cols], preferred_element_type=F32)
            us.append(jnp.square(jnp.maximum(u, 0.0)).astype(BF16))
            if c == 0:
                _pin_before_next_load(h_cur, finish(acc_prev))
            elif c == 1:
                _pin_before_next_load(h_cur, pre_norm(h_next))
        part = jnp.dot(jnp.concatenate(us, axis=1), wdown_ref[...], preferred_element_type=F32)
        acc_cur[...] = jnp.where(j > 0, acc_cur[...], 0.0) + part

    @pl.when(blk < 0)
    def _():
        pre_norm(h_a)
        acc_a[rows, :] = jnp.zeros((MLP_SUB, D_MODEL), F32)
        acc_b[rows, :] = jnp.zeros((MLP_SUB, D_MODEL), F32)

    in_range = (blk >= 0) & (blk < n_blocks)
    even = (blk % 2) == 0

    @pl.when(in_range & even)
    def _():
        main(h_a, h_b, acc_a, acc_b)

    @pl.when(in_range & jnp.logical_not(even))
    def _():
        main(h_b, h_a, acc_b, acc_a)

    @pl.when(blk == n_blocks)
    def _():
        finish(acc_a if (n_blocks - 1) % 2 == 0 else acc_b)


def _mlp(x2d, g_pre, w_up, w_down, g_post):
    t = x2d.shape[0]
    ni = t // MLP_TM
    nj = D_FF // MLP_TF
    clamp = lambda blk: jnp.clip(blk, 0, ni - 1)
    sub_row = lambda blk, j: (clamp(blk) * nj + j, 0)
    wj = lambda i, j: jnp.where(i < 1, 0, jnp.where(i > ni, nj - 1, j))
    return pl.pallas_call(
        functools.partial(_mlp_kernel, n_blocks=ni),
        grid=(ni + 2, nj),
        in_specs=[
            pl.BlockSpec((MLP_SUB, D_MODEL), lambda i, j: sub_row(i, j)),
            pl.BlockSpec((MLP_SUB, D_MODEL), lambda i, j: sub_row(i - 2, j)),
            pl.BlockSpec((1, D_MODEL), lambda i, j: (0, 0)),
            pl.BlockSpec((D_MODEL, MLP_TF), lambda i, j: (0, wj(i, j))),
            pl.BlockSpec((MLP_TF, D_MODEL), lambda i, j: (wj(i, j), 0)),
            pl.BlockSpec((1, D_MODEL), lambda i, j: (0, 0)),
        ],
        out_specs=pl.BlockSpec((MLP_SUB, D_MODEL), lambda i, j: (jnp.where(i < 2, 0, (i - 2) * nj + j), 0)),
        out_shape=jax.ShapeDtypeStruct((t, D_MODEL), F32),
        scratch_shapes=[pltpu.VMEM((MLP_TM, D_MODEL), BF16), pltpu.VMEM((MLP_TM, D_MODEL), BF16),
                        pltpu.VMEM((MLP_TM, D_MODEL), F32), pltpu.VMEM((MLP_TM, D_MODEL), F32)],
        compiler_params=pltpu.CompilerParams(
            dimension_semantics=("arbitrary", "arbitrary"), vmem_limit_bytes=BIG_VMEM_LIMIT),
        name="mlp",
    )(x2d, x2d, g_pre, w_up, w_down, g_post)


def _rope_tables(seq):
    half = HEAD_DIM // 2
    inv_freq = ROPE_THETA ** (-np.arange(half, dtype=np.float64) / half)
    ang = np.arange(seq, dtype=np.float64)[:, None] * inv_freq[None, :]
    cos = np.cos(ang)
    sin = np.sin(ang)
    cos_t = np.concatenate([cos, cos], axis=1).astype(np.float32)
    sin_t = np.concatenate([-sin, sin], axis=1).astype(np.float32)
    return jnp.asarray(cos_t), jnp.asarray(sin_t)


def kernel(x, w_in, conv_w, gate_bias, ml_norm_g, attn_sink, w_out, g_pre_mix, g_post_mix,
           g_pre_mlp, g_post_mlp, w_up, w_down):
    batch, seq, d = x.shape
    depth = w_in.shape[0]
    cos_t, sin_t = _rope_tables(seq)
    x2d = x.reshape(batch * seq, d)
    w_in_bf = jnp.swapaxes(w_in, 1, 2).astype(BF16)
    for l in range(depth):
        proj, projt, gates_t, w_up_bf, w_out_bf, w_down_bf = _in_proj(
            x2d, g_pre_mix[l][None, :], w_in_bf, gate_bias[l][:, None], cos_t, sin_t,
            w_up, w_out, w_down, l, seq)
        k_conv, qt_conv = _qk_conv(proj, conv_w[l], seq)
        att = _attention(proj, attn_sink[l], batch, seq)
        hft, hbt = _mlstm(k_conv, qt_conv, projt, gates_t, batch, seq)
        g_ml_b = jnp.broadcast_to(ml_norm_g[l][:, None], (ML_V_COLS, LANES))
        x2d = _mix(att, hft, hbt, projt, x2d, w_out_bf, g_ml_b, g_post_mix[l][None, :])
        x2d = _mlp(x2d, g_pre_mlp[l][None, :], w_up_bf, w_down_bf, g_post_mlp[l][None, :])
    return x2d.reshape(batch, seq, d)
```

```python
import functools

import jax
import jax.numpy as jnp
import numpy as np
from jax import lax
from jax.experimental import pallas as pl
from jax.experimental.pallas import tpu as pltpu

F32 = jnp.float32
BF16 = jnp.bfloat16

D_MODEL = 2048
ATT_HEADS = 8
ATT_KV_HEADS = 2
ATT_GROUP = ATT_HEADS // ATT_KV_HEADS
HEAD_DIM = 128
ATT_WINDOW = 128
ROPE_THETA = 10000.0
ML_HEADS = 4
ML_V_DIM = 256
ML_QK_DIM = 128
ML_CHUNK = 256
M_INIT = -1e30
ATT_MASKED = -1e30
LOG2_E = 1.4426950408889634
D_FF = 4 * D_MODEL
NORM_EPS = 1e-6
GATE_COLS = 4 * ML_HEADS

ATT_Q_COLS = ATT_HEADS * HEAD_DIM
ATT_KV_COLS = ATT_KV_HEADS * HEAD_DIM
ML_QK_COLS = ML_HEADS * ML_QK_DIM
ML_V_COLS = ML_HEADS * ML_V_DIM

S_AK = ATT_Q_COLS
S_AV = S_AK + ATT_KV_COLS
S_MQ = S_AV + ATT_KV_COLS
S_MV = S_MQ + 2 * ML_QK_COLS
S_GATE = S_MV + 2 * ML_V_COLS
IN_COLS = S_GATE + GATE_COLS
C_AQ = 0
C_MQK = C_AQ + ATT_Q_COLS
C_AK = C_MQK + 2 * ML_QK_COLS
C_AV = C_AK + ATT_KV_COLS
PROJ_COLS = C_AV + ATT_KV_COLS
R_MV = 0
R_MO = R_MV + ML_V_COLS
PROJT_ROWS = R_MO + ML_V_COLS
LANES = 128
SUBLANES = 8
BF16_ROWS = 16

VMEM_LIMIT = 56 * 1024 * 1024
BIG_VMEM_LIMIT = 60 * 1024 * 1024

IN_TM = 512
IN_TN = 512
CONV_TM = 2048
CONV_SUB = 256
ATT_TQ = 2048
MIX_TM = 512
MLP_TM = 512
MLP_TF = 2048
MLP_UP_CHUNK = 512
MLP_SUB = MLP_TM // (D_FF // MLP_TF)
ML_STATE_ROWS = ML_V_DIM + BF16_ROWS
ML_PAIRS = 2 * ML_HEADS
ML_ROW_KINDS = 7
ML_SUB = 4

_NT = (((1,), (1,)), ((), ()))
_TN = (((0,), (0,)), ((), ()))


def _sigmoid(x):
    return 1.0 / (1.0 + jnp.exp2(x * (-LOG2_E)))


def _log_sigmoid(x):
    return jnp.minimum(x, 0.0) - jnp.log(1.0 + jnp.exp(-jnp.abs(x)))


def _rms(x, axis=-1):
    return x * lax.rsqrt(jnp.mean(x * x, axis=axis, keepdims=True) + NORM_EPS)


_ROW_MAJOR_CHUNKS = ((0, C_AQ), (IN_TN, C_AQ + IN_TN), (S_AK, C_AK), (S_MQ, C_MQK),
                     (S_MQ + IN_TN, C_MQK + IN_TN))


def _in_proj_kernel(x_ref, g_ref, wt_ref, gbias_ref, cos_ref, sin_ref, wup_ref, wout_ref, wdown_ref,
                    proj_ref, projt_ref, gatet_ref, wup_bf_ref, wout_bf_ref, wdown_bf_ref, h_ref):
    wup_bf_ref[...] = wup_ref[...].astype(BF16)
    wout_bf_ref[...] = wout_ref[...].astype(BF16)
    wdown_bf_ref[...] = wdown_ref[...].astype(BF16)
    h_ref[...] = (_rms(x_ref[...]) * g_ref[...]).astype(BF16)
    cos = cos_ref[...]
    sin = sin_ref[...]

    def rope(a):
        return a * cos + pltpu.roll(a, HEAD_DIM // 2, 1) * sin

    q_scale = HEAD_DIM ** -0.5 * LOG2_E
    for src, dst in _ROW_MAJOR_CHUNKS:
        acc = lax.dot_general(h_ref[...], wt_ref[src:src + IN_TN, :], _NT, preferred_element_type=F32)
        for k in range(IN_TN // HEAD_DIM):
            col = src + k * HEAD_DIM
            a = acc[:, k * HEAD_DIM:(k + 1) * HEAD_DIM]
            if col < S_AK:
                a = rope(a) * q_scale
            elif col < S_AV:
                a = rope(a)
            proj_ref[:, dst + k * HEAD_DIM:dst + (k + 1) * HEAD_DIM] = a.astype(BF16)
    n_chunks = PROJT_ROWS // IN_TN
    for c in range(n_chunks):
        r0 = S_MV + c * IN_TN
        rows = IN_TN + (GATE_COLS if c == n_chunks - 1 else 0)
        acc = lax.dot_general(wt_ref[r0:r0 + rows, :], h_ref[...], _NT, preferred_element_type=F32)
        projt_ref[c * IN_TN:(c + 1) * IN_TN, :] = acc[:IN_TN].astype(BF16)
        if c == n_chunks - 1:
            gatet_ref[...] = acc[IN_TN:] + gbias_ref[...]


def _in_proj(x2d, g, wt_bf_all, gate_bias_col, cos_t, sin_t, w_up_all, w_out_all, w_down_all, layer, seq):
    t = x2d.shape[0]
    steps = t // IN_TM
    pos_blocks = seq // IN_TM
    up_rows = D_MODEL // steps
    down_rows = D_FF // steps
    const = lambda i: (0, 0)
    row = lambda i: (i, 0)
    layer_row = lambda i: (layer, i, 0)
    return pl.pallas_call(
        _in_proj_kernel,
        grid=(steps,),
        in_specs=[
            pl.BlockSpec((IN_TM, D_MODEL), row),
            pl.BlockSpec((1, D_MODEL), const),
            pl.BlockSpec((None, IN_COLS, D_MODEL), lambda i: (layer, 0, 0), pipeline_mode=pl.Buffered(1)),
            pl.BlockSpec((GATE_COLS, 1), const),
            pl.BlockSpec((IN_TM, HEAD_DIM), lambda i: (i % pos_blocks, 0)),
            pl.BlockSpec((IN_TM, HEAD_DIM), lambda i: (i % pos_blocks, 0)),
            pl.BlockSpec((None, up_rows, D_FF), layer_row),
            pl.BlockSpec((None, up_rows, D_MODEL), layer_row),
            pl.BlockSpec((None, down_rows, D_MODEL), layer_row),
        ],
        out_specs=[
            pl.BlockSpec((IN_TM, PROJ_COLS), row),
            pl.BlockSpec((PROJT_ROWS, IN_TM), lambda i: (0, i)),
            pl.BlockSpec((GATE_COLS, IN_TM), lambda i: (0, i)),
            pl.BlockSpec((up_rows, D_FF), row),
            pl.BlockSpec((up_rows, D_MODEL), row),
            pl.BlockSpec((down_rows, D_MODEL), row),
        ],
        out_shape=[
            jax.ShapeDtypeStruct((t, PROJ_COLS), BF16),
            jax.ShapeDtypeStruct((PROJT_ROWS, t), BF16),
            jax.ShapeDtypeStruct((GATE_COLS, t), F32),
            jax.ShapeDtypeStruct((D_MODEL, D_FF), BF16),
            jax.ShapeDtypeStruct((D_MODEL, D_MODEL), BF16),
            jax.ShapeDtypeStruct((D_FF, D_MODEL), BF16),
        ],
        scratch_shapes=[pltpu.VMEM((IN_TM, D_MODEL), BF16)],
        compiler_params=pltpu.CompilerParams(
            dimension_semantics=("parallel",), vmem_limit_bytes=BIG_VMEM_LIMIT),
        name="in_proj",
    )(x2d, g, wt_bf_all, gate_bias_col, cos_t, sin_t, w_up_all, w_out_all, w_down_all)


def _qk_conv_kernel(x_ref, xp_ref, xn_ref, w_ref, k_ref, qt_ref, *, seq_blocks):
    n = CONV_SUB
    pos = pl.program_id(0) % seq_blocks
    has_prev = (pos > 0).astype(F32)
    has_next = (pos < seq_blocks - 1).astype(F32)
    ri = lax.broadcasted_iota(jnp.int32, (n, n), 0)
    ci = lax.broadcasted_iota(jnp.int32, (n, n), 1)
    shift_prev = (ci == ri - 1).astype(BF16)
    shift_next = (ci == ri + 1).astype(BF16)
    rowid = lax.broadcasted_iota(jnp.int32, (n, 1), 0)
    w0 = w_ref[0:1, :]
    w1 = w_ref[1:2, :]
    w2 = w_ref[2:3, :]
    n_sub = CONV_TM // n
    for sb in range(n_sub):
        xs = x_ref[sb * n:(sb + 1) * n, :]
        x_prev = jnp.dot(shift_prev, xs, preferred_element_type=F32)
        x_next = jnp.dot(shift_next, xs, preferred_element_type=F32)
        if sb == 0:
            prev_row = xp_ref[BF16_ROWS - 1:BF16_ROWS, :].astype(F32) * has_prev
        else:
            prev_row = x_ref[sb * n - BF16_ROWS:sb * n, :].astype(F32)[BF16_ROWS - 1:BF16_ROWS]
        if sb == n_sub - 1:
            next_row = xn_ref[0:1, :].astype(F32) * has_next
        else:
            next_row = x_ref[(sb + 1) * n:(sb + 1) * n + BF16_ROWS, :].astype(F32)[0:1]
        x_prev = jnp.where(rowid == 0, prev_row, x_prev)
        x_next = jnp.where(rowid == n - 1, next_row, x_next)
        y = x_prev * w0 + xs.astype(F32) * w1 + x_next * w2
        y = y * _sigmoid(y)
        k_ref[sb * n:(sb + 1) * n, :] = y[:, ML_QK_COLS:].astype(BF16)
        q = y[:, :ML_QK_COLS] * (ML_QK_DIM ** -0.5)
        qt_ref[:, sb * n:(sb + 1) * n] = q.T.astype(BF16)


def _qk_conv(proj, conv_w, seq):
    t = proj.shape[0]
    width = 2 * ML_QK_COLS
    col = C_MQK // width
    per = CONV_TM // BF16_ROWS
    last = t // BF16_ROWS - 1
    return pl.pallas_call(
        functools.partial(_qk_conv_kernel, seq_blocks=seq // CONV_TM),
        grid=(t // CONV_TM,),
        in_specs=[
            pl.BlockSpec((CONV_TM, width), lambda i: (i, col)),
            pl.BlockSpec((BF16_ROWS, width), lambda i: (jnp.maximum(i * per - 1, 0), col)),
            pl.BlockSpec((BF16_ROWS, width), lambda i: (jnp.minimum((i + 1) * per, last), col)),
            pl.BlockSpec((3, width), lambda i: (0, 0)),
        ],
        out_specs=[pl.BlockSpec((CONV_TM, ML_QK_COLS), lambda i: (i, 0)),
                   pl.BlockSpec((ML_QK_COLS, CONV_TM), lambda i: (0, i))],
        out_shape=[jax.ShapeDtypeStruct((t, ML_QK_COLS), BF16),
                   jax.ShapeDtypeStruct((ML_QK_COLS, t), BF16)],
        compiler_params=pltpu.CompilerParams(
            dimension_semantics=("parallel",), vmem_limit_bytes=VMEM_LIMIT),
        name="qk_conv",
    )(proj, proj, proj, conv_w)


def _attn_kernel(sink_ref, q_ref, kc_ref, kp_ref, kn_ref, vc_ref, vp_ref, vn_ref,
                 o_ref, kbuf, vbuf, *, n_blocks):
    w = ATT_WINDOW
    kbuf[0:w] = kp_ref[...]
    kbuf[w:w + ATT_TQ] = kc_ref[...]
    kbuf[w + ATT_TQ:] = kn_ref[...]
    for h in range(ATT_KV_HEADS):
        hs = slice(h * HEAD_DIM, (h + 1) * HEAD_DIM)
        vbuf[h, 0:w, 0:HEAD_DIM] = vp_ref[:, hs]
        vbuf[h, w:w + ATT_TQ, 0:HEAD_DIM] = vc_ref[:, hs]
        vbuf[h, w + ATT_TQ:, 0:HEAD_DIM] = vn_ref[:, hs]
        vbuf[h, :, HEAD_DIM:] = jnp.ones((ATT_TQ + 2 * w, HEAD_DIM), BF16)
    i = pl.program_id(1)
    rows = ATT_GROUP * w
    qi = lax.broadcasted_iota(jnp.int32, (rows, 1), 0) & (w - 1)
    blk = lax.broadcasted_iota(jnp.int32, (rows, 1), 0) // w
    kj = lax.broadcasted_iota(jnp.int32, (rows, 3 * w), 1)
    kj_row = lax.broadcasted_iota(jnp.int32, (1, 3 * w), 1)
    band = jnp.where((kj >= qi) & (kj <= qi + 2 * w), 0.0, ATT_MASKED)
    n_win = ATT_TQ // w
    sinks = []
    for h in range(ATT_KV_HEADS):
        sink = jnp.full((rows, 1), sink_ref[h * ATT_GROUP], F32)
        for g in range(1, ATT_GROUP):
            sink = jnp.where(blk == g, sink_ref[h * ATT_GROUP + g], sink)
        sinks.append(sink * LOG2_E)
    tiles = [(n, h) for n in range(n_win) for h in range(ATT_KV_HEADS)]
    scores = []
    for n, h in tiles:
        qs = jnp.concatenate(
            [q_ref[n * w:(n + 1) * w, (h * ATT_GROUP + g) * HEAD_DIM:(h * ATT_GROUP + g + 1) * HEAD_DIM]
             for g in range(ATT_GROUP)], axis=0)
        kw = kbuf[n * w:(n + 3) * w, h * HEAD_DIM:(h + 1) * HEAD_DIM]
        scores.append(lax.dot_general(qs, kw, _NT, preferred_element_type=F32))
    probs = []
    maxes = []
    for idx, (n, h) in enumerate(tiles):
        bias = band
        if n == 0:
            bias = bias + jnp.where(kj_row < w, jnp.where(i == 0, ATT_MASKED, 0.0), 0.0)
        if n == n_win - 1:
            bias = bias + jnp.where(kj_row >= 2 * w, jnp.where(i == n_blocks - 1, ATT_MASKED, 0.0), 0.0)
        s = scores[idx] + bias
        m = jnp.maximum(jnp.max(s, axis=-1, keepdims=True), sinks[h])
        maxes.append(m)
        probs.append(jnp.exp2(s - m).astype(BF16))
    for idx, (n, h) in enumerate(tiles):
        o_aug = jnp.dot(probs[idx], vbuf[h, n * w:(n + 3) * w, :], preferred_element_type=F32)
        denom = o_aug[:, HEAD_DIM:] + jnp.exp2(sinks[h] - maxes[idx])
        o = o_aug[:, :HEAD_DIM] * (1.0 / denom)
        for g in range(ATT_GROUP):
            col = (h * ATT_GROUP + g) * HEAD_DIM
            o_ref[n * w:(n + 1) * w, col:col + HEAD_DIM] = o[g * w:(g + 1) * w].astype(BF16)


def _attention(proj, sink, batch, seq):
    t = proj.shape[0]
    w = ATT_WINDOW
    nq = seq // ATT_TQ
    per = ATT_TQ // w
    last_blk = t // w - 1
    cur = lambda b, i: b * nq + i
    prev = lambda b, i: jnp.maximum((b * nq + i) * per - 1, 0)
    nxt = lambda b, i: jnp.minimum((b * nq + i + 1) * per, last_blk)
    kcol = C_AK // ATT_KV_COLS
    vcol = C_AV // ATT_KV_COLS
    return pl.pallas_call(
        functools.partial(_attn_kernel, n_blocks=nq),
        grid=(batch, nq),
        in_specs=[
            pl.BlockSpec(memory_space=pltpu.SMEM),
            pl.BlockSpec((ATT_TQ, ATT_Q_COLS), lambda b, i: (cur(b, i), C_AQ // ATT_Q_COLS)),
            pl.BlockSpec((ATT_TQ, ATT_KV_COLS), lambda b, i: (cur(b, i), kcol)),
            pl.BlockSpec((w, ATT_KV_COLS), lambda b, i: (prev(b, i), kcol)),
            pl.BlockSpec((w, ATT_KV_COLS), lambda b, i: (nxt(b, i), kcol)),
            pl.BlockSpec((ATT_TQ, ATT_KV_COLS), lambda b, i: (cur(b, i), vcol)),
            pl.BlockSpec((w, ATT_KV_COLS), lambda b, i: (prev(b, i), vcol)),
            pl.BlockSpec((w, ATT_KV_COLS), lambda b, i: (nxt(b, i), vcol)),
        ],
        out_specs=pl.BlockSpec((ATT_TQ, ATT_Q_COLS), lambda b, i: (cur(b, i), 0)),
        out_shape=jax.ShapeDtypeStruct((t, ATT_Q_COLS), BF16),
        scratch_shapes=[pltpu.VMEM((ATT_TQ + 2 * w, ATT_KV_COLS), BF16),
                        pltpu.VMEM((ATT_KV_HEADS, ATT_TQ + 2 * w, 2 * HEAD_DIM), BF16)],
        compiler_params=pltpu.CompilerParams(
            dimension_semantics=("parallel", "parallel"), vmem_limit_bytes=VMEM_LIMIT),
        name="attn",
    )(sink, proj, proj, proj, proj, proj, proj, proj)


def _rows_to_cols(x):
    length = x.shape[1]
    padded = jnp.concatenate([x, jnp.zeros((length - x.shape[0], length), x.dtype)], axis=0)
    return padded.T


def _ml_gate_rows(gates_f, gates_b, m_prev):
    L = ML_CHUNK
    H = ML_HEADS
    row8 = lax.broadcasted_iota(jnp.int32, (2 * H, L), 0)
    lane = lax.broadcasted_iota(jnp.int32, (2 * H, L), 1)
    is_fwd = row8 < H
    fwd_if = gates_f[0:2 * H]
    bwd_if = gates_b[2 * H:4 * H]
    gi = jnp.where(is_fwd, fwd_if, pltpu.roll(bwd_if, H, 0))
    gf = jnp.where(is_fwd, pltpu.roll(fwd_if, H, 0), bwd_if)
    ls = _log_sigmoid(gf)
    hi = ls.astype(BF16).astype(F32)
    rem = ls - hi
    mid = rem.astype(BF16).astype(F32)
    lo = rem - mid
    parts = jnp.concatenate([hi, mid, lo, jnp.zeros_like(hi)], axis=0).astype(BF16)
    si = lax.broadcasted_iota(jnp.int32, (L, L), 0)
    ti = lax.broadcasted_iota(jnp.int32, (L, L), 1)
    pre = jnp.dot(parts, (si <= ti).astype(BF16), preferred_element_type=F32)
    prefix = pre[0:2 * H] + pre[2 * H:4 * H] + pre[4 * H:6 * H]
    gtot = jnp.sum(ls, axis=1, keepdims=True)
    b = jnp.where(is_fwd, prefix, gtot - prefix + ls)
    r = gi - b
    cm = r
    sh = 1
    while sh < L:
        from_left = jnp.where(lane >= sh, pltpu.roll(cm, sh, 1), -jnp.inf)
        from_right = jnp.where(lane < L - sh, pltpu.roll(cm, L - sh, 1), -jnp.inf)
        cm = jnp.maximum(cm, jnp.where(is_fwd, from_left, from_right))
        sh *= 2
    top = jnp.maximum(m_prev, cm)
    a = gtot - b + gi
    m_loc = jnp.max(a, axis=1, keepdims=True)
    m_new = jnp.maximum(gtot + m_prev, m_loc)
    rows = jnp.concatenate([
        -top,
        jnp.exp(m_prev - top),
        jnp.exp(-(b + top)),
        r,
        jnp.exp(a - m_loc),
        jnp.exp(gtot + m_prev - m_new),
        jnp.exp(m_loc - m_new) + jnp.zeros_like(r),
    ], axis=0)
    return rows, m_new


def _mlstm_kernel(kf_ref, qtf_ref, vtf_ref, gtf_ref, gtfn_ref, kb_ref, qtb_ref, vtb_ref, gtb_ref,
                  gtbn_ref, hf_ref, hb_ref, c_sc, m_sc, rows_sc):
    L = ML_CHUNK
    H = ML_HEADS
    P = ML_PAIRS
    c = pl.program_id(1)
    slot = c % 2

    def tok(d, u):
        lo = u * L if d == 0 else (ML_SUB - 1 - u) * L
        return slice(lo, lo + L)

    def gate_rows(gf_ref, gb_ref, u, m_prev):
        return _ml_gate_rows(gf_ref[:, tok(0, u)], gb_ref[:, tok(1, u)], m_prev)

    @pl.when(c == 0)
    def _():
        c_sc[...] = jnp.zeros_like(c_sc)
        m = jnp.full((P, L), M_INIT, F32)
        for u in range(ML_SUB):
            rows0, m = gate_rows(gtf_ref, gtb_ref, u, m)
            rows_sc[0, u] = rows0
        m_sc[...] = m

    rows_now = [rows_sc[slot, u] for u in range(ML_SUB)]
    m = m_sc[...]
    for u in range(ML_SUB):
        rows_next, m = gate_rows(gtfn_ref, gtbn_ref, u, m)
        rows_sc[1 - slot, u] = rows_next
    m_sc[...] = m

    si = lax.broadcasted_iota(jnp.int32, (L, L), 0)
    ti = lax.broadcasted_iota(jnp.int32, (L, L), 1)
    masks = (si <= ti, si >= ti)
    n_row = lax.broadcasted_iota(jnp.int32, (BF16_ROWS, 1), 0) == 0
    refs = ((kf_ref, qtf_ref, vtf_ref, hf_ref), (kb_ref, qtb_ref, vtb_ref, hb_ref))
    pairs = [(d, j) for d in range(2) for j in range(H)]

    for u in range(ML_SUB):
        rows = rows_now[u]
        bm, w_inter, clamp, r, w_end, s_prev, s_loc = [rows[i * P:(i + 1) * P] for i in range(ML_ROW_KINDS)]
        r_cols = _rows_to_cols(r)
        w_end_cols = _rows_to_cols(w_end)

        def k_of(d, j):
            return refs[d][0][tok(d, u), j * ML_QK_DIM:(j + 1) * ML_QK_DIM]

        def qt_of(d, j):
            return refs[d][1][j * ML_QK_DIM:(j + 1) * ML_QK_DIM, tok(d, u)]

        scores = []
        carried = []
        for d, j in pairs:
            p = d * H + j
            scores.append(jnp.dot(k_of(d, j), qt_of(d, j), preferred_element_type=F32))
            qtw = (qt_of(d, j).astype(F32) * w_inter[p:p + 1, :]).astype(BF16)
            carried.append(jnp.dot(c_sc[p].astype(BF16), qtw, preferred_element_type=F32))
        dens = []
        n_locs = []
        rhss = []
        for d, j in pairs:
            p = d * H + j
            arg = jnp.broadcast_to(r_cols[:, p:p + 1], (L, L)) + bm[p:p + 1, :]
            sc_t = scores[p] * jnp.exp(jnp.where(masks[d], arg, -jnp.inf))
            dens.append(jnp.sum(sc_t, axis=0, keepdims=True))
            kw = k_of(d, j).astype(F32) * jnp.broadcast_to(w_end_cols[:, p:p + 1], (L, ML_QK_DIM))
            n_locs.append(jnp.sum(kw, axis=0, keepdims=True))
            rhss.append(jnp.concatenate([sc_t.astype(BF16), kw.astype(BF16)], axis=1))
        boths = []
        for d, j in pairs:
            p = d * H + j
            vt_j = refs[d][2][j * ML_V_DIM:(j + 1) * ML_V_DIM, tok(d, u)]
            boths.append(jnp.dot(vt_j, rhss[p], preferred_element_type=F32))
        for d, j in pairs:
            p = d * H + j
            num = boths[p][:, :L] + carried[p][:ML_V_DIM]
            den = dens[p] + carried[p][ML_V_DIM:ML_V_DIM + 1]
            refs[d][3][j * ML_V_DIM:(j + 1) * ML_V_DIM, tok(d, u)] = (
                num / jnp.maximum(jnp.abs(den), clamp[p:p + 1, :])).astype(BF16)
            c_prev = c_sc[p]
            sp = s_prev[p:p + 1, :ML_QK_DIM]
            sl = s_loc[p:p + 1, :ML_QK_DIM]
            c_sc[p, 0:ML_V_DIM, :] = sp * c_prev[:ML_V_DIM] + sl * boths[p][:, L:]
            c_sc[p, ML_V_DIM:, :] = sp * c_prev[ML_V_DIM:] + sl * jnp.where(n_row, n_locs[p], 0.0)


def _mlstm(k_conv, qt_conv, projt, gates_t, batch, seq):
    t = k_conv.shape[0]
    blk = ML_SUB * ML_CHUNK
    nc = seq // blk
    fwd = lambda b, c: b * nc + c
    bwd = lambda b, c: b * nc + nc - 1 - c
    nxt = lambda c: jnp.minimum(c + 1, nc - 1)

    def dir_specs(ch):
        return [
            pl.BlockSpec((blk, ML_QK_COLS), lambda b, c: (ch(b, c), 0)),
            pl.BlockSpec((ML_QK_COLS, blk), lambda b, c: (0, ch(b, c))),
            pl.BlockSpec((ML_V_COLS, blk), lambda b, c: (R_MV // ML_V_COLS, ch(b, c))),
            pl.BlockSpec((GATE_COLS, blk), lambda b, c: (0, ch(b, c))),
            pl.BlockSpec((GATE_COLS, blk), lambda b, c: (0, ch(b, nxt(c)))),
        ]

    return pl.pallas_call(
        _mlstm_kernel,
        grid=(batch, nc),
        in_specs=dir_specs(fwd) + dir_specs(bwd),
        out_specs=[pl.BlockSpec((ML_V_COLS, blk), lambda b, c: (0, fwd(b, c))),
                   pl.BlockSpec((ML_V_COLS, blk), lambda b, c: (0, bwd(b, c)))],
        out_shape=[jax.ShapeDtypeStruct((ML_V_COLS, t), BF16),
                   jax.ShapeDtypeStruct((ML_V_COLS, t), BF16)],
        scratch_shapes=[pltpu.VMEM((ML_PAIRS, ML_STATE_ROWS, ML_QK_DIM), F32),
                        pltpu.VMEM((ML_PAIRS, ML_CHUNK), F32),
                        pltpu.VMEM((2, ML_SUB, ML_ROW_KINDS * ML_PAIRS, ML_CHUNK), F32)],
        compiler_params=pltpu.CompilerParams(
            dimension_semantics=("parallel", "arbitrary"), vmem_limit_bytes=VMEM_LIMIT),
        name="mlstm",
    )(k_conv, qt_conv, projt, gates_t, gates_t, k_conv, qt_conv, projt, gates_t, gates_t)


def _mix_kernel(att_ref, hft_ref, hbt_ref, mot_ref, x_ref, w_ref, gml_ref, gpost_ref, o_ref, cat_ref):
    cat_ref[:, 0:ATT_Q_COLS] = att_ref[...]
    for j in range(ML_HEADS):
        sl = slice(j * ML_V_DIM, (j + 1) * ML_V_DIM)
        h = _rms(hft_ref[sl, :].astype(F32) + hbt_ref[sl, :].astype(F32), axis=0)
        gain = jnp.concatenate([gml_ref[sl, :]] * (MIX_TM // LANES), axis=1)
        mem = h * gain * _sigmoid(mot_ref[sl, :].astype(F32))
        cat_ref[:, ATT_Q_COLS + j * ML_V_DIM:ATT_Q_COLS + (j + 1) * ML_V_DIM] = mem.T.astype(BF16)
    mix = jnp.dot(cat_ref[...], w_ref[...], preferred_element_type=F32)
    o_ref[...] = x_ref[...] + _rms(mix) * gpost_ref[...]


def _mix(att, hft, hbt, projt, x2d, w_out_bf, g_ml_b, g_post):
    t = x2d.shape[0]
    steps = t // MIX_TM
    row = lambda i: (i, 0)
    col = lambda i: (0, i)
    const = lambda i: (0, 0)
    return pl.pallas_call(
        _mix_kernel,
        grid=(steps,),
        in_specs=[
            pl.BlockSpec((MIX_TM, ATT_Q_COLS), row),
            pl.BlockSpec((ML_V_COLS, MIX_TM), col),
            pl.BlockSpec((ML_V_COLS, MIX_TM), col),
            pl.BlockSpec((ML_V_COLS, MIX_TM), lambda i: (R_MO // ML_V_COLS, i)),
            pl.BlockSpec((MIX_TM, D_MODEL), row),
            pl.BlockSpec((D_MODEL, D_MODEL), const, pipeline_mode=pl.Buffered(1)),
            pl.BlockSpec((ML_V_COLS, LANES), const),
            pl.BlockSpec((1, D_MODEL), const),
        ],
        out_specs=pl.BlockSpec((MIX_TM, D_MODEL), row),
        out_shape=jax.ShapeDtypeStruct((t, D_MODEL), F32),
        scratch_shapes=[pltpu.VMEM((MIX_TM, D_MODEL), BF16)],
        compiler_params=pltpu.CompilerParams(
            dimension_semantics=("parallel",), vmem_limit_bytes=VMEM_LIMIT),
        name="mix",
    )(att, hft, hbt, projt, x2d, w_out_bf, g_ml_b, g_post)


def _pin_before_next_load(operand_ref, value):
    bits = pltpu.bitcast(value, jnp.uint32)
    tiles = [bits[r:r + SUBLANES, c:c + LANES]
             for r in range(0, value.shape[0], SUBLANES) for c in range(0, value.shape[1], LANES)]
    folded = functools.reduce(jnp.bitwise_or, tiles)
    half = jnp.uint32(16)
    zero = pltpu.bitcast(lax.shift_right_logical(lax.shift_right_logical(folded, half), half), F32)
    zero = jnp.concatenate([zero] * (BF16_ROWS // SUBLANES), axis=0).astype(BF16)
    operand_ref[0:BF16_ROWS, 0:LANES] = operand_ref[0:BF16_ROWS, 0:LANES] + zero


def _mlp_kernel(xn_ref, xp_ref, gpre_ref, wup_ref, wdown_ref, gpost_ref, o_ref,
                h_a, h_b, acc_a, acc_b, *, n_blocks):
    blk = pl.program_id(0) - 1
    j = pl.program_id(1)
    rows = pl.ds(pl.multiple_of(j * MLP_SUB, MLP_SUB), MLP_SUB)

    def pre_norm(h_next):
        h = _rms(xn_ref[...]) * gpre_ref[...]
        h_next[rows, :] = h.astype(BF16)
        return h

    def finish(acc_prev):
        y = xp_ref[...] + _rms(acc_prev[rows, :]) * gpost_ref[...]
        o_ref[...] = y
        return y

    def main(h_cur, h_next, acc_cur, acc_prev):
        n_chunks = MLP_TF // MLP_UP_CHUNK
        us = []
        for c in range(n_chunks):
            cols = slice(c * MLP_UP_CHUNK, (c + 1) * MLP_UP_CHUNK)
            u = jnp.dot(h_cur[...], wup_ref[:, cols], preferred_element_type=F32)
            us.append(jnp.square(jnp.maximum(u, 0.0)).astype(BF16))
            if c == 0:
                _pin_before_next_load(h_cur, finish(acc_prev))
            elif c == 1:
                _pin_before_next_load(h_cur, pre_norm(h_next))
        part = jnp.dot(jnp.concatenate(us, axis=1), wdown_ref[...], preferred_element_type=F32)
        acc_cur[...] = jnp.where(j > 0, acc_cur[...], 0.0) + part

    @pl.when(blk < 0)
    def _():
        pre_norm(h_a)
        acc_a[rows, :] = jnp.zeros((MLP_SUB, D_MODEL), F32)
        acc_b[rows, :] = jnp.zeros((MLP_SUB, D_MODEL), F32)

    in_range = (blk >= 0) & (blk < n_blocks)
    even = (blk % 2) == 0

    @pl.when(in_range & even)
    def _():
        main(h_a, h_b, acc_a, acc_b)

    @pl.when(in_range & jnp.logical_not(even))
    def _():
        main(h_b, h_a, acc_b, acc_a)

    @pl.when(blk == n_blocks)
    def _():
        finish(acc_a if (n_blocks - 1) % 2 == 0 else acc_b)


def _mlp(x2d, g_pre, w_up, w_down, g_post):
    t = x2d.shape[0]
    ni = t // MLP_TM
    nj = D_FF // MLP_TF
    clamp = lambda blk: jnp.clip(blk, 0, ni - 1)
    sub_row = lambda blk, j: (clamp(blk) * nj + j, 0)
    wj = lambda i, j: jnp.where(i < 1, 0, jnp.where(i > ni, nj - 1, j))
    return pl.pallas_call(
        functools.partial(_mlp_kernel, n_blocks=ni),
        grid=(ni + 2, nj),
        in_specs=[
            pl.BlockSpec((MLP_SUB, D_MODEL), lambda i, j: sub_row(i, j)),
            pl.BlockSpec((MLP_SUB, D_MODEL), lambda i, j: sub_row(i - 2, j)),
            pl.BlockSpec((1, D_MODEL), lambda i, j: (0, 0)),
            pl.BlockSpec((D_MODEL, MLP_TF), lambda i, j: (0, wj(i, j))),
            pl.BlockSpec((MLP_TF, D_MODEL), lambda i, j: (wj(i, j), 0)),
            pl.BlockSpec((1, D_MODEL), lambda i, j: (0, 0)),
        ],
        out_specs=pl.BlockSpec((MLP_SUB, D_MODEL), lambda i, j: (jnp.where(i < 2, 0, (i - 2) * nj + j), 0)),
        out_shape=jax.ShapeDtypeStruct((t, D_MODEL), F32),
        scratch_shapes=[pltpu.VMEM((MLP_TM, D_MODEL), BF16), pltpu.VMEM((MLP_TM, D_MODEL), BF16),
                        pltpu.VMEM((MLP_TM, D_MODEL), F32), pltpu.VMEM((MLP_TM, D_MODEL), F32)],
        compiler_params=pltpu.CompilerParams(
            dimension_semantics=("arbitrary", "arbitrary"), vmem_limit_bytes=BIG_VMEM_LIMIT),
        name="mlp",
    )(x2d, x2d, g_pre, w_up, w_down, g_post)


def _rope_tables(seq):
    half = HEAD_DIM // 2
    inv_freq = ROPE_THETA ** (-np.arange(half, dtype=np.float64) / half)
    ang = np.arange(seq, dtype=np.float64)[:, None] * inv_freq[None, :]
    cos = np.cos(ang)
    sin = np.sin(ang)
    cos_t = np.concatenate([cos, cos], axis=1).astype(np.float32)
    sin_t = np.concatenate([-sin, sin], axis=1).astype(np.float32)
    return jnp.asarray(cos_t), jnp.asarray(sin_t)


def kernel(x, w_in, conv_w, gate_bias, ml_norm_g, attn_sink, w_out, g_pre_mix, g_post_mix,
           g_pre_mlp, g_post_mlp, w_up, w_down):
    batch, seq, d = x.shape
    depth = w_in.shape[0]
    cos_t, sin_t = _rope_tables(seq)
    x2d = x.reshape(batch * seq, d)
    w_in_bf = jnp.swapaxes(w_in, 1, 2).astype(BF16)
    for l in range(depth):
        proj, projt, gates_t, w_up_bf, w_out_bf, w_down_bf = _in_proj(
            x2d, g_pre_mix[l][None, :], w_in_bf, gate_bias[l][:, None], cos_t, sin_t,
            w_up, w_out, w_down, l, seq)
        k_conv, qt_conv = _qk_conv(proj, conv_w[l], seq)
        att = _attention(proj, attn_sink[l], batch, seq)
        hft, hbt = _mlstm(k_conv, qt_conv, projt, gates_t, batch, seq)
        g_ml_b = jnp.broadcast_to(ml_norm_g[l][:, None], (ML_V_COLS, LANES))
        x2d = _mix(att, hft, hbt, projt, x2d, w_out_bf, g_ml_b, g_post_mix[l][None, :])
        x2d = _mlp(x2d, g_pre_mlp[l][None, :], w_up_bf, w_down_bf, g_post_mlp[l][None, :])
    return x2d.reshape(batch, seq, d)
```
